```python
import math, functools
import jax, jax.numpy as jnp
from jax import lax
import numpy as np

D_MODEL = 1024
BATCH = 8
SEQ = 4096
DEPTH = 2
DEC_BATCH = 32
DEC_SEQ = 1
PAST_LEN = 16384
PAGE_SIZE = 128

GROUP_DIM = 64
W_A = D_MODEL // 4
G_A = W_A // GROUP_DIM
CHUNK = 128
W_B = 3 * D_MODEL // 8
H_B = W_B // GROUP_DIM
HEAD_B = GROUP_DIM
R_DECAY = 64
R_AAA = 64
R_GATE = 128
B_PROJ = 3 * W_B + R_DECAY + R_AAA + R_GATE
W_C = D_MODEL - W_A - W_B
H_C = W_C // GROUP_DIM
HEAD_C = GROUP_DIM
QBLOCK = 128
A_PROJ = 2 * W_A
C_PROJ = 3 * W_C + H_C
IN_PROJ = A_PROJ + B_PROJ + C_PROJ
MIX = W_A + W_B + W_C
D_FF = ((8 * D_MODEL // 3 + 255) // 256) * 256
NORM_EPS = 1e-6
GN_EPS = 64e-5
F32 = jnp.float32

kernel_name = "hybrid_gmlp_rwkv7_fox_macaron_step"


def rmsnorm(x, g):
    xf = x.astype(F32)
    y = xf * lax.rsqrt(jnp.mean(xf * xf, axis=-1, keepdims=True) + NORM_EPS)
    return (y * g.astype(F32)).astype(x.dtype)


def swiglu(h, w_in, w_out):
    gu = h @ w_in
    return (jax.nn.silu(gu[..., :D_FF]) * gu[..., D_FF:]) @ w_out


def chunk_spatial_gate(za, ws, bs, gain):
    n, L, _ = za.shape
    z = jax.nn.gelu(za)
    u, v = z[..., :W_A], z[..., W_A:]
    v = rmsnorm(v.reshape(n, L, G_A, GROUP_DIM), gain.reshape(G_A, GROUP_DIM))
    pad = (-L) % CHUNK
    nc = (L + pad) // CHUNK
    vp = jnp.pad(v, ((0, 0), (0, pad), (0, 0), (0, 0))).reshape(n, nc, CHUNK, G_A, GROUP_DIM)
    causal = jnp.tril(jnp.ones((CHUNK, CHUNK), dtype=bool))
    wm = ws * causal.astype(ws.dtype)
    s = jnp.einsum('gts,bnsgc->bntgc', wm.astype(vp.dtype), vp) + bs.T[:, :, None].astype(vp.dtype)
    s = s.reshape(n, nc * CHUNK, W_A)[:, :L]
    return u * s, v.reshape(n, L, W_A)


def rwkv7_time_mix(zb, shift_prev, s0, lp):
    n, L, _ = zb.shape
    prev = jnp.concatenate([shift_prev[:, None].astype(zb.dtype), zb[:, :-1]], axis=1)
    zs = zb + (prev - zb) * lp['b_mu'].astype(zb.dtype)
    o = 3 * W_B
    r, k, v = zs[..., :W_B], zs[..., W_B:2 * W_B], zs[..., 2 * W_B:o]
    wl = zs[..., o:o + R_DECAY]
    al = zs[..., o + R_DECAY:o + R_DECAY + R_AAA]
    gl = zs[..., o + R_DECAY + R_AAA:]
    w = -jax.nn.softplus(-(lp['b_w0'] + jnp.tanh(wl) @ lp['b_wB']).astype(F32)) - 0.5
    decay = jnp.exp(-jnp.exp(w))
    a = jax.nn.sigmoid((lp['b_a0'] + al @ lp['b_aB']).astype(F32))
    g = jax.nn.sigmoid(gl) @ lp['b_gB']
    hd = lambda t: t.reshape(n, L, H_B, HEAD_B).astype(F32)
    hp = lambda p: p.reshape(H_B, HEAD_B).astype(F32)
    r, k, v, decay, a = hd(r), hd(k), hd(v), hd(decay), hd(a)
    kk = k * hp(lp['b_kk'])
    kk = kk / jnp.maximum(jnp.sqrt(jnp.sum(kk * kk, axis=-1, keepdims=True)), 1e-12)
    k = k * (1.0 + (a - 1.0) * hp(lp['b_ka']))

    def step(S, xs):
        r_t, w_t, k_t, v_t, kk_t, a_t = xs
        sk = jnp.einsum('bhvk,bhk->bhv', S, kk_t)
        S = (S * w_t[:, :, None, :] - sk[..., None] * (kk_t * a_t)[:, :, None, :]
             + v_t[..., None] * k_t[:, :, None, :])
        return S, jnp.einsum('bhvk,bhk->bhv', S, r_t)

    tm = lambda t: jnp.moveaxis(t, 1, 0)
    S, out = lax.scan(step, s0.astype(F32), (tm(r), tm(decay), tm(k), tm(v), tm(kk), tm(a)))
    out = jnp.moveaxis(out, 0, 1)
    mu = jnp.mean(out, axis=-1, keepdims=True)
    var = jnp.mean((out - mu) ** 2, axis=-1, keepdims=True)
    out = (out - mu) * lax.rsqrt(var + GN_EPS) * hp(lp['b_ln_g']) + hp(lp['b_ln_b'])
    out = out + jnp.sum(r * k * hp(lp['b_rk']), axis=-1, keepdims=True) * v
    y = out.reshape(n, L, W_B).astype(zb.dtype) * g
    return y, zb[:, -1], S.astype(s0.dtype)


def fox_prompt(q, k, v, logf):
    n, L, H, D = q.shape
    nb = L // QBLOCK
    c = jnp.cumsum(logf, axis=1).transpose(0, 2, 1)
    qb = q.reshape(n, nb, QBLOCK, H, D).transpose(1, 0, 2, 3, 4)
    cb = c.reshape(n, H, nb, QBLOCK).transpose(2, 0, 1, 3)
    starts = jnp.arange(nb, dtype=jnp.int32) * QBLOCK
    kpos = jnp.arange(L, dtype=jnp.int32)
    scale = D ** -0.5

    def block(args):
        qi, ci, s0 = args
        s = jnp.einsum('bqhd,bkhd->bhqk', qi, k, preferred_element_type=F32) * scale
        s = s + ci[..., None] - c[:, :, None, :]
        qpos = s0 + jnp.arange(QBLOCK, dtype=jnp.int32)
        s = jnp.where(qpos[:, None] >= kpos[None, :], s, -jnp.inf)
        p = jax.nn.softmax(s, axis=-1)
        return jnp.einsum('bhqk,bkhd->bqhd', p.astype(v.dtype), v)

    o = lax.map(block, (qb, cb, starts))
    return o.transpose(1, 0, 2, 3, 4).reshape(n, L, H * D)


def fox_sample(q, k, v, logf, *, k_past, v_past, logf_past):
    n, T, H, D = q.shape
    P = k_past.shape[1]
    c = jnp.cumsum(jnp.concatenate([logf_past.astype(F32), logf], axis=1), axis=1).transpose(0, 2, 1)
    s = jnp.concatenate([
        jnp.einsum('bthd,bshd->bhts', q, k_past, preferred_element_type=F32),
        jnp.einsum('bthd,bshd->bhts', q, k, preferred_element_type=F32)], axis=-1) * (D ** -0.5)
    s = s + c[:, :, P:, None] - c[:, :, None, :]
    tpos = jnp.arange(T, dtype=jnp.int32)
    spos = jnp.arange(P + T, dtype=jnp.int32) - P
    s = jnp.where(tpos[:, None] >= spos[None, :], s, -jnp.inf)
    p = jax.nn.softmax(s, axis=-1)
    o = (jnp.einsum('bhts,bshd->bthd', p[..., :P].astype(v.dtype), v_past)
         + jnp.einsum('bhts,bshd->bthd', p[..., P:].astype(v.dtype), v))
    return o.reshape(n, T, H * D)


def decoder_layer(x, lp, shift0, wkv0, attend):
    h = rmsnorm(x, lp['norm_g'][0])
    x = x + 0.5 * swiglu(h, lp['w_ffn_in'][0], lp['w_ffn_out'][0])
    h = rmsnorm(x, lp['norm_g'][1])
    z = h @ lp['w_in']
    za = z[..., :A_PROJ]
    zb = z[..., A_PROJ:A_PROJ + B_PROJ]
    zc = z[..., A_PROJ + B_PROJ:]
    ya, va = chunk_spatial_gate(za, lp['a_ws'], lp['a_bs'], lp['a_norm_g'])
    yb, shift, wkv = rwkv7_time_mix(zb, shift0, wkv0, lp)
    n, L, _ = zc.shape
    q = zc[..., :W_C].reshape(n, L, H_C, HEAD_C)
    k = zc[..., W_C:2 * W_C].reshape(n, L, H_C, HEAD_C)
    v = zc[..., 2 * W_C:3 * W_C].reshape(n, L, H_C, HEAD_C)
    logf = jax.nn.log_sigmoid((zc[..., 3 * W_C:] + lp['c_fb']).astype(F32))
    yc = attend(q, k, v, logf)
    x = x + jnp.concatenate([ya, yb, yc], axis=-1) @ lp['w_o']
    h = rmsnorm(x, lp['norm_g'][2])
    x = x + 0.5 * swiglu(h, lp['w_ffn_in'][1], lp['w_ffn_out'][1])
    return x, (va, shift, wkv, k, v, logf.astype(zc.dtype))


def setup_inputs(seed: int = 0) -> dict:
    key = jax.random.key(seed)
    ks = iter(jax.random.split(key, 40))
    nrm = lambda shape, scale: jax.random.normal(next(ks), shape, F32) * scale
    n_pages = PAST_LEN // PAGE_SIZE
    n_used = DEC_BATCH * n_pages
    n_phys = n_used + max(1, n_used // 4)
    x_prompt = nrm((BATCH, SEQ, D_MODEL), 1.0)
    x_sample = nrm((DEC_BATCH, DEC_SEQ, D_MODEL), 1.0)
    cache_k = nrm((DEPTH, n_phys, PAGE_SIZE, H_C, HEAD_C), 1.0)
    cache_v = nrm((DEPTH, n_phys, PAGE_SIZE, H_C, HEAD_C), 1.0)
    cache_logf = jax.nn.log_sigmoid(2.0 + nrm((DEPTH, n_phys, PAGE_SIZE, H_C), 0.5))
    state_wkv = nrm((DEPTH, DEC_BATCH, H_B, HEAD_B, HEAD_B), 0.3)
    state_shift = nrm((DEPTH, DEC_BATCH, B_PROJ), 1.0)
    page_table = jax.random.permutation(next(ks), n_phys)[:n_used].reshape(DEC_BATCH, n_pages).astype(jnp.int32)
    return {
        'x_prompt': x_prompt,
        'x_sample': x_sample,
        'cache_k': cache_k,
        'cache_v': cache_v,
        'cache_logf': cache_logf,
        'state_wkv': state_wkv,
        'state_shift': state_shift,
        'page_table': page_table,
        'norm_g': 1.0 + nrm((DEPTH, 3, D_MODEL), 0.1),
        'w_ffn_in': nrm((DEPTH, 2, D_MODEL, 2 * D_FF), D_MODEL ** -0.5),
        'w_ffn_out': nrm((DEPTH, 2, D_FF, D_MODEL), D_FF ** -0.5),
        'w_in': nrm((DEPTH, D_MODEL, IN_PROJ), D_MODEL ** -0.5),
        'a_ws': nrm((DEPTH, G_A, CHUNK, CHUNK), CHUNK ** -0.5),
        'a_bs': 1.0 + nrm((DEPTH, G_A, CHUNK), 0.1),
        'a_norm_g': 1.0 + nrm((DEPTH, W_A), 0.1),
        'b_mu': jax.random.uniform(next(ks), (DEPTH, B_PROJ), F32),
        'b_w0': -1.0 + nrm((DEPTH, W_B), 0.5),
        'b_wB': nrm((DEPTH, R_DECAY, W_B), 0.1),
        'b_a0': nrm((DEPTH, W_B), 0.1),
        'b_aB': nrm((DEPTH, R_AAA, W_B), 0.5 * R_AAA ** -0.5),
        'b_gB': nrm((DEPTH, R_GATE, W_B), R_GATE ** -0.5),
        'b_kk': 1.0 + nrm((DEPTH, W_B), 0.1),
        'b_ka': 1.0 + nrm((DEPTH, W_B), 0.1),
        'b_rk': nrm((DEPTH, W_B), 0.1),
        'b_ln_g': 1.0 + nrm((DEPTH, W_B), 0.1),
        'b_ln_b': nrm((DEPTH, W_B), 0.01),
        'c_fb': 2.0 + nrm((DEPTH, H_C), 0.5),
        'w_o': nrm((DEPTH, MIX, D_MODEL), 0.5 * MIX ** -0.5),
        'final_norm': 1.0 + nrm((D_MODEL,), 0.1),
    }


def reference(x_prompt, x_sample, cache_k, cache_v, cache_logf, state_wkv, state_shift, page_table,
              norm_g, w_ffn_in, w_ffn_out, w_in, a_ws, a_bs, a_norm_g, b_mu, b_w0, b_wB, b_a0, b_aB,
              b_gB, b_kk, b_ka, b_rk, b_ln_g, b_ln_b, c_fb, w_o, final_norm):
    n_p = x_prompt.shape[0]
    n_s = x_sample.shape[0]
    n_pages = page_table.shape[1]
    page = cache_k.shape[2]
    xp, xs = x_prompt, x_sample
    kp_l, vp_l, lfp_l, wkvp_l, shp_l = [], [], [], [], []
    ks_l, vs_l, lfs_l, wkvs_l, shs_l, va_l = [], [], [], [], [], []
    for l in range(DEPTH):
        lp = dict(norm_g=norm_g[l], w_ffn_in=w_ffn_in[l], w_ffn_out=w_ffn_out[l], w_in=w_in[l],
                  a_ws=a_ws[l], a_bs=a_bs[l], a_norm_g=a_norm_g[l], b_mu=b_mu[l], b_w0=b_w0[l],
                  b_wB=b_wB[l], b_a0=b_a0[l], b_aB=b_aB[l], b_gB=b_gB[l], b_kk=b_kk[l], b_ka=b_ka[l],
                  b_rk=b_rk[l], b_ln_g=b_ln_g[l], b_ln_b=b_ln_b[l], c_fb=c_fb[l], w_o=w_o[l])
        shift0 = jnp.zeros((n_p, B_PROJ), xp.dtype)
        wkv0 = jnp.zeros((n_p, H_B, HEAD_B, HEAD_B), xp.dtype)
        xp, (_, shp, wkvp, kp, vp, lfp) = decoder_layer(xp, lp, shift0, wkv0, fox_prompt)
        kp_l.append(kp); vp_l.append(vp); lfp_l.append(lfp); wkvp_l.append(wkvp); shp_l.append(shp)
        k_past = cache_k[l][page_table].reshape(n_s, n_pages * page, H_C, HEAD_C)
        v_past = cache_v[l][page_table].reshape(n_s, n_pages * page, H_C, HEAD_C)
        lf_past = cache_logf[l][page_table].reshape(n_s, n_pages * page, H_C)
        attend = functools.partial(fox_sample, k_past=k_past, v_past=v_past, logf_past=lf_past)
        xs, (va, shs, wkvs, kss, vss, lfs) = decoder_layer(xs, lp, state_shift[l], state_wkv[l], attend)
        ks_l.append(kss); vs_l.append(vss); lfs_l.append(lfs); wkvs_l.append(wkvs); shs_l.append(shs)
        va_l.append(va)
    y_prompt = rmsnorm(xp, final_norm)
    y_sample = rmsnorm(xs, final_norm)
    return (y_prompt, y_sample,
            jnp.stack(kp_l), jnp.stack(vp_l), jnp.stack(lfp_l), jnp.stack(wkvp_l), jnp.stack(shp_l),
            jnp.stack(ks_l), jnp.stack(vs_l), jnp.stack(lfs_l), jnp.stack(wkvs_l), jnp.stack(shs_l),
            jnp.stack(va_l))
```

```python
import functools

import jax
import jax.numpy as jnp
from jax import lax
from jax.experimental import pallas as pl
from jax.experimental.pallas import tpu as pltpu

F32 = jnp.float32
BF16 = jnp.bfloat16
HIGHEST = lax.Precision.HIGHEST

LANES = 128
D_MODEL = 1024
D_FF = 2816
GROUP = 64
W_A = 256
W_B = 384
W_C = 384
H_B = W_B // GROUP
H_C = W_C // GROUP
N_PAIR = W_B // LANES
R_DECAY = 64
R_AAA = 64
R_GATE = 128
B_PROJ = 3 * W_B + R_DECAY + R_AAA + R_GATE
A_PROJ = 2 * W_A
C_PROJ = 3 * W_C + H_C
IN_PROJ = A_PROJ + B_PROJ + C_PROJ
IN_PROJ_PAD = A_PROJ + B_PROJ + 3 * W_C + LANES
CHUNK_A = 128
CHUNK_B = 64
NORM_EPS = 1e-6
GN_EPS = 64e-5
NEG_BIG = -1e30
VMEM_LIMIT = 56 << 20


def _params(n_axes, vmem=VMEM_LIMIT):
    return pltpu.CompilerParams(dimension_semantics=("arbitrary",) * n_axes,
                                vmem_limit_bytes=vmem)


def _sigmoid(x):
    return 1.0 / (1.0 + jnp.exp(-x))


def _softplus(x):
    return jnp.maximum(x, 0.0) + jnp.log(1.0 + jnp.exp(-jnp.abs(x)))


def _gelu_tanh(x):
    return 0.5 * x * (1.0 + jnp.tanh(0.7978845608028654 * (x + 0.044715 * (x * x * x))))


def _rms(x, g):
    return x * lax.rsqrt(jnp.mean(x * x, axis=-1, keepdims=True) + NORM_EPS) * g


def _dot(a, b):
    return jnp.dot(a, b, preferred_element_type=F32)


def _dot_nt(a, b):
    return lax.dot_general(a, b, (((1,), (1,)), ((), ())), preferred_element_type=F32)


def _dot_split(a, b_bf):
    hi = a.astype(BF16)
    lo = (a - hi.astype(F32)).astype(BF16)
    return _dot(hi, b_bf) + _dot(lo, b_bf)


def _iota(shape, dim):
    return lax.broadcasted_iota(jnp.int32, shape, dim)


def _ffn_kernel(*refs, n_ff, final):
    if final:
        x_ref, g_ref, wg_ref, wu_ref, wo_ref, fg_ref, o_ref, h_scr, acc_scr = refs
    else:
        x_ref, g_ref, wg_ref, wu_ref, wo_ref, o_ref, h_scr, acc_scr = refs
    j = pl.program_id(1)

    @pl.when(j == 0)
    def _():
        h_scr[...] = _rms(x_ref[...], g_ref[...]).astype(BF16)
        acc_scr[...] = jnp.zeros_like(acc_scr)

    h = h_scr[...]
    gate = _dot(h, wg_ref[...])
    up = _dot(h, wu_ref[...])
    act = (gate * _sigmoid(gate) * up).astype(BF16)
    acc_scr[...] += _dot(act, wo_ref[...])

    @pl.when(j == n_ff - 1)
    def _():
        y = x_ref[...] + 0.5 * acc_scr[...]
        if final:
            y = _rms(y, fg_ref[...])
        o_ref[...] = y


def _ffn(x, g, w_in, w_out, final_g=None, tm=512, n_ff=2):
    m = x.shape[0]
    tm = min(tm, m)
    tf = D_FF // n_ff
    final = final_g is not None
    in_specs = [
        pl.BlockSpec((tm, D_MODEL), lambda i, j: (i, 0)),
        pl.BlockSpec((1, D_MODEL), lambda i, j: (0, 0)),
        pl.BlockSpec((D_MODEL, tf), lambda i, j: (0, j)),
        pl.BlockSpec((D_MODEL, tf), lambda i, j: (0, j + n_ff)),
        pl.BlockSpec((tf, D_MODEL), lambda i, j: (j, 0)),
    ]
    args = [x, g.reshape(1, D_MODEL), w_in, w_in, w_out]
    if final:
        in_specs.append(pl.BlockSpec((1, D_MODEL), lambda i, j: (0, 0)))
        args.append(final_g.reshape(1, D_MODEL))
    return pl.pallas_call(
        functools.partial(_ffn_kernel, n_ff=n_ff, final=final),
        grid=(m // tm, n_ff),
        in_specs=in_specs,
        out_specs=pl.BlockSpec((tm, D_MODEL), lambda i, j: (i, 0)),
        out_shape=jax.ShapeDtypeStruct((m, D_MODEL), F32),
        scratch_shapes=[pltpu.VMEM((tm, D_MODEL), BF16), pltpu.VMEM((tm, D_MODEL), F32)],
        compiler_params=_params(2),
        name="ffn",
    )(*args)


def _inproj_kernel(x_ref, g_ref, w_ref, fb_ref, za_ref, zb_ref, q_ref, k_ref, v_ref, lf_ref):
    h = _rms(x_ref[...], g_ref[...]).astype(BF16)
    z = _dot(h, w_ref[...])
    o = A_PROJ
    za_ref[...] = z[:, :o]
    zb_ref[...] = z[:, o:o + B_PROJ]
    o += B_PROJ
    q_ref[...] = (z[:, o:o + W_C] * (GROUP ** -0.5)).astype(BF16)
    k_ref[...] = z[:, o + W_C:o + 2 * W_C]
    v_ref[...] = z[:, o + 2 * W_C:o + 3 * W_C]
    lf_ref[...] = -_softplus(-(z[:, o + 3 * W_C:] + fb_ref[...]))


def _inproj(x, g, w_pad, fb_pad, tm=512):
    m = x.shape[0]
    tm = min(tm, m)
    row = lambda i: (i, 0)
    fix = lambda i: (0, 0)
    widths = (A_PROJ, B_PROJ, W_C, W_C, W_C, LANES)
    dtypes = (F32, F32, BF16, F32, F32, F32)
    return pl.pallas_call(
        _inproj_kernel,
        grid=(m // tm,),
        in_specs=[pl.BlockSpec((tm, D_MODEL), row), pl.BlockSpec((1, D_MODEL), fix),
                  pl.BlockSpec((D_MODEL, IN_PROJ_PAD), fix), pl.BlockSpec((1, LANES), fix)],
        out_specs=[pl.BlockSpec((tm, w), row) for w in widths],
        out_shape=[jax.ShapeDtypeStruct((m, w), d) for w, d in zip(widths, dtypes)],
        compiler_params=_params(1),
        name="inproj",
    )(x, g.reshape(1, D_MODEL), w_pad, fb_pad)


def _gmlp_kernel(za_ref, gain_ref, ws_ref, bias_ref, avg_ref, ya_ref, va_ref, *, n_chunks):
    z = _gelu_tanh(za_ref[...])
    u = z[:, :W_A]
    v = z[:, W_A:]
    ms = _dot((v * v).astype(BF16), avg_ref[...])
    vn = v * lax.rsqrt(ms + NORM_EPS) * gain_ref[...]
    va_ref[...] = vn
    causal = _iota((CHUNK_A, CHUNK_A), 0) >= _iota((CHUNK_A, CHUNK_A), 1)
    lane_group = _iota((CHUNK_A, W_A), 1) // GROUP
    wm = [jnp.where(causal, ws_ref[g], 0.0).astype(BF16) for g in range(W_A // GROUP)]
    for c in range(n_chunks):
        rows = slice(c * CHUNK_A, (c + 1) * CHUNK_A)
        vc = vn[rows]
        s = bias_ref[...]
        for g in range(W_A // GROUP):
            s = s + _dot(wm[g], jnp.where(lane_group == g, vc, 0.0).astype(BF16))
        ya_ref[rows, :] = (u[rows] * s).astype(BF16)


def _gmlp(za, gain, ws, bias_full, avg_a, tm=512):
    m = za.shape[0]
    tm = min(tm, m)
    row = lambda i: (i, 0)
    fix = lambda i: (0, 0)
    return pl.pallas_call(
        functools.partial(_gmlp_kernel, n_chunks=tm // CHUNK_A),
        grid=(m // tm,),
        in_specs=[pl.BlockSpec((tm, A_PROJ), row), pl.BlockSpec((1, W_A), fix),
                  pl.BlockSpec((W_A // GROUP, CHUNK_A, CHUNK_A), lambda i: (0, 0, 0)),
                  pl.BlockSpec((CHUNK_A, W_A), fix), pl.BlockSpec((W_A, W_A), fix)],
        out_specs=[pl.BlockSpec((tm, W_A), row), pl.BlockSpec((tm, W_A), row)],
        out_shape=[jax.ShapeDtypeStruct((m, W_A), BF16), jax.ShapeDtypeStruct((m, W_A), F32)],
        compiler_params=_params(1),
        name="gmlp",
    )(za, gain.reshape(1, W_A), ws, bias_full, avg_a)


def _rwkv_prep_math(zb, prev, vec_ref, wb_ref, ab_ref, gb_ref, ones_ref, outs):
    r_ref, lw_ref, k_ref, v_ref, kk_ref, beta_ref, g_ref, bonus_ref = outs
    mu = vec_ref[0:1, :]
    zs = zb + (prev - zb) * mu
    r = zs[:, :W_B]
    k = zs[:, W_B:2 * W_B]
    v = zs[:, 2 * W_B:3 * W_B]
    lora = zs[:, 3 * W_B:3 * W_B + LANES]
    gl = zs[:, 3 * W_B + LANES:]
    w0 = vec_ref[1:2, :W_B]
    a0 = vec_ref[2:3, :W_B]
    kkw = vec_ref[3:4, :W_B]
    kaw = vec_ref[4:5, :W_B]
    rkw = vec_ref[5:6, :W_B]
    w = -_softplus(-(w0 + _dot(jnp.tanh(lora).astype(BF16), wb_ref[...]))) - 0.5
    a = _sigmoid(a0 + _dot(lora.astype(BF16), ab_ref[...]))
    g = _dot(_sigmoid(gl).astype(BF16), gb_ref[...])
    kk = k * kkw
    ss = _dot_split(kk * kk, ones_ref[...])
    kk = kk / jnp.maximum(jnp.sqrt(ss), 1e-12)
    k2 = k * (1.0 + (a - 1.0) * kaw)
    r_ref[...] = r
    lw_ref[...] = -jnp.exp(w)
    k_ref[...] = k2
    v_ref[...] = v
    kk_ref[...] = kk
    beta_ref[...] = kk * a
    g_ref[...] = g
    bonus_ref[...] = _dot_split(r * k2 * rkw, ones_ref[...]) * v


def _rwkv_prep_seq_kernel(zb_ref, pb_ref, vec_ref, wb_ref, ab_ref, gb_ref, ones_ref, *outs,
                          blocks_per_seq):
    i = pl.program_id(0)
    zb = zb_ref[...]
    tm = zb.shape[0]
    first = (i % blocks_per_seq) == 0
    last_prev = jnp.where(first, 0.0, pb_ref[7:8, :])
    prev = jnp.where(_iota((tm, 1), 0) == 0, last_prev, pltpu.roll(zb, shift=1, axis=0))
    _rwkv_prep_math(zb, prev, vec_ref, wb_ref, ab_ref, gb_ref, ones_ref, outs)


def _rwkv_prep_tok_kernel(zb_ref, prev_ref, vec_ref, wb_ref, ab_ref, gb_ref, ones_ref, *outs):
    _rwkv_prep_math(zb_ref[...], prev_ref[...], vec_ref, wb_ref, ab_ref, gb_ref, ones_ref, outs)


def _rwkv_prep(zb, prev, seq_len, wts, tm=512):
    m = zb.shape[0]
    tm = min(tm, m)
    row = lambda i: (i, 0)
    fix = lambda i: (0, 0)
    w_specs = [pl.BlockSpec((8, B_PROJ), fix), pl.BlockSpec((LANES, W_B), fix),
               pl.BlockSpec((LANES, W_B), fix), pl.BlockSpec((R_GATE, W_B), fix),
               pl.BlockSpec((W_B, W_B), fix)]
    w_args = [wts["b_vec"], wts["b_wB"], wts["b_aB"], wts["b_gB"], wts["ones_b"]]
    if prev is None:
        kern = functools.partial(_rwkv_prep_seq_kernel, blocks_per_seq=seq_len // tm)
        sub = tm // 8
        in_specs = [pl.BlockSpec((tm, B_PROJ), row),
                    pl.BlockSpec((8, B_PROJ), lambda i: (jnp.maximum(i * sub - 1, 0), 0))]
        args = [zb, zb]
    else:
        kern = _rwkv_prep_tok_kernel
        in_specs = [pl.BlockSpec((tm, B_PROJ), row), pl.BlockSpec((tm, B_PROJ), row)]
        args = [zb, prev]
    return pl.pallas_call(
        kern,
        grid=(m // tm,),
        in_specs=in_specs + w_specs,
        out_specs=[pl.BlockSpec((tm, W_B), row)] * 8,
        out_shape=[jax.ShapeDtypeStruct((m, W_B), F32)] * 8,
        compiler_params=_params(1),
        name="rwkv_prep",
    )(*args, *w_args)


def _stack(x, low):
    return jnp.concatenate([jnp.where(low, x, 0.0), jnp.where(low, 0.0, x)], axis=0)


def _rwkv_chunk_kernel(r_ref, lw_ref, k_ref, v_ref, kk_ref, beta_ref,
                       x1_ref, x2_ref, ub_ref, op_ref, sp_ref, gam_ref):
    c = CHUNK_B
    tri = (_iota((c, c), 0) >= _iota((c, c), 1)).astype(F32)
    lw = lw_ref[...]
    cum = jnp.dot(tri, lw, precision=HIGHEST, preferred_element_type=F32)
    cum_last = cum[c - 1:c, :]
    e_pos = jnp.exp(cum)
    e_neg = jnp.exp(-cum)
    e_prev = jnp.exp(cum - lw)
    e_tail = jnp.exp(cum_last - cum)
    r_t = r_ref[...] * e_pos
    kap_t = kk_ref[...] * e_prev
    beta_h = beta_ref[...] * e_neg
    k_h = k_ref[...] * e_neg
    beta_c = beta_ref[...] * e_tail
    k_c = k_ref[...] * e_tail
    v = v_ref[...]
    gam_ref[0] = jnp.exp(cum_last)

    n2 = 2 * c
    low = _iota((c, LANES), 1) < GROUP
    rr = _iota((n2, n2), 0) & (c - 1)
    cc = _iota((n2, n2), 1) & (c - 1)
    strict = rr > cc
    incl = rr >= cc
    eye = _iota((n2, n2), 0) == _iota((n2, n2), 1)
    eye_f = eye.astype(F32)
    eye_bf = eye_f.astype(BF16)
    for p in range(N_PAIR):
        ls = slice(p * LANES, (p + 1) * LANES)
        kap_s = _stack(kap_t[:, ls], low)
        r_s = _stack(r_t[:, ls], low)
        beta_s = _stack(beta_h[:, ls], low)
        k_s = _stack(k_h[:, ls], low)
        v_s = _stack(v[:, ls], low).astype(BF16)
        betac_s = _stack(beta_c[:, ls], low).astype(BF16)
        kc_s = _stack(k_c[:, ls], low).astype(BF16)
        lhs = jnp.concatenate([kap_s, r_s], axis=0).astype(BF16)
        rhs = jnp.concatenate([beta_s, k_s], axis=0).astype(BF16)
        gram = _dot_nt(lhs, rhs)
        n_mat = jnp.where(strict, gram[:n2, :n2], 0.0)
        a_kk = jnp.where(strict, gram[:n2, n2:], 0.0)
        a_rb = jnp.where(incl, gram[n2:, :n2], 0.0)
        a_rk = jnp.where(incl, gram[n2:, n2:], 0.0)
        inv = eye_f - n_mat
        pw = n_mat
        for _ in range(5):
            pw_bf = pw.astype(BF16)
            pw = _dot(pw_bf, pw_bf)
            inv = inv + _dot(inv.astype(BF16), pw.astype(BF16))
        inv_bf = inv.astype(BF16)
        w_s = _dot(inv_bf, kap_s.astype(BF16))
        av = _dot(a_kk.astype(BF16), v_s)
        ub_ref[0, p] = -_dot(inv_bf, av.astype(BF16))
        op_ref[0, p] = _dot(a_rk.astype(BF16), v_s)
        sp_ref[0, p] = _dot(_dot_nt(eye_bf, kc_s).astype(BF16), v_s)
        betac_t = _dot_nt(eye_bf, betac_s)
        x1_ref[0, p] = jnp.concatenate([w_s, r_s], axis=0).astype(BF16)
        x2_ref[0, p] = jnp.concatenate([a_rb, betac_t], axis=0).astype(BF16)


def _rwkv_chunks(r, lw, k2, v, kk, beta):
    m = r.shape[0]
    nc = m // CHUNK_B
    row = lambda i: (i, 0)
    blk = lambda i: (i, 0, 0, 0)
    t = 2 * CHUNK_B
    return pl.pallas_call(
        _rwkv_chunk_kernel,
        grid=(nc,),
        in_specs=[pl.BlockSpec((CHUNK_B, W_B), row)] * 6,
        out_specs=[pl.BlockSpec((1, N_PAIR, 2 * t, LANES), blk),
                   pl.BlockSpec((1, N_PAIR, 2 * t, LANES), blk),
                   pl.BlockSpec((1, N_PAIR, t, LANES), blk),
                   pl.BlockSpec((1, N_PAIR, t, LANES), blk),
                   pl.BlockSpec((1, N_PAIR, t, LANES), blk),
                   pl.BlockSpec((1, 1, W_B), lambda i: (i, 0, 0))],
        out_shape=[jax.ShapeDtypeStruct((nc, N_PAIR, 2 * t, LANES), BF16),
                   jax.ShapeDtypeStruct((nc, N_PAIR, 2 * t, LANES), BF16),
                   jax.ShapeDtypeStruct((nc, N_PAIR, t, LANES), F32),
                   jax.ShapeDtypeStruct((nc, N_PAIR, t, LANES), F32),
                   jax.ShapeDtypeStruct((nc, N_PAIR, t, LANES), F32),
                   jax.ShapeDtypeStruct((nc, 1, W_B), F32)],
        compiler_params=_params(1),
        name="rwkv_chunks",
    )(r, lw, k2, v, kk, beta)


def _rwkv_scan_kernel(x1_ref, x2_ref, ub_ref, op_ref, sp_ref, gam_ref, o_ref, st_ref, st_scr,
                      *, n_seq, n_chunks):
    ci = pl.program_id(0)
    t = 2 * CHUNK_B

    @pl.when(ci == 0)
    def _():
        st_scr[...] = jnp.zeros_like(st_scr)

    eye = _iota((t, t), 0) == _iota((t, t), 1)
    for b in range(n_seq):
        for p in range(N_PAIR):
            st = st_scr[b * N_PAIR + p]
            y = _dot(x1_ref[b, 0, p], st.astype(BF16))
            u = ub_ref[b, 0, p] - y[:t]
            z = _dot(x2_ref[b, 0, p], u.astype(BF16))
            o_s = op_ref[b, 0, p] + y[t:] + z[:t]
            gam_row = gam_ref[b, 0, :, p * LANES:(p + 1) * LANES]
            gam_col = jnp.sum(jnp.where(eye, gam_row, 0.0), axis=1, keepdims=True)
            st_scr[b * N_PAIR + p] = gam_col * st + z[t:] + sp_ref[b, 0, p]
            o_ref[b, :, p * LANES:(p + 1) * LANES] = o_s[:CHUNK_B] + o_s[CHUNK_B:]

    @pl.when(ci == n_chunks - 1)
    def _():
        st_ref[...] = st_scr[...]


def _rwkv_scan(x1, x2, ub, op, sp, gam, n_seq, seq_len):
    nc = seq_len // CHUNK_B
    t = 2 * CHUNK_B
    r5 = lambda a: a.reshape((n_seq, nc) + a.shape[1:])
    blk5 = lambda rows: pl.BlockSpec((n_seq, 1, N_PAIR, rows, LANES), lambda c: (0, c, 0, 0, 0))
    return pl.pallas_call(
        functools.partial(_rwkv_scan_kernel, n_seq=n_seq, n_chunks=nc),
        grid=(nc,),
        in_specs=[blk5(2 * t), blk5(2 * t), blk5(t), blk5(t), blk5(t),
                  pl.BlockSpec((n_seq, 1, 1, W_B), lambda c: (0, c, 0, 0))],
        out_specs=[pl.BlockSpec((n_seq, CHUNK_B, W_B), lambda c: (0, c, 0)),
                   pl.BlockSpec((n_seq * N_PAIR, t, LANES), lambda c: (0, 0, 0))],
        out_shape=[jax.ShapeDtypeStruct((n_seq, seq_len, W_B), F32),
                   jax.ShapeDtypeStruct((n_seq * N_PAIR, t, LANES), F32)],
        scratch_shapes=[pltpu.VMEM((n_seq * N_PAIR, t, LANES), F32)],
        compiler_params=_params(1),
        name="rwkv_scan",
    )(r5(x1), r5(x2), r5(ub), r5(op), r5(sp), r5(gam))


def _rwkv_step_kernel(s_ref, r_ref, lw_ref, k_ref, kk_ref, beta_ref, vcol_ref, o_ref, so_ref):
    for h in range(H_B):
        s = s_ref[0, h]
        sk = jnp.sum(s * kk_ref[0, h], axis=1, keepdims=True)
        s_new = s * jnp.exp(lw_ref[0, h]) - sk * beta_ref[0, h] + vcol_ref[0, h] * k_ref[0, h]
        so_ref[0, h] = s_new
        o_ref[0, h] = jnp.sum(s_new * r_ref[0, h], axis=1, keepdims=True)


def _rwkv_step(state, r, lw, k2, kk, beta, v):
    n = state.shape[0]
    rowv = lambda a: a.reshape(n, H_B, 1, GROUP)
    idx = lambda b: (b, 0, 0, 0)
    row_spec = pl.BlockSpec((1, H_B, 1, GROUP), idx)
    col_spec = pl.BlockSpec((1, H_B, GROUP, 1), idx)
    mat_spec = pl.BlockSpec((1, H_B, GROUP, GROUP), idx)
    o, s_new = pl.pallas_call(
        _rwkv_step_kernel,
        grid=(n,),
        in_specs=[mat_spec] + [row_spec] * 5 + [col_spec],
        out_specs=[col_spec, mat_spec],
        out_shape=[jax.ShapeDtypeStruct((n, H_B, GROUP, 1), F32),
                   jax.ShapeDtypeStruct((n, H_B, GROUP, GROUP), F32)],
        compiler_params=_params(1),
        name="rwkv_step",
    )(state, rowv(r), rowv(lw), rowv(k2), rowv(kk), rowv(beta), v.reshape(n, H_B, GROUP, 1))
    return o.reshape(n, W_B), s_new


def _cumsum_kernel(lf_ref, c_ref, ct_ref, carry):
    @pl.when(pl.program_id(1) == 0)
    def _():
        carry[...] = jnp.zeros_like(carry)

    tb = lf_ref.shape[0]
    tri = (_iota((tb, tb), 0) >= _iota((tb, tb), 1)).astype(F32)
    c = jnp.dot(tri, lf_ref[...], precision=HIGHEST, preferred_element_type=F32) + carry[...]
    c_ref[...] = c
    ct_ref[0] = c.T[0:8, :]
    carry[...] = c[tb - 1:tb, :]


def _cumsum(lf, n_seq, seq_len, tb=256):
    nb = seq_len // tb
    return pl.pallas_call(
        _cumsum_kernel,
        grid=(n_seq, nb),
        in_specs=[pl.BlockSpec((tb, LANES), lambda b, j: (b * nb + j, 0))],
        out_specs=[pl.BlockSpec((tb, LANES), lambda b, j: (b * nb + j, 0)),
                   pl.BlockSpec((1, 8, tb), lambda b, j: (b, 0, j))],
        out_shape=[jax.ShapeDtypeStruct((n_seq * seq_len, LANES), F32),
                   jax.ShapeDtypeStruct((n_seq, 8, seq_len), F32)],
        scratch_shapes=[pltpu.VMEM((1, LANES), F32)],
        compiler_params=_params(2),
        name="logf_cumsum",
    )(lf)


def _fox_prompt_kernel(q_ref, k_ref, v_ref, c_ref, ck0_ref, ck1_ref, o_ref,
                       qm_scr, cq_scr, m_scr, acc_scr):
    p = pl.program_id(1)
    i = pl.program_id(2)
    j = pl.program_id(3)
    tq = q_ref.shape[0]
    tk = k_ref.shape[0]
    low_q = _iota((tq, LANES), 1) < GROUP

    @pl.when(j == 0)
    def _():
        q = q_ref[...]
        zero = jnp.zeros_like(q)
        qm_scr[0] = jnp.where(low_q, q, zero)
        qm_scr[1] = jnp.where(low_q, zero, q)
        lane = _iota((tq, LANES), 1)
        c = c_ref[...]
        for hh in range(2):
            cq_scr[hh] = jnp.sum(jnp.where(lane == 2 * p + hh, c, 0.0), axis=1, keepdims=True)
        m_scr[...] = jnp.full_like(m_scr, NEG_BIG)
        acc_scr[...] = jnp.zeros_like(acc_scr)

    def step(diagonal):
        kb = k_ref[...].astype(BF16)
        v = v_ref[...]
        low_k = _iota((tk, LANES), 1) < GROUP
        cks = (ck0_ref[...], ck1_ref[...])
        for hh in range(2):
            s = _dot_nt(qm_scr[hh], kb) + (cq_scr[hh] - cks[hh])
            if diagonal:
                s = jnp.where(_iota((tq, tk), 0) >= _iota((tq, tk), 1), s, NEG_BIG)
            m_prev = m_scr[hh]
            m_new = jnp.maximum(m_prev, jnp.max(s, axis=1, keepdims=True))
            alpha = jnp.exp(m_prev - m_new)
            pr = jnp.exp(s - m_new).astype(BF16)
            vm = jnp.where(low_k == (hh == 0), v, 1.0).astype(BF16)
            acc_scr[hh] = alpha * acc_scr[hh] + _dot(pr, vm)
            m_scr[hh] = m_new

    @pl.when(j < i)
    def _():
        step(False)

    @pl.when(j == i)
    def _():
        step(True)
        a0 = acc_scr[0]
        a1 = acc_scr[1]
        out = jnp.where(low_q, a0 / a0[:, LANES - 1:LANES], a1 / a1[:, 0:1])
        o_ref[...] = out.astype(BF16)


def _fox_prompt(q, k, v, c, ct, n_seq, seq_len, tq=512):
    nq = seq_len // tq
    ct4 = ct.reshape(n_seq, 8, 1, seq_len)
    kv_spec = pl.BlockSpec((tq, LANES), lambda b, p, i, j: (b * nq + jnp.minimum(j, i), p))
    ck_spec = lambda hh: pl.BlockSpec((None, None, 1, tq),
                                      lambda b, p, i, j: (b, 2 * p + hh, 0, jnp.minimum(j, i)))
    return pl.pallas_call(
        _fox_prompt_kernel,
        grid=(n_seq, N_PAIR, nq, nq),
        in_specs=[pl.BlockSpec((tq, LANES), lambda b, p, i, j: (b * nq + i, p)),
                  kv_spec, kv_spec,
                  pl.BlockSpec((tq, LANES), lambda b, p, i, j: (b * nq + i, 0)),
                  ck_spec(0), ck_spec(1)],
        out_specs=pl.BlockSpec((tq, LANES), lambda b, p, i, j: (b * nq + i, p)),
        out_shape=jax.ShapeDtypeStruct((n_seq * seq_len, W_C), BF16),
        scratch_shapes=[pltpu.VMEM((2, tq, LANES), BF16), pltpu.VMEM((2, tq, 1), F32),
                        pltpu.VMEM((2, tq, 1), F32), pltpu.VMEM((2, tq, LANES), F32)],
        compiler_params=_params(4),
        name="fox_prompt",
    )(q, k, v, c, ct4, ct4)


HEAD_ROWS = 16


def _fox_sample_kernel(pt_ref, q_ref, kn_ref, vn_ref, lfn_ref, *refs, pages, n_steps):
    k_refs = refs[:pages]
    v_refs = refs[pages:2 * pages]
    lf_refs = refs[2 * pages:3 * pages]
    o_ref, qf_scr, qb_scr, m_scr, l_scr, acc_scr, carry = refs[3 * pages:]
    s_id = pl.program_id(1)
    page = k_refs[0].shape[0]
    head_of_lane = _iota((HEAD_ROWS, W_C), 1) // GROUP
    own = head_of_lane == _iota((HEAD_ROWS, W_C), 0)

    @pl.when(s_id == 0)
    def _():
        qrows = jnp.where(own, q_ref[0].astype(F32), 0.0)
        qf_scr[...] = qrows
        qb_scr[...] = qrows.astype(BF16)
        m_scr[...] = jnp.full_like(m_scr, NEG_BIG)
        l_scr[...] = jnp.zeros_like(l_scr)
        acc_scr[...] = jnp.zeros_like(acc_scr)
        carry[...] = jnp.zeros_like(carry)

    upto = _iota((page, page), 0) <= _iota((page, page), 1)
    sub = _iota((HEAD_ROWS, page), 0)
    for u in range(pages):
        lf = lf_refs[u][...]
        ct = jnp.zeros((HEAD_ROWS, page), F32)
        for h in range(H_C):
            row = jnp.sum(jnp.where(upto, lf[:, h:h + 1], 0.0), axis=0, keepdims=True)
            ct = ct + jnp.where(sub == h, row, 0.0)
        ct = ct + carry[...]
        carry[...] = ct[:, page - 1:page]
        s = _dot_nt(qb_scr[...], k_refs[u][...].astype(BF16)) - ct
        m_prev = m_scr[...]
        m_new = jnp.maximum(m_prev, jnp.max(s, axis=1, keepdims=True))
        alpha = jnp.exp(m_prev - m_new)
        pr = jnp.exp(s - m_new)
        l_scr[...] = alpha * l_scr[...] + jnp.sum(pr, axis=1, keepdims=True)
        acc_scr[...] = alpha * acc_scr[...] + _dot(pr.astype(BF16), v_refs[u][...].astype(BF16))
        m_scr[...] = m_new

    @pl.when(s_id == n_steps - 1)
    def _():
        s_new = (jnp.sum(qf_scr[...] * kn_ref[0], axis=1, keepdims=True)
                 - (carry[...] + lfn_ref[0]))
        m_prev = m_scr[...]
        m_new = jnp.maximum(m_prev, s_new)
        alpha = jnp.exp(m_prev - m_new)
        pn = jnp.exp(s_new - m_new)
        l_fin = alpha * l_scr[...] + pn
        acc = alpha * acc_scr[...] + pn * vn_ref[0]
        o_ref[0] = jnp.sum(jnp.where(own, acc / l_fin, 0.0), axis=0, keepdims=True)


def _fox_sample(layer, q, k_new, v_new, lf_new, cache_k, cache_v, cache_lf, page_table, pages=8):
    n, n_pages = page_table.shape
    page = cache_k.shape[2]
    n_steps = n_pages // pages
    lfn = jnp.pad(lf_new[:, :H_C], ((0, 0), (0, HEAD_ROWS - H_C))).reshape(n, HEAD_ROWS, 1)
    seq3 = lambda w: pl.BlockSpec((1, 1, w), lambda b, s, pt: (b, 0, 0))

    def paged(width, u):
        return pl.BlockSpec((None, None, page, width),
                            lambda b, s, pt, u=u: (layer, pt[b, s * pages + u], 0, 0))

    in_specs = ([seq3(W_C), seq3(W_C), seq3(W_C),
                 pl.BlockSpec((1, HEAD_ROWS, 1), lambda b, s, pt: (b, 0, 0))]
                + [paged(W_C, u) for u in range(pages)]
                + [paged(W_C, u) for u in range(pages)]
                + [paged(H_C, u) for u in range(pages)])
    out = pl.pallas_call(
        functools.partial(_fox_sample_kernel, pages=pages, n_steps=n_steps),
        grid_spec=pltpu.PrefetchScalarGridSpec(
            num_scalar_prefetch=1,
            grid=(n, n_steps),
            in_specs=in_specs,
            out_specs=pl.BlockSpec((1, 1, W_C), lambda b, s, pt: (b, 0, 0)),
            scratch_shapes=[pltpu.VMEM((HEAD_ROWS, W_C), F32), pltpu.VMEM((HEAD_ROWS, W_C), BF16),
                            pltpu.VMEM((HEAD_ROWS, 1), F32), pltpu.VMEM((HEAD_ROWS, 1), F32),
                            pltpu.VMEM((HEAD_ROWS, W_C), F32), pltpu.VMEM((HEAD_ROWS, 1), F32)]),
        out_shape=jax.ShapeDtypeStruct((n, 1, W_C), F32),
        compiler_params=_params(2),
        name="fox_sample",
    )(page_table, q.reshape(n, 1, W_C), k_new.reshape(n, 1, W_C), v_new.reshape(n, 1, W_C), lfn,
      *([cache_k] * pages), *([cache_v] * pages), *([cache_lf] * pages))
    return out.reshape(n, W_C)


def _outproj_kernel(x_ref, ya_ref, ob_ref, bonus_ref, g_ref, yc_ref, ln_ref, avg_ref, wo_ref, o_ref):
    ob = ob_ref[...]
    mu = _dot_split(ob, avg_ref[...])
    d = ob - mu
    var = _dot((d * d).astype(BF16), avg_ref[...])
    yb = (d * lax.rsqrt(var + GN_EPS) * ln_ref[0:1, :] + ln_ref[1:2, :] + bonus_ref[...]) * g_ref[...]
    acc = _dot(ya_ref[...], wo_ref[0:W_A, :])
    acc += _dot(yb.astype(BF16), wo_ref[W_A:W_A + W_B, :])
    acc += _dot(yc_ref[...], wo_ref[W_A + W_B:, :])
    o_ref[...] = x_ref[...] + acc


def _outproj(x, ya, ob, bonus, g, yc, ln, avg_b, wo, tm=512):
    m = x.shape[0]
    tm = min(tm, m)
    row = lambda i: (i, 0)
    fix = lambda i: (0, 0)
    return pl.pallas_call(
        _outproj_kernel,
        grid=(m // tm,),
        in_specs=[pl.BlockSpec((tm, D_MODEL), row), pl.BlockSpec((tm, W_A), row),
                  pl.BlockSpec((tm, W_B), row), pl.BlockSpec((tm, W_B), row),
                  pl.BlockSpec((tm, W_B), row), pl.BlockSpec((tm, W_C), row),
                  pl.BlockSpec((8, W_B), fix), pl.BlockSpec((W_B, W_B), fix),
                  pl.BlockSpec((D_MODEL, D_MODEL), fix)],
        out_specs=pl.BlockSpec((tm, D_MODEL), row),
        out_shape=jax.ShapeDtypeStruct((m, D_MODEL), F32),
        compiler_params=_params(1),
        name="outproj",
    )(x, ya, ob, bonus, g, yc, ln, avg_b, wo)


def _block_diag_const(width, value):
    idx = jnp.arange(width) // GROUP
    return jnp.where(idx[:, None] == idx[None, :], value, 0.0).astype(BF16)


def _pad_rows(vecs, width):
    rows = [jnp.pad(v, (0, width - v.shape[0])) for v in vecs]
    rows += [jnp.zeros((width,), F32)] * (8 - len(rows))
    return jnp.stack(rows)


def _layer_weights(l, norm_g, w_ffn_in, w_ffn_out, w_in, a_ws, a_bs, a_norm_g, b_mu, b_w0, b_wB,
                   b_a0, b_aB, b_gB, b_kk, b_ka, b_rk, b_ln_g, b_ln_b, c_fb, w_o):
    zeros_lora = jnp.zeros((R_DECAY, W_B), F32)
    return dict(
        norm_g=norm_g[l],
        w_ffn_in=w_ffn_in[l].astype(BF16),
        w_ffn_out=w_ffn_out[l].astype(BF16),
        w_in=jnp.pad(w_in[l], ((0, 0), (0, IN_PROJ_PAD - IN_PROJ))).astype(BF16),
        c_fb=jnp.pad(c_fb[l], (0, LANES - H_C)).reshape(1, LANES),
        a_ws=a_ws[l],
        a_bias=jnp.repeat(a_bs[l].T, GROUP, axis=1),
        a_norm_g=a_norm_g[l],
        b_vec=_pad_rows([b_mu[l], b_w0[l], b_a0[l], b_kk[l], b_ka[l], b_rk[l]], B_PROJ),
        b_wB=jnp.concatenate([b_wB[l], zeros_lora], axis=0).astype(BF16),
        b_aB=jnp.concatenate([zeros_lora, b_aB[l]], axis=0).astype(BF16),
        b_gB=b_gB[l].astype(BF16),
        b_ln=_pad_rows([b_ln_g[l], b_ln_b[l]], W_B),
        w_o=w_o[l].astype(BF16),
        ones_b=_block_diag_const(W_B, 1.0),
        avg_b=_block_diag_const(W_B, 1.0 / GROUP),
        avg_a=_block_diag_const(W_A, 1.0 / GROUP),
    )


def _mix_prompt(wts, z, n_seq, seq_len):
    za, zb, q, k, v, lf = z
    ya, _ = _gmlp(za, wts["a_norm_g"], wts["a_ws"], wts["a_bias"], wts["avg_a"])
    r, lw, k2, vb, kk, beta, g, bonus = _rwkv_prep(zb, None, seq_len, wts)
    x1, x2, ub, op, sp, gam = _rwkv_chunks(r, lw, k2, vb, kk, beta)
    ob, st = _rwkv_scan(x1, x2, ub, op, sp, gam, n_seq, seq_len)
    c, ct = _cumsum(lf, n_seq, seq_len)
    yc = _fox_prompt(q, k, v, c, ct, n_seq, seq_len)
    st = st.reshape(n_seq, N_PAIR, 2, GROUP, 2, GROUP)
    wkv = jnp.stack([st[:, :, 0, :, 0, :], st[:, :, 1, :, 1, :]], axis=2)
    wkv = wkv.reshape(n_seq, H_B, GROUP, GROUP).transpose(0, 1, 3, 2)
    return ya, ob.reshape(n_seq * seq_len, W_B), bonus, g, yc, wkv


def _mix_sample(wts, layer, z, shift0, wkv0, cache_k, cache_v, cache_lf, page_table):
    za, zb, q, k, v, lf = z
    n = za.shape[0]
    za_pad = jnp.pad(za[:, None, :], ((0, 0), (0, CHUNK_A - 1), (0, 0))).reshape(n * CHUNK_A, A_PROJ)
    ya, va = _gmlp(za_pad, wts["a_norm_g"], wts["a_ws"], wts["a_bias"], wts["avg_a"])
    ya = ya.reshape(n, CHUNK_A, W_A)[:, 0]
    va = va.reshape(n, CHUNK_A, W_A)[:, 0]
    r, lw, k2, vb, kk, beta, g, bonus = _rwkv_prep(zb, shift0, 1, wts)
    ob, wkv = _rwkv_step(wkv0, r, lw, k2, kk, beta, vb)
    yc = _fox_sample(layer, q, k, v, lf, cache_k, cache_v, cache_lf, page_table)
    return ya, ob, bonus, g, yc.astype(BF16), wkv, va


def kernel(x_prompt, x_sample, cache_k, cache_v, cache_logf, state_wkv, state_shift, page_table,
           norm_g, w_ffn_in, w_ffn_out, w_in, a_ws, a_bs, a_norm_g, b_mu, b_w0, b_wB, b_a0, b_aB,
           b_gB, b_kk, b_ka, b_rk, b_ln_g, b_ln_b, c_fb, w_o, final_norm):
    n_p, seq_len, _ = x_prompt.shape
    n_s = x_sample.shape[0]
    depth = norm_g.shape[0]
    n_phys, page = cache_k.shape[1], cache_k.shape[2]
    ck = cache_k.reshape(depth, n_phys, page, W_C)
    cv = cache_v.reshape(depth, n_phys, page, W_C)
    xp = x_prompt.reshape(n_p * seq_len, D_MODEL)
    xs = x_sample.reshape(n_s, D_MODEL)
    outs = {name: [] for name in ("kp", "vp", "lfp", "wkvp", "shp", "ks", "vs", "lfs", "wkvs", "shs", "va")}
    for l in range(depth):
        wts = _layer_weights(l, norm_g, w_ffn_in, w_ffn_out, w_in, a_ws, a_bs, a_norm_g, b_mu, b_w0,
                             b_wB, b_a0, b_aB, b_gB, b_kk, b_ka, b_rk, b_ln_g, b_ln_b, c_fb, w_o)
        last = l == depth - 1
        fin = final_norm if last else None

        xp = _ffn(xp, wts["norm_g"][0], wts["w_ffn_in"][0], wts["w_ffn_out"][0])
        zp = _inproj(xp, wts["norm_g"][1], wts["w_in"], wts["c_fb"])
        ya, ob, bonus, g, yc, wkvp = _mix_prompt(wts, zp, n_p, seq_len)
        xp = _outproj(xp, ya, ob, bonus, g, yc, wts["b_ln"], wts["avg_b"], wts["w_o"])
        xp = _ffn(xp, wts["norm_g"][2], wts["w_ffn_in"][1], wts["w_ffn_out"][1], final_g=fin)
        outs["kp"].append(zp[3].reshape(n_p, seq_len, H_C, GROUP))
        outs["vp"].append(zp[4].reshape(n_p, seq_len, H_C, GROUP))
        outs["lfp"].append(zp[5][:, :H_C].reshape(n_p, seq_len, H_C))
        outs["wkvp"].append(wkvp)
        outs["shp"].append(zp[1].reshape(n_p, seq_len, B_PROJ)[:, -1])

        xs = _ffn(xs, wts["norm_g"][0], wts["w_ffn_in"][0], wts["w_ffn_out"][0])
        zs = _inproj(xs, wts["norm_g"][1], wts["w_in"], wts["c_fb"])
        ya, ob, bonus, g, yc, wkvs, va = _mix_sample(wts, l, zs, state_shift[l], state_wkv[l],
                                                     ck, cv, cache_logf, page_table)
        xs = _outproj(xs, ya, ob, bonus, g, yc, wts["b_ln"], wts["avg_b"], wts["w_o"])
        xs = _ffn(xs, wts["norm_g"][2], wts["w_ffn_in"][1], wts["w_ffn_out"][1], final_g=fin)
        outs["ks"].append(zs[3].reshape(n_s, 1, H_C, GROUP))
        outs["vs"].append(zs[4].reshape(n_s, 1, H_C, GROUP))
        outs["lfs"].append(zs[5][:, :H_C].reshape(n_s, 1, H_C))
        outs["wkvs"].append(wkvs)
        outs["shs"].append(zs[1])
        outs["va"].append(va.reshape(n_s, 1, W_A))
    st = lambda name: jnp.stack(outs[name])
    return (xp.reshape(n_p, seq_len, D_MODEL), xs.reshape(n_s, 1, D_MODEL),
            st("kp"), st("vp"), st("lfp"), st("wkvp"), st("shp"),
            st("ks"), st("vs"), st("lfs"), st("wkvs"), st("shs"), st("va"))
```

```python
import functools

import jax
import jax.numpy as jnp
from jax import lax
from jax.experimental import pallas as pl
from jax.experimental.pallas import tpu as pltpu

F32 = jnp.float32
BF16 = jnp.bfloat16
HIGHEST = lax.Precision.HIGHEST

LANES = 128
D_MODEL = 1024
D_FF = 2816
GROUP = 64
W_A = 256
W_B = 384
W_C = 384
H_B = W_B // GROUP
H_C = W_C // GROUP
N_PAIR = W_B // LANES
R_DECAY = 64
R_AAA = 64
R_GATE = 128
B_PROJ = 3 * W_B + R_DECAY + R_AAA + R_GATE
A_PROJ = 2 * W_A
C_PROJ = 3 * W_C + H_C
IN_PROJ = A_PROJ + B_PROJ + C_PROJ
IN_PROJ_PAD = A_PROJ + B_PROJ + 3 * W_C + LANES
CHUNK_A = 128
CHUNK_B = 64
NORM_EPS = 1e-6
GN_EPS = 64e-5
NEG_BIG = -1e30
LOG2E = 1.4426950408889634
Q_SCALE = LOG2E * GROUP ** -0.5
VMEM_LIMIT = 56 << 20


def _params(n_axes, vmem=VMEM_LIMIT):
    return pltpu.CompilerParams(dimension_semantics=("arbitrary",) * n_axes,
                                vmem_limit_bytes=vmem)


def _sigmoid(x):
    return 1.0 / (1.0 + jnp.exp(-x))


def _softplus(x):
    return jnp.maximum(x, 0.0) + jnp.log(1.0 + jnp.exp(-jnp.abs(x)))


def _gelu_tanh(x):
    return 0.5 * x * (1.0 + jnp.tanh(0.7978845608028654 * (x + 0.044715 * (x * x * x))))


def _rms(x, g):
    return x * lax.rsqrt(jnp.mean(x * x, axis=-1, keepdims=True) + NORM_EPS) * g


def _dot(a, b):
    return jnp.dot(a, b, preferred_element_type=F32)


def _dot_nt(a, b):
    return lax.dot_general(a, b, (((1,), (1,)), ((), ())), preferred_element_type=F32)


def _dot_split(a, b_bf):
    hi = a.astype(BF16)
    lo = (a - hi.astype(F32)).astype(BF16)
    return _dot(hi, b_bf) + _dot(lo, b_bf)


def _iota(shape, dim):
    return lax.broadcasted_iota(jnp.int32, shape, dim)


def _ffn_kernel(*refs, n_ff, final):
    if final:
        x_ref, g_ref, wg_ref, wu_ref, wo_ref, fg_ref, o_ref, h_scr, acc_scr = refs
    else:
        x_ref, g_ref, wg_ref, wu_ref, wo_ref, o_ref, h_scr, acc_scr = refs
    j = pl.program_id(1)

    @pl.when(j == 0)
    def _():
        h_scr[...] = _rms(x_ref[...], g_ref[...]).astype(BF16)
        acc_scr[...] = jnp.zeros_like(acc_scr)

    h = h_scr[...]
    gate = _dot(h, wg_ref[...])
    up = _dot(h, wu_ref[...])
    act = (gate * _sigmoid(gate) * up).astype(BF16)
    acc_scr[...] += _dot(act, wo_ref[...])

    @pl.when(j == n_ff - 1)
    def _():
        y = x_ref[...] + 0.5 * acc_scr[...]
        if final:
            y = _rms(y, fg_ref[...])
        o_ref[...] = y


def _ffn(x, g, w_in, w_out, final_g=None, tm=512, n_ff=2):
    m = x.shape[0]
    tm = min(tm, m)
    tf = D_FF // n_ff
    final = final_g is not None
    in_specs = [
        pl.BlockSpec((tm, D_MODEL), lambda i, j: (i, 0)),
        pl.BlockSpec((1, D_MODEL), lambda i, j: (0, 0)),
        pl.BlockSpec((D_MODEL, tf), lambda i, j: (0, j)),
        pl.BlockSpec((D_MODEL, tf), lambda i, j: (0, j + n_ff)),
        pl.BlockSpec((tf, D_MODEL), lambda i, j: (j, 0)),
    ]
    args = [x, g.reshape(1, D_MODEL), w_in, w_in, w_out]
    if final:
        in_specs.append(pl.BlockSpec((1, D_MODEL), lambda i, j: (0, 0)))
        args.append(final_g.reshape(1, D_MODEL))
    return pl.pallas_call(
        functools.partial(_ffn_kernel, n_ff=n_ff, final=final),
        grid=(m // tm, n_ff),
        in_specs=in_specs,
        out_specs=pl.BlockSpec((tm, D_MODEL), lambda i, j: (i, 0)),
        out_shape=jax.ShapeDtypeStruct((m, D_MODEL), F32),
        scratch_shapes=[pltpu.VMEM((tm, D_MODEL), BF16), pltpu.VMEM((tm, D_MODEL), F32)],
        compiler_params=_params(2),
        name="ffn",
    )(*args)


def _inproj_kernel(x_ref, g_ref, w_ref, wkvt_ref, fb_ref, za_ref, zb_ref, q_ref, lf_ref, *kv_refs,
                   channel_major):
    h = _rms(x_ref[...], g_ref[...]).astype(BF16)
    z = _dot(h, w_ref[...])
    o = A_PROJ
    za_ref[...] = z[:, :o]
    zb_ref[...] = z[:, o:o + B_PROJ]
    o += B_PROJ
    q_ref[...] = (z[:, o:o + W_C] * Q_SCALE).astype(BF16)
    k = z[:, o + W_C:o + 2 * W_C]
    v = z[:, o + 2 * W_C:o + 3 * W_C]
    lf_ref[...] = -_softplus(-(z[:, o + 3 * W_C:] + fb_ref[...]))
    if channel_major:
        kt_ref, vt_ref, kb_ref, vtb_ref = kv_refs
        kt_ref[...] = _dot_nt(wkvt_ref[0:W_C, :], h)
        vt = _dot_nt(wkvt_ref[W_C:, :], h)
        vt_ref[...] = vt
        vtb_ref[...] = vt.astype(BF16)
        kb_ref[...] = k.astype(BF16)
    else:
        k_ref, v_ref = kv_refs
        k_ref[...] = k
        v_ref[...] = v


def _inproj(x, g, w_pad, wkv_t, fb_pad, seq_len=None, tm=512):
    m = x.shape[0]
    tm = min(tm, m)
    row = lambda i: (i, 0)
    fix = lambda i: (0, 0)
    widths = (A_PROJ, B_PROJ, W_C, LANES)
    dtypes = (F32, F32, BF16, F32)
    out_specs = [pl.BlockSpec((tm, w), row) for w in widths]
    out_shape = [jax.ShapeDtypeStruct((m, w), d) for w, d in zip(widths, dtypes)]
    if seq_len is None:
        out_specs += [pl.BlockSpec((tm, W_C), row)] * 2
        out_shape += [jax.ShapeDtypeStruct((m, W_C), F32)] * 2
    else:
        bps = seq_len // tm
        seq_blk = pl.BlockSpec((None, W_C, tm), lambda i: (i // bps, 0, i % bps))
        out_specs += [seq_blk, seq_blk, pl.BlockSpec((tm, W_C), row), pl.BlockSpec((W_C, tm), lambda i: (0, i))]
        out_shape += [jax.ShapeDtypeStruct((m // seq_len, W_C, seq_len), F32)] * 2
        out_shape += [jax.ShapeDtypeStruct((m, W_C), BF16), jax.ShapeDtypeStruct((W_C, m), BF16)]
    return pl.pallas_call(
        functools.partial(_inproj_kernel, channel_major=seq_len is not None),
        grid=(m // tm,),
        in_specs=[pl.BlockSpec((tm, D_MODEL), row), pl.BlockSpec((1, D_MODEL), fix),
                  pl.BlockSpec((D_MODEL, IN_PROJ_PAD), fix), pl.BlockSpec((2 * W_C, D_MODEL), fix),
                  pl.BlockSpec((1, LANES), fix)],
        out_specs=out_specs,
        out_shape=out_shape,
        compiler_params=_params(1),
        name="inproj",
    )(x, g.reshape(1, D_MODEL), w_pad, wkv_t, fb_pad)


def _gmlp_kernel(za_ref, gain_ref, ws_ref, bias_ref, avg_ref, ya_ref, va_ref, *, n_chunks):
    z = _gelu_tanh(za_ref[...])
    u = z[:, :W_A]
    v = z[:, W_A:]
    ms = _dot((v * v).astype(BF16), avg_ref[...])
    vn = v * lax.rsqrt(ms + NORM_EPS) * gain_ref[...]
    va_ref[...] = vn
    causal = _iota((CHUNK_A, CHUNK_A), 0) >= _iota((CHUNK_A, CHUNK_A), 1)
    lane_group = _iota((CHUNK_A, W_A), 1) // GROUP
    wm = [jnp.where(causal, ws_ref[g], 0.0).astype(BF16) for g in range(W_A // GROUP)]
    for c in range(n_chunks):
        rows = slice(c * CHUNK_A, (c + 1) * CHUNK_A)
        vc = vn[rows]
        s = bias_ref[...]
        for g in range(W_A // GROUP):
            s = s + _dot(wm[g], jnp.where(lane_group == g, vc, 0.0).astype(BF16))
        ya_ref[rows, :] = (u[rows] * s).astype(BF16)


def _gmlp(za, gain, ws, bias_full, avg_a, tm=512):
    m = za.shape[0]
    tm = min(tm, m)
    row = lambda i: (i, 0)
    fix = lambda i: (0, 0)
    return pl.pallas_call(
        functools.partial(_gmlp_kernel, n_chunks=tm // CHUNK_A),
        grid=(m // tm,),
        in_specs=[pl.BlockSpec((tm, A_PROJ), row), pl.BlockSpec((1, W_A), fix),
                  pl.BlockSpec((W_A // GROUP, CHUNK_A, CHUNK_A), lambda i: (0, 0, 0)),
                  pl.BlockSpec((CHUNK_A, W_A), fix), pl.BlockSpec((W_A, W_A), fix)],
        out_specs=[pl.BlockSpec((tm, W_A), row), pl.BlockSpec((tm, W_A), row)],
        out_shape=[jax.ShapeDtypeStruct((m, W_A), BF16), jax.ShapeDtypeStruct((m, W_A), F32)],
        compiler_params=_params(1),
        name="gmlp",
    )(za, gain.reshape(1, W_A), ws, bias_full, avg_a)


def _rwkv_prep_math(zb, prev, vec_ref, wb_ref, ab_ref, gb_ref, ones_ref, outs):
    r_ref, lw_ref, k_ref, v_ref, kk_ref, beta_ref, g_ref, bonus_ref = outs
    mu = vec_ref[0:1, :]
    zs = zb + (prev - zb) * mu
    r = zs[:, :W_B]
    k = zs[:, W_B:2 * W_B]
    v = zs[:, 2 * W_B:3 * W_B]
    lora = zs[:, 3 * W_B:3 * W_B + LANES]
    gl = zs[:, 3 * W_B + LANES:]
    w0 = vec_ref[1:2, :W_B]
    a0 = vec_ref[2:3, :W_B]
    kkw = vec_ref[3:4, :W_B]
    kaw = vec_ref[4:5, :W_B]
    rkw = vec_ref[5:6, :W_B]
    w = -_softplus(-(w0 + _dot(jnp.tanh(lora).astype(BF16), wb_ref[...]))) - 0.5
    a = _sigmoid(a0 + _dot(lora.astype(BF16), ab_ref[...]))
    g = _dot(_sigmoid(gl).astype(BF16), gb_ref[...])
    kk = k * kkw
    ss = _dot_split(kk * kk, ones_ref[...])
    kk = kk / jnp.maximum(jnp.sqrt(ss), 1e-12)
    k2 = k * (1.0 + (a - 1.0) * kaw)
    r_ref[...] = r
    lw_ref[...] = -jnp.exp(w)
    k_ref[...] = k2
    v_ref[...] = v
    kk_ref[...] = kk
    beta_ref[...] = kk * a
    g_ref[...] = g
    bonus_ref[...] = _dot_split(r * k2 * rkw, ones_ref[...]) * v


def _rwkv_prep_seq_kernel(zb_ref, pb_ref, vec_ref, wb_ref, ab_ref, gb_ref, ones_ref, *outs,
                          blocks_per_seq):
    i = pl.program_id(0)
    zb = zb_ref[...]
    tm = zb.shape[0]
    first = (i % blocks_per_seq) == 0
    last_prev = jnp.where(first, 0.0, pb_ref[7:8, :])
    prev = jnp.where(_iota((tm, 1), 0) == 0, last_prev, pltpu.roll(zb, shift=1, axis=0))
    _rwkv_prep_math(zb, prev, vec_ref, wb_ref, ab_ref, gb_ref, ones_ref, outs)


def _rwkv_prep_tok_kernel(zb_ref, prev_ref, vec_ref, wb_ref, ab_ref, gb_ref, ones_ref, *outs):
    _rwkv_prep_math(zb_ref[...], prev_ref[...], vec_ref, wb_ref, ab_ref, gb_ref, ones_ref, outs)


def _rwkv_prep(zb, prev, seq_len, wts, tm=512):
    m = zb.shape[0]
    tm = min(tm, m)
    row = lambda i: (i, 0)
    fix = lambda i: (0, 0)
    w_specs = [pl.BlockSpec((8, B_PROJ), fix), pl.BlockSpec((LANES, W_B), fix),
               pl.BlockSpec((LANES, W_B), fix), pl.BlockSpec((R_GATE, W_B), fix),
               pl.BlockSpec((W_B, W_B), fix)]
    w_args = [wts["b_vec"], wts["b_wB"], wts["b_aB"], wts["b_gB"], wts["ones_b"]]
    if prev is None:
        kern = functools.partial(_rwkv_prep_seq_kernel, blocks_per_seq=seq_len // tm)
        sub = tm // 8
        in_specs = [pl.BlockSpec((tm, B_PROJ), row),
                    pl.BlockSpec((8, B_PROJ), lambda i: (jnp.maximum(i * sub - 1, 0), 0))]
        args = [zb, zb]
    else:
        kern = _rwkv_prep_tok_kernel
        in_specs = [pl.BlockSpec((tm, B_PROJ), row), pl.BlockSpec((tm, B_PROJ), row)]
        args = [zb, prev]
    return pl.pallas_call(
        kern,
        grid=(m // tm,),
        in_specs=in_specs + w_specs,
        out_specs=[pl.BlockSpec((tm, W_B), row)] * 8,
        out_shape=[jax.ShapeDtypeStruct((m, W_B), F32)] * 8,
        compiler_params=_params(1),
        name="rwkv_prep",
    )(*args, *w_args)


def _stack(x, low):
    return jnp.concatenate([jnp.where(low, x, 0.0), jnp.where(low, 0.0, x)], axis=0)


CHUNKS_PER_STEP = 4

def _rwkv_chunk_kernel(r_ref, lw_ref, k_ref, v_ref, kk_ref, beta_ref,
                       x1_ref, x2_ref, ub_ref, op_ref, sp_ref, gam_ref):
    c = CHUNK_B
    n2 = 2 * c
    rows = CHUNKS_PER_STEP * c
    ri = _iota((rows, rows), 0)
    ci = _iota((rows, rows), 1)
    tri = jnp.where(ri >= ci, 1.0, 0.0) * jnp.where((ri // c) == (ci // c), 1.0, 0.0)
    lw = lw_ref[...]
    cum = jnp.dot(tri, lw, precision=HIGHEST, preferred_element_type=F32)
    lasts = [cum[(j + 1) * c - 1:(j + 1) * c, :] for j in range(CHUNKS_PER_STEP)]
    for j in range(CHUNKS_PER_STEP):
        gam_ref[j] = jnp.exp(lasts[j])
    cum_last = jnp.concatenate([jnp.broadcast_to(l, (c, W_B)) for l in lasts], axis=0)
    e_pos = jnp.exp(cum)
    e_neg = jnp.exp(-cum)
    e_tail = jnp.exp(cum_last - cum)
    r_t = r_ref[...] * e_pos
    kap_t = kk_ref[...] * jnp.exp(cum - lw)
    beta_h = beta_ref[...] * e_neg
    k_h = k_ref[...] * e_neg
    beta_c = beta_ref[...] * e_tail
    k_c = k_ref[...] * e_tail
    v = v_ref[...]

    low = _iota((c, LANES), 1) < GROUP
    rr = _iota((n2, n2), 0) & (c - 1)
    cc = _iota((n2, n2), 1) & (c - 1)
    strict = rr > cc
    incl = rr >= cc
    streams = [(j, p) for j in range(CHUNKS_PER_STEP) for p in range(N_PAIR)]

    def tile(x, j, p):
        return _stack(x[j * c:(j + 1) * c, p * LANES:(p + 1) * LANES], low)

    kap_s = [tile(kap_t, j, p) for j, p in streams]
    r_s = [tile(r_t, j, p) for j, p in streams]
    v_s = [tile(v, j, p).astype(BF16) for j, p in streams]
    gram = [_dot_nt(jnp.concatenate([kap_s[i], r_s[i]], axis=0).astype(BF16),
                    jnp.concatenate([tile(beta_h, j, p), tile(k_h, j, p)], axis=0).astype(BF16))
            for i, (j, p) in enumerate(streams)]
    n_bf = [jnp.where(strict, g[:n2, :n2], 0.0).astype(BF16) for g in gram]
    av = [_dot(jnp.where(strict, g[:n2, n2:], 0.0).astype(BF16), vs) for g, vs in zip(gram, v_s)]
    for i, (j, p) in enumerate(streams):
        op_ref[j, p] = _dot(jnp.where(incl, gram[i][n2:, n2:], 0.0).astype(BF16), v_s[i])
        sp_ref[j, p] = _dot(tile(k_c, j, p).T.astype(BF16), v_s[i])
        x2_ref[j, p] = jnp.concatenate([jnp.where(incl, gram[i][n2:, :n2], 0.0),
                                        tile(beta_c, j, p).T], axis=0).astype(BF16)
    x = [jnp.concatenate([ks, -a], axis=1) for ks, a in zip(kap_s, av)]
    x = [xi - _dot(nb, xi.astype(BF16)) for xi, nb in zip(x, n_bf)]
    pw = n_bf
    for _ in range(5):
        pw = [_dot(q, q).astype(BF16) for q in pw]
        x = [xi + _dot(q, xi.astype(BF16)) for xi, q in zip(x, pw)]
    for i, (j, p) in enumerate(streams):
        x1_ref[j, p] = jnp.concatenate([x[i][:, :LANES], r_s[i]], axis=0).astype(BF16)
        ub_ref[j, p] = x[i][:, LANES:]


def _rwkv_chunks(r, lw, k2, v, kk, beta):
    m = r.shape[0]
    nc = m // CHUNK_B
    cb = CHUNKS_PER_STEP
    row = lambda i: (i, 0)
    blk = lambda i: (i, 0, 0, 0)
    t = 2 * CHUNK_B
    return pl.pallas_call(
        _rwkv_chunk_kernel,
        grid=(nc // cb,),
        in_specs=[pl.BlockSpec((cb * CHUNK_B, W_B), row)] * 6,
        out_specs=[pl.BlockSpec((cb, N_PAIR, 2 * t, LANES), blk),
                   pl.BlockSpec((cb, N_PAIR, 2 * t, LANES), blk),
                   pl.BlockSpec((cb, N_PAIR, t, LANES), blk),
                   pl.BlockSpec((cb, N_PAIR, t, LANES), blk),
                   pl.BlockSpec((cb, N_PAIR, t, LANES), blk),
                   pl.BlockSpec((cb, 1, W_B), lambda i: (i, 0, 0))],
        out_shape=[jax.ShapeDtypeStruct((nc, N_PAIR, 2 * t, LANES), BF16),
                   jax.ShapeDtypeStruct((nc, N_PAIR, 2 * t, LANES), BF16),
                   jax.ShapeDtypeStruct((nc, N_PAIR, t, LANES), F32),
                   jax.ShapeDtypeStruct((nc, N_PAIR, t, LANES), F32),
                   jax.ShapeDtypeStruct((nc, N_PAIR, t, LANES), F32),
                   jax.ShapeDtypeStruct((nc, 1, W_B), F32)],
        compiler_params=_params(1),
        name="rwkv_chunks",
    )(r, lw, k2, v, kk, beta)


def _rwkv_scan_kernel(x1_ref, x2_ref, ub_ref, op_ref, sp_ref, gam_ref, o_ref, st_ref, st_scr,
                      *, n_seq, n_chunks):
    ci = pl.program_id(0)
    t = 2 * CHUNK_B

    @pl.when(ci == 0)
    def _():
        st_scr[...] = jnp.zeros_like(st_scr)

    eye = _iota((t, t), 0) == _iota((t, t), 1)
    for b in range(n_seq):
        for p in range(N_PAIR):
            st = st_scr[b * N_PAIR + p]
            y = _dot(x1_ref[b, 0, p], st.astype(BF16))
            u = ub_ref[b, 0, p] - y[:t]
            z = _dot(x2_ref[b, 0, p], u.astype(BF16))
            o_s = op_ref[b, 0, p] + y[t:] + z[:t]
            gam_row = gam_ref[b, 0, :, p * LANES:(p + 1) * LANES]
            gam_col = jnp.sum(jnp.where(eye, gam_row, 0.0), axis=1, keepdims=True)
            st_scr[b * N_PAIR + p] = gam_col * st + z[t:] + sp_ref[b, 0, p]
            o_ref[b, :, p * LANES:(p + 1) * LANES] = o_s[:CHUNK_B] + o_s[CHUNK_B:]

    @pl.when(ci == n_chunks - 1)
    def _():
        st_ref[...] = st_scr[...]


def _rwkv_scan(x1, x2, ub, op, sp, gam, n_seq, seq_len):
    nc = seq_len // CHUNK_B
    t = 2 * CHUNK_B
    r5 = lambda a: a.reshape((n_seq, nc) + a.shape[1:])
    blk5 = lambda rows: pl.BlockSpec((n_seq, 1, N_PAIR, rows, LANES), lambda c: (0, c, 0, 0, 0))
    return pl.pallas_call(
        functools.partial(_rwkv_scan_kernel, n_seq=n_seq, n_chunks=nc),
        grid=(nc,),
        in_specs=[blk5(2 * t), blk5(2 * t), blk5(t), blk5(t), blk5(t),
                  pl.BlockSpec((n_seq, 1, 1, W_B), lambda c: (0, c, 0, 0))],
        out_specs=[pl.BlockSpec((n_seq, CHUNK_B, W_B), lambda c: (0, c, 0)),
                   pl.BlockSpec((n_seq * N_PAIR, t, LANES), lambda c: (0, 0, 0))],
        out_shape=[jax.ShapeDtypeStruct((n_seq, seq_len, W_B), F32),
                   jax.ShapeDtypeStruct((n_seq * N_PAIR, t, LANES), F32)],
        scratch_shapes=[pltpu.VMEM((n_seq * N_PAIR, t, LANES), F32)],
        compiler_params=_params(1),
        name="rwkv_scan",
    )(r5(x1), r5(x2), r5(ub), r5(op), r5(sp), r5(gam))


def _rwkv_step_kernel(s_ref, r_ref, lw_ref, k_ref, kk_ref, beta_ref, vcol_ref, o_ref, so_ref):
    for h in range(H_B):
        s = s_ref[0, h]
        sk = jnp.sum(s * kk_ref[0, h], axis=1, keepdims=True)
        s_new = s * jnp.exp(lw_ref[0, h]) - sk * beta_ref[0, h] + vcol_ref[0, h] * k_ref[0, h]
        so_ref[0, h] = s_new
        o_ref[0, h] = jnp.sum(s_new * r_ref[0, h], axis=1, keepdims=True)


def _rwkv_step(state, r, lw, k2, kk, beta, v):
    n = state.shape[0]
    rowv = lambda a: a.reshape(n, H_B, 1, GROUP)
    idx = lambda b: (b, 0, 0, 0)
    row_spec = pl.BlockSpec((1, H_B, 1, GROUP), idx)
    col_spec = pl.BlockSpec((1, H_B, GROUP, 1), idx)
    mat_spec = pl.BlockSpec((1, H_B, GROUP, GROUP), idx)
    o, s_new = pl.pallas_call(
        _rwkv_step_kernel,
        grid=(n,),
        in_specs=[mat_spec] + [row_spec] * 5 + [col_spec],
        out_specs=[col_spec, mat_spec],
        out_shape=[jax.ShapeDtypeStruct((n, H_B, GROUP, 1), F32),
                   jax.ShapeDtypeStruct((n, H_B, GROUP, GROUP), F32)],
        compiler_params=_params(1),
        name="rwkv_step",
    )(state, rowv(r), rowv(lw), rowv(k2), rowv(kk), rowv(beta), v.reshape(n, H_B, GROUP, 1))
    return o.reshape(n, W_B), s_new


BIAS_PIECES = 3


def _cumsum_kernel(lf_ref, place_ref, b_ref, carry):
    @pl.when(pl.program_id(1) == 0)
    def _():
        carry[...] = jnp.zeros_like(carry)

    tb = lf_ref.shape[0]
    tri = (_iota((tb, tb), 0) >= _iota((tb, tb), 1)).astype(F32)
    c = jnp.dot(tri, lf_ref[...], precision=HIGHEST, preferred_element_type=F32) + carry[...]
    carry[...] = c[tb - 1:tb, :]
    rest = -LOG2E * c
    pieces = []
    for _ in range(BIAS_PIECES):
        piece = rest.astype(BF16)
        pieces.append(piece)
        rest = rest - piece.astype(F32)
    b_ref[...] = _dot(jnp.concatenate(pieces, axis=1), place_ref[...]).astype(BF16)


def _bias_placement():
    rows = jnp.arange(BIAS_PIECES * LANES)
    piece, head = rows // LANES, rows % LANES
    col = LANES * (head // 2) + jnp.where(head % 2 == 0, GROUP, 0) + piece
    hit = (col[:, None] == jnp.arange(W_C)[None, :]) & (head < H_C)[:, None]
    return hit.astype(BF16)


def _cumsum(lf, n_seq, seq_len, tb=512):
    nb = seq_len // tb
    return pl.pallas_call(
        _cumsum_kernel,
        grid=(n_seq, nb),
        in_specs=[pl.BlockSpec((tb, LANES), lambda b, j: (b * nb + j, 0)),
                  pl.BlockSpec((BIAS_PIECES * LANES, W_C), lambda b, j: (0, 0))],
        out_specs=pl.BlockSpec((tb, W_C), lambda b, j: (b * nb + j, 0)),
        out_shape=jax.ShapeDtypeStruct((n_seq * seq_len, W_C), BF16),
        scratch_shapes=[pltpu.VMEM((1, LANES), F32)],
        compiler_params=_params(2),
        name="logf_cumsum",
    )(lf, _bias_placement())


def _fox_prompt_kernel(qi_ref, kj_ref, q_ref, k_ref, vt_ref, b_ref, o_ref, qa_scr, m_scr, acc_scr):
    t = pl.program_id(2)
    i = qi_ref[t]
    j = kj_ref[t]
    tq = q_ref.shape[0]
    tk = k_ref.shape[0]
    low_row = _iota((LANES, tq), 0) < GROUP

    @pl.when(j == 0)
    def _():
        q = q_ref[...]
        lane = _iota((tq, LANES), 1)
        ones_hi = jnp.where(lane < GROUP + BIAS_PIECES, 1.0, 0.0).astype(BF16)
        ones_lo = jnp.where(lane < BIAS_PIECES, 1.0, 0.0).astype(BF16)
        qa_scr[0] = jnp.where(lane < GROUP, q, ones_hi)
        qa_scr[1] = jnp.where(lane < GROUP, ones_lo, q)
        m_scr[...] = jnp.full_like(m_scr, NEG_BIG)
        acc_scr[...] = jnp.zeros_like(acc_scr)

    def step(diagonal):
        k = k_ref[...]
        bias = b_ref[...]
        vt = vt_ref[...]
        low_k = _iota((tk, LANES), 1) < GROUP
        low_v = _iota((LANES, tk), 0) < GROUP
        one = jnp.ones_like(vt)
        ka = (jnp.where(low_k, k, bias), jnp.where(low_k, bias, k))
        va = (jnp.where(low_v, vt, one), jnp.where(low_v, one, vt))
        for hh in range(2):
            s = _dot_nt(ka[hh], qa_scr[hh])
            if diagonal:
                s = jnp.where(_iota((tk, tq), 0) <= _iota((tk, tq), 1), s, NEG_BIG)
            m_prev = m_scr[hh]
            m_new = jnp.maximum(m_prev, jnp.max(s, axis=0, keepdims=True))
            alpha = jnp.exp2(m_prev - m_new)
            pr = jnp.exp2(s - m_new).astype(BF16)
            acc_scr[hh] = alpha * acc_scr[hh] + _dot(va[hh], pr)
            m_scr[hh] = m_new

    @pl.when(j < i)
    def _():
        step(False)

    @pl.when(j == i)
    def _():
        step(True)
        a0 = acc_scr[0]
        a1 = acc_scr[1]
        out = jnp.where(low_row, a0 / a0[LANES - 1:LANES, :], a1 / a1[0:1, :])
        o_ref[...] = out.T.astype(BF16)


def _fox_prompt(q, k, vt, bias, n_seq, seq_len, tq=512):
    nq = seq_len // tq
    pairs = [(i, j) for i in range(nq) for j in range(i + 1)]
    qi = jnp.asarray([i for i, _ in pairs], jnp.int32)
    kj = jnp.asarray([j for _, j in pairs], jnp.int32)
    q_spec = pl.BlockSpec((tq, LANES), lambda b, p, t, qi, kj: (b * nq + qi[t], p))
    k_spec = pl.BlockSpec((tq, LANES), lambda b, p, t, qi, kj: (b * nq + kj[t], p))
    vt_spec = pl.BlockSpec((LANES, tq), lambda b, p, t, qi, kj: (p, b * nq + kj[t]))
    return pl.pallas_call(
        _fox_prompt_kernel,
        grid_spec=pltpu.PrefetchScalarGridSpec(
            num_scalar_prefetch=2,
            grid=(n_seq, N_PAIR, len(pairs)),
            in_specs=[q_spec, k_spec, vt_spec, k_spec],
            out_specs=q_spec,
            scratch_shapes=[pltpu.VMEM((2, tq, LANES), BF16), pltpu.VMEM((2, 1, tq), F32),
                            pltpu.VMEM((2, LANES, tq), F32)]),
        out_shape=jax.ShapeDtypeStruct((n_seq * seq_len, W_C), BF16),
        compiler_params=_params(3),
        name="fox_prompt",
    )(qi, kj, q, k, vt, bias)


HEAD_ROWS = 16


def _fox_sample_kernel(pt_ref, q_ref, kn_ref, vn_ref, lfn_ref, *refs, pages, n_steps):
    k_refs = refs[:pages]
    v_refs = refs[pages:2 * pages]
    lf_refs = refs[2 * pages:3 * pages]
    o_ref, qf_scr, qb_scr, m_scr, l_scr, acc_scr, carry = refs[3 * pages:]
    s_id = pl.program_id(1)
    page = k_refs[0].shape[1]
    head_of_lane = _iota((HEAD_ROWS, W_C), 1) // GROUP
    own = head_of_lane == _iota((HEAD_ROWS, W_C), 0)

    @pl.when(s_id == 0)
    def _():
        qrows = jnp.where(own, q_ref[0].astype(F32), 0.0)
        qf_scr[...] = qrows
        qb_scr[...] = qrows.astype(BF16)
        m_scr[...] = jnp.full_like(m_scr, NEG_BIG)
        l_scr[...] = jnp.zeros_like(l_scr)
        acc_scr[...] = jnp.zeros_like(acc_scr)
        carry[...] = jnp.zeros_like(carry)

    upto = (_iota((page, page), 0) <= _iota((page, page), 1)).astype(F32)
    lf_all = jnp.concatenate([lf_refs[u][...] for u in range(pages)], axis=0)
    c_all = jnp.dot(lf_all, upto, precision=HIGHEST, preferred_element_type=F32)
    totals = [c_all[8 * u:8 * (u + 1), page - 1:page] for u in range(pages)]
    run = carry[...]
    cts = []
    for u in range(pages):
        cts.append(c_all[8 * u:8 * (u + 1)] + run)
        run = run + totals[u]
    carry[...] = run
    ct = jnp.concatenate(cts, axis=1)
    ct = jnp.concatenate([ct, jnp.zeros_like(ct)], axis=0)
    kcat = jnp.concatenate([k_refs[u][...].astype(BF16) for u in range(pages)], axis=1)
    vcat = jnp.concatenate([v_refs[u][...].astype(BF16) for u in range(pages)], axis=1)
    s = _dot(qb_scr[...], kcat) - LOG2E * ct
    m_prev = m_scr[...]
    m_new = jnp.maximum(m_prev, jnp.max(s, axis=1, keepdims=True))
    alpha = jnp.exp2(m_prev - m_new)
    pr = jnp.exp2(s - m_new)
    l_scr[...] = alpha * l_scr[...] + jnp.sum(pr, axis=1, keepdims=True)
    acc_scr[...] = alpha * acc_scr[...] + _dot_nt(pr.astype(BF16), vcat)
    m_scr[...] = m_new

    @pl.when(s_id == n_steps - 1)
    def _():
        c_past = jnp.concatenate([carry[...], jnp.zeros_like(carry)], axis=0)
        s_new = (jnp.sum(qf_scr[...] * kn_ref[0], axis=1, keepdims=True)
                 - LOG2E * (c_past + lfn_ref[0]))
        m_prev = m_scr[...]
        m_new = jnp.maximum(m_prev, s_new)
        alpha = jnp.exp2(m_prev - m_new)
        pn = jnp.exp2(s_new - m_new)
        l_fin = alpha * l_scr[...] + pn
        acc = alpha * acc_scr[...] + pn * vn_ref[0]
        o_ref[0] = jnp.sum(jnp.where(own, acc / l_fin, 0.0), axis=0, keepdims=True)


def _fox_sample(layer, q, k_new, v_new, lf_new, cache_k, cache_v, cache_lf, page_table, pages=16):
    n, n_pages = page_table.shape
    page = cache_k.shape[3]
    n_steps = n_pages // pages
    lfn = jnp.pad(lf_new[:, :H_C], ((0, 0), (0, HEAD_ROWS - H_C))).reshape(n, HEAD_ROWS, 1)
    seq3 = lambda w: pl.BlockSpec((1, 1, w), lambda b, s, pt: (b, 0, 0))

    def paged(rows, u):
        return pl.BlockSpec((None, None, rows, page),
                            lambda b, s, pt, u=u: (layer, pt[b, s * pages + u], 0, 0))

    in_specs = ([seq3(W_C), seq3(W_C), seq3(W_C),
                 pl.BlockSpec((1, HEAD_ROWS, 1), lambda b, s, pt: (b, 0, 0))]
                + [paged(W_C, u) for u in range(pages)]
                + [paged(W_C, u) for u in range(pages)]
                + [paged(8, u) for u in range(pages)])
    out = pl.pallas_call(
        functools.partial(_fox_sample_kernel, pages=pages, n_steps=n_steps),
        grid_spec=pltpu.PrefetchScalarGridSpec(
            num_scalar_prefetch=1,
            grid=(n, n_steps),
            in_specs=in_specs,
            out_specs=pl.BlockSpec((1, 1, W_C), lambda b, s, pt: (b, 0, 0)),
            scratch_shapes=[pltpu.VMEM((HEAD_ROWS, W_C), F32), pltpu.VMEM((HEAD_ROWS, W_C), BF16),
                            pltpu.VMEM((HEAD_ROWS, 1), F32), pltpu.VMEM((HEAD_ROWS, 1), F32),
                            pltpu.VMEM((HEAD_ROWS, W_C), F32), pltpu.VMEM((8, 1), F32)]),
        out_shape=jax.ShapeDtypeStruct((n, 1, W_C), F32),
        compiler_params=_params(2),
        name="fox_sample",
    )(page_table, q.reshape(n, 1, W_C), k_new.reshape(n, 1, W_C), v_new.reshape(n, 1, W_C), lfn,
      *([cache_k] * pages), *([cache_v] * pages), *([cache_lf] * pages))
    return out.reshape(n, W_C)


def _outproj_kernel(x_ref, ya_ref, ob_ref, bonus_ref, g_ref, yc_ref, ln_ref, avg_ref, wo_ref, o_ref):
    ob = ob_ref[...]
    mu = _dot_split(ob, avg_ref[...])
    d = ob - mu
    var = _dot((d * d).astype(BF16), avg_ref[...])
    yb = (d * lax.rsqrt(var + GN_EPS) * ln_ref[0:1, :] + ln_ref[1:2, :] + bonus_ref[...]) * g_ref[...]
    acc = _dot(ya_ref[...], wo_ref[0:W_A, :])
    acc += _dot(yb.astype(BF16), wo_ref[W_A:W_A + W_B, :])
    acc += _dot(yc_ref[...], wo_ref[W_A + W_B:, :])
    o_ref[...] = x_ref[...] + acc


def _outproj(x, ya, ob, bonus, g, yc, ln, avg_b, wo, tm=512):
    m = x.shape[0]
    tm = min(tm, m)
    row = lambda i: (i, 0)
    fix = lambda i: (0, 0)
    return pl.pallas_call(
        _outproj_kernel,
        grid=(m // tm,),
        in_specs=[pl.BlockSpec((tm, D_MODEL), row), pl.BlockSpec((tm, W_A), row),
                  pl.BlockSpec((tm, W_B), row), pl.BlockSpec((tm, W_B), row),
                  pl.BlockSpec((tm, W_B), row), pl.BlockSpec((tm, W_C), row),
                  pl.BlockSpec((8, W_B), fix), pl.BlockSpec((W_B, W_B), fix),
                  pl.BlockSpec((D_MODEL, D_MODEL), fix)],
        out_specs=pl.BlockSpec((tm, D_MODEL), row),
        out_shape=jax.ShapeDtypeStruct((m, D_MODEL), F32),
        compiler_params=_params(1),
        name="outproj",
    )(x, ya, ob, bonus, g, yc, ln, avg_b, wo)


def _block_diag_const(width, value):
    idx = jnp.arange(width) // GROUP
    return jnp.where(idx[:, None] == idx[None, :], value, 0.0).astype(BF16)


def _pad_rows(vecs, width):
    rows = [jnp.pad(v, (0, width - v.shape[0])) for v in vecs]
    rows += [jnp.zeros((width,), F32)] * (8 - len(rows))
    return jnp.stack(rows)


def _layer_weights(l, norm_g, w_ffn_in, w_ffn_out, w_in, a_ws, a_bs, a_norm_g, b_mu, b_w0, b_wB,
                   b_a0, b_aB, b_gB, b_kk, b_ka, b_rk, b_ln_g, b_ln_b, c_fb, w_o):
    zeros_lora = jnp.zeros((R_DECAY, W_B), F32)
    return dict(
        norm_g=norm_g[l],
        w_ffn_in=w_ffn_in[l].astype(BF16),
        w_ffn_out=w_ffn_out[l].astype(BF16),
        w_in=jnp.pad(w_in[l], ((0, 0), (0, IN_PROJ_PAD - IN_PROJ))).astype(BF16),
        wkv_t=w_in[l][:, A_PROJ + B_PROJ + W_C:A_PROJ + B_PROJ + 3 * W_C].T.astype(BF16),
        c_fb=jnp.pad(c_fb[l], (0, LANES - H_C)).reshape(1, LANES),
        a_ws=a_ws[l],
        a_bias=jnp.repeat(a_bs[l].T, GROUP, axis=1),
        a_norm_g=a_norm_g[l],
        b_vec=_pad_rows([b_mu[l], b_w0[l], b_a0[l], b_kk[l], b_ka[l], b_rk[l]], B_PROJ),
        b_wB=jnp.concatenate([b_wB[l], zeros_lora], axis=0).astype(BF16),
        b_aB=jnp.concatenate([zeros_lora, b_aB[l]], axis=0).astype(BF16),
        b_gB=b_gB[l].astype(BF16),
        b_ln=_pad_rows([b_ln_g[l], b_ln_b[l]], W_B),
        w_o=w_o[l].astype(BF16),
        ones_b=_block_diag_const(W_B, 1.0),
        avg_b=_block_diag_const(W_B, 1.0 / GROUP),
        avg_a=_block_diag_const(W_A, 1.0 / GROUP),
    )


def _mix_prompt(wts, z, n_seq, seq_len):
    za, zb, q, lf, _, _, kc16, vct16 = z
    ya, _ = _gmlp(za, wts["a_norm_g"], wts["a_ws"], wts["a_bias"], wts["avg_a"])
    r, lw, k2, vb, kk, beta, g, bonus = _rwkv_prep(zb, None, seq_len, wts)
    x1, x2, ub, op, sp, gam = _rwkv_chunks(r, lw, k2, vb, kk, beta)
    ob, st = _rwkv_scan(x1, x2, ub, op, sp, gam, n_seq, seq_len)
    yc = _fox_prompt(q, kc16, vct16, _cumsum(lf, n_seq, seq_len), n_seq, seq_len)
    st = st.reshape(n_seq, N_PAIR, 2, GROUP, 2, GROUP)
    wkv = jnp.stack([st[:, :, 0, :, 0, :], st[:, :, 1, :, 1, :]], axis=2)
    wkv = wkv.reshape(n_seq, H_B, GROUP, GROUP).transpose(0, 1, 3, 2)
    return ya, ob.reshape(n_seq * seq_len, W_B), bonus, g, yc, wkv


def _mix_sample(wts, layer, z, shift0, wkv0, cache_k, cache_v, cache_lf, page_table):
    za, zb, q, lf, k, v = z
    n = za.shape[0]
    za_pad = jnp.pad(za[:, None, :], ((0, 0), (0, CHUNK_A - 1), (0, 0))).reshape(n * CHUNK_A, A_PROJ)
    ya, va = _gmlp(za_pad, wts["a_norm_g"], wts["a_ws"], wts["a_bias"], wts["avg_a"])
    ya = ya.reshape(n, CHUNK_A, W_A)[:, 0]
    va = va.reshape(n, CHUNK_A, W_A)[:, 0]
    r, lw, k2, vb, kk, beta, g, bonus = _rwkv_prep(zb, shift0, 1, wts)
    ob, wkv = _rwkv_step(wkv0, r, lw, k2, kk, beta, vb)
    yc = _fox_sample(layer, q, k, v, lf, cache_k, cache_v, cache_lf, page_table)
    return ya, ob, bonus, g, yc.astype(BF16), wkv, va


def kernel(x_prompt, x_sample, cache_k, cache_v, cache_logf, state_wkv, state_shift, page_table,
           norm_g, w_ffn_in, w_ffn_out, w_in, a_ws, a_bs, a_norm_g, b_mu, b_w0, b_wB, b_a0, b_aB,
           b_gB, b_kk, b_ka, b_rk, b_ln_g, b_ln_b, c_fb, w_o, final_norm):
    n_p, seq_len, _ = x_prompt.shape
    n_s = x_sample.shape[0]
    depth = norm_g.shape[0]
    n_phys, page = cache_k.shape[1], cache_k.shape[2]
    ck = jnp.transpose(cache_k, (0, 1, 3, 4, 2)).reshape(depth, n_phys, W_C, page)
    cv = jnp.transpose(cache_v, (0, 1, 3, 4, 2)).reshape(depth, n_phys, W_C, page)
    clf = jnp.pad(jnp.transpose(cache_logf, (0, 1, 3, 2)), ((0, 0), (0, 0), (0, 8 - H_C), (0, 0)))
    xp = x_prompt.reshape(n_p * seq_len, D_MODEL)
    xs = x_sample.reshape(n_s, D_MODEL)
    outs = {name: [] for name in ("kp", "vp", "lfp", "wkvp", "shp", "ks", "vs", "lfs", "wkvs", "shs", "va")}
    for l in range(depth):
        wts = _layer_weights(l, norm_g, w_ffn_in, w_ffn_out, w_in, a_ws, a_bs, a_norm_g, b_mu, b_w0,
                             b_wB, b_a0, b_aB, b_gB, b_kk, b_ka, b_rk, b_ln_g, b_ln_b, c_fb, w_o)
        last = l == depth - 1
        fin = final_norm if last else None

        xp = _ffn(xp, wts["norm_g"][0], wts["w_ffn_in"][0], wts["w_ffn_out"][0])
        zp = _inproj(xp, wts["norm_g"][1], wts["w_in"], wts["wkv_t"], wts["c_fb"], seq_len=seq_len)
        ya, ob, bonus, g, yc, wkvp = _mix_prompt(wts, zp, n_p, seq_len)
        xp = _outproj(xp, ya, ob, bonus, g, yc, wts["b_ln"], wts["avg_b"], wts["w_o"])
        xp = _ffn(xp, wts["norm_g"][2], wts["w_ffn_in"][1], wts["w_ffn_out"][1], final_g=fin)
        outs["kp"].append(zp[4].reshape(n_p, H_C, GROUP, seq_len).transpose(0, 3, 1, 2))
        outs["vp"].append(zp[5].reshape(n_p, H_C, GROUP, seq_len).transpose(0, 3, 1, 2))
        outs["lfp"].append(zp[3][:, :H_C].reshape(n_p, seq_len, H_C))
        outs["wkvp"].append(wkvp)
        outs["shp"].append(zp[1].reshape(n_p, seq_len, B_PROJ)[:, -1])

        xs = _ffn(xs, wts["norm_g"][0], wts["w_ffn_in"][0], wts["w_ffn_out"][0])
        zs = _inproj(xs, wts["norm_g"][1], wts["w_in"], wts["wkv_t"], wts["c_fb"])
        ya, ob, bonus, g, yc, wkvs, va = _mix_sample(wts, l, zs, state_shift[l], state_wkv[l],
                                                     ck, cv, clf, page_table)
        xs = _outproj(xs, ya, ob, bonus, g, yc, wts["b_ln"], wts["avg_b"], wts["w_o"])
        xs = _ffn(xs, wts["norm_g"][2], wts["w_ffn_in"][1], wts["w_ffn_out"][1], final_g=fin)
        outs["ks"].append(zs[4].reshape(n_s, 1, H_C, GROUP))
        outs["vs"].append(zs[5].reshape(n_s, 1, H_C, GROUP))
        outs["lfs"].append(zs[3][:, :H_C].reshape(n_s, 1, H_C))
        outs["wkvs"].append(wkvs)
        outs["shs"].append(zs[1])
        outs["va"].append(va.reshape(n_s, 1, W_A))
    st = lambda name: jnp.stack(outs[name])
    return (xp.reshape(n_p, seq_len, D_MODEL), xs.reshape(n_s, 1, D_MODEL),
            st("kp"), st("vp"), st("lfp"), st("wkvp"), st("shp"),
            st("ks"), st("vs"), st("lfs"), st("wkvs"), st("shs"), st("va"))
```

```python
import functools

import jax
import jax.numpy as jnp
from jax import lax
from jax.experimental import pallas as pl
from jax.experimental.pallas import tpu as pltpu

F32 = jnp.float32
BF16 = jnp.bfloat16
HIGHEST = lax.Precision.HIGHEST

LANES = 128
D_MODEL = 1024
D_FF = 2816
GROUP = 64
W_A = 256
W_B = 384
W_C = 384
H_B = W_B // GROUP
H_C = W_C // GROUP
N_PAIR = W_B // LANES
R_DECAY = 64
R_AAA = 64
R_GATE = 128
B_PROJ = 3 * W_B + R_DECAY + R_AAA + R_GATE
A_PROJ = 2 * W_A
C_PROJ = 3 * W_C + H_C
IN_PROJ = A_PROJ + B_PROJ + C_PROJ
IN_PROJ_PAD = A_PROJ + B_PROJ + 3 * W_C + LANES
CHUNK_A = 128
CHUNK_B = 64
NORM_EPS = 1e-6
GN_EPS = 64e-5
NEG_BIG = -1e30
LOG2E = 1.4426950408889634
Q_SCALE = LOG2E * GROUP ** -0.5
VMEM_LIMIT = 56 << 20


def _params(n_axes, vmem=VMEM_LIMIT):
    return pltpu.CompilerParams(dimension_semantics=("arbitrary",) * n_axes,
                                vmem_limit_bytes=vmem)


def _sigmoid(x):
    return 1.0 / (1.0 + jnp.exp(-x))


def _softplus(x):
    return jnp.maximum(x, 0.0) + jnp.log(1.0 + jnp.exp(-jnp.abs(x)))


def _gelu_tanh(x):
    return 0.5 * x * (1.0 + jnp.tanh(0.7978845608028654 * (x + 0.044715 * (x * x * x))))


def _rms(x, g):
    return x * lax.rsqrt(jnp.mean(x * x, axis=-1, keepdims=True) + NORM_EPS) * g


def _dot(a, b):
    return jnp.dot(a, b, preferred_element_type=F32)


def _dot_nt(a, b):
    return lax.dot_general(a, b, (((1,), (1,)), ((), ())), preferred_element_type=F32)


def _dot_split(a, b_bf):
    hi = a.astype(BF16)
    lo = (a - hi.astype(F32)).astype(BF16)
    return _dot(hi, b_bf) + _dot(lo, b_bf)


def _iota(shape, dim):
    return lax.broadcasted_iota(jnp.int32, shape, dim)


def _ffn_kernel(*refs, n_ff, final):
    if final:
        x_ref, g_ref, wg_ref, wu_ref, wo_ref, fg_ref, o_ref, h_scr, acc_scr = refs
    else:
        x_ref, g_ref, wg_ref, wu_ref, wo_ref, o_ref, h_scr, acc_scr = refs
    j = pl.program_id(1)

    @pl.when(j == 0)
    def _():
        h_scr[...] = _rms(x_ref[...], g_ref[...]).astype(BF16)
        acc_scr[...] = jnp.zeros_like(acc_scr)

    h = h_scr[...]
    gate = _dot(h, wg_ref[...])
    up = _dot(h, wu_ref[...])
    act = (gate * _sigmoid(gate) * up).astype(BF16)
    acc_scr[...] += _dot(act, wo_ref[...])

    @pl.when(j == n_ff - 1)
    def _():
        y = x_ref[...] + 0.5 * acc_scr[...]
        if final:
            y = _rms(y, fg_ref[...])
        o_ref[...] = y


def _ffn(x, g, w_in, w_out, final_g=None, tm=512, n_ff=2):
    m = x.shape[0]
    tm = min(tm, m)
    tf = D_FF // n_ff
    final = final_g is not None
    in_specs = [
        pl.BlockSpec((tm, D_MODEL), lambda i, j: (i, 0)),
        pl.BlockSpec((1, D_MODEL), lambda i, j: (0, 0)),
        pl.BlockSpec((D_MODEL, tf), lambda i, j: (0, j)),
        pl.BlockSpec((D_MODEL, tf), lambda i, j: (0, j + n_ff)),
        pl.BlockSpec((tf, D_MODEL), lambda i, j: (j, 0)),
    ]
    args = [x, g.reshape(1, D_MODEL), w_in, w_in, w_out]
    if final:
        in_specs.append(pl.BlockSpec((1, D_MODEL), lambda i, j: (0, 0)))
        args.append(final_g.reshape(1, D_MODEL))
    return pl.pallas_call(
        functools.partial(_ffn_kernel, n_ff=n_ff, final=final),
        grid=(m // tm, n_ff),
        in_specs=in_specs,
        out_specs=pl.BlockSpec((tm, D_MODEL), lambda i, j: (i, 0)),
        out_shape=jax.ShapeDtypeStruct((m, D_MODEL), F32),
        scratch_shapes=[pltpu.VMEM((tm, D_MODEL), BF16), pltpu.VMEM((tm, D_MODEL), F32)],
        compiler_params=_params(2),
        name="ffn",
    )(*args)


def _inproj_kernel(x_ref, g_ref, w_ref, wkvt_ref, fb_ref, za_ref, zb_ref, q_ref, lf_ref, *kv_refs,
                   channel_major):
    h = _rms(x_ref[...], g_ref[...]).astype(BF16)
    z = _dot(h, w_ref[...])
    o = A_PROJ
    za_ref[...] = z[:, :o]
    zb_ref[...] = z[:, o:o + B_PROJ]
    o += B_PROJ
    q_ref[...] = (z[:, o:o + W_C] * Q_SCALE).astype(BF16)
    k = z[:, o + W_C:o + 2 * W_C]
    v = z[:, o + 2 * W_C:o + 3 * W_C]
    lf_ref[...] = -_softplus(-(z[:, o + 3 * W_C:] + fb_ref[...]))
    if channel_major:
        kt_ref, vt_ref, kb_ref, vtb_ref = kv_refs
        kt_ref[...] = _dot_nt(wkvt_ref[0:W_C, :], h)
        vt = _dot_nt(wkvt_ref[W_C:, :], h)
        vt_ref[...] = vt
        vtb_ref[...] = vt.astype(BF16)
        kb_ref[...] = k.astype(BF16)
    else:
        k_ref, v_ref = kv_refs
        k_ref[...] = k
        v_ref[...] = v


def _inproj(x, g, w_pad, wkv_t, fb_pad, seq_len=None, tm=512):
    m = x.shape[0]
    tm = min(tm, m)
    row = lambda i: (i, 0)
    fix = lambda i: (0, 0)
    widths = (A_PROJ, B_PROJ, W_C, LANES)
    dtypes = (F32, F32, BF16, F32)
    out_specs = [pl.BlockSpec((tm, w), row) for w in widths]
    out_shape = [jax.ShapeDtypeStruct((m, w), d) for w, d in zip(widths, dtypes)]
    if seq_len is None:
        out_specs += [pl.BlockSpec((tm, W_C), row)] * 2
        out_shape += [jax.ShapeDtypeStruct((m, W_C), F32)] * 2
    else:
        bps = seq_len // tm
        seq_blk = pl.BlockSpec((None, W_C, tm), lambda i: (i // bps, 0, i % bps))
        out_specs += [seq_blk, seq_blk, pl.BlockSpec((tm, W_C), row), pl.BlockSpec((W_C, tm), lambda i: (0, i))]
        out_shape += [jax.ShapeDtypeStruct((m // seq_len, W_C, seq_len), F32)] * 2
        out_shape += [jax.ShapeDtypeStruct((m, W_C), BF16), jax.ShapeDtypeStruct((W_C, m), BF16)]
    return pl.pallas_call(
        functools.partial(_inproj_kernel, channel_major=seq_len is not None),
        grid=(m // tm,),
        in_specs=[pl.BlockSpec((tm, D_MODEL), row), pl.BlockSpec((1, D_MODEL), fix),
                  pl.BlockSpec((D_MODEL, IN_PROJ_PAD), fix), pl.BlockSpec((2 * W_C, D_MODEL), fix),
                  pl.BlockSpec((1, LANES), fix)],
        out_specs=out_specs,
        out_shape=out_shape,
        compiler_params=_params(1),
        name="inproj",
    )(x, g.reshape(1, D_MODEL), w_pad, wkv_t, fb_pad)


def _gmlp_kernel(za_ref, gain_ref, ws_ref, bias_ref, avg_ref, ya_ref, va_ref, *, n_chunks):
    z = _gelu_tanh(za_ref[...])
    u = z[:, :W_A]
    v = z[:, W_A:]
    ms = _dot((v * v).astype(BF16), avg_ref[...])
    vn = v * lax.rsqrt(ms + NORM_EPS) * gain_ref[...]
    va_ref[...] = vn
    causal = _iota((CHUNK_A, CHUNK_A), 0) >= _iota((CHUNK_A, CHUNK_A), 1)
    lane_group = _iota((CHUNK_A, W_A), 1) // GROUP
    wm = [jnp.where(causal, ws_ref[g], 0.0).astype(BF16) for g in range(W_A // GROUP)]
    for c in range(n_chunks):
        rows = slice(c * CHUNK_A, (c + 1) * CHUNK_A)
        vc = vn[rows]
        s = bias_ref[...]
        for g in range(W_A // GROUP):
            s = s + _dot(wm[g], jnp.where(lane_group == g, vc, 0.0).astype(BF16))
        ya_ref[rows, :] = (u[rows] * s).astype(BF16)


def _gmlp(za, gain, ws, bias_full, avg_a, tm=512):
    m = za.shape[0]
    tm = min(tm, m)
    row = lambda i: (i, 0)
    fix = lambda i: (0, 0)
    return pl.pallas_call(
        functools.partial(_gmlp_kernel, n_chunks=tm // CHUNK_A),
        grid=(m // tm,),
        in_specs=[pl.BlockSpec((tm, A_PROJ), row), pl.BlockSpec((1, W_A), fix),
                  pl.BlockSpec((W_A // GROUP, CHUNK_A, CHUNK_A), lambda i: (0, 0, 0)),
                  pl.BlockSpec((CHUNK_A, W_A), fix), pl.BlockSpec((W_A, W_A), fix)],
        out_specs=[pl.BlockSpec((tm, W_A), row), pl.BlockSpec((tm, W_A), row)],
        out_shape=[jax.ShapeDtypeStruct((m, W_A), BF16), jax.ShapeDtypeStruct((m, W_A), F32)],
        compiler_params=_params(1),
        name="gmlp",
    )(za, gain.reshape(1, W_A), ws, bias_full, avg_a)


def _rwkv_prep_math(zb, prev, vec_ref, wb_ref, ab_ref, gb_ref, ones_ref, outs):
    r_ref, lw_ref, k_ref, v_ref, kk_ref, beta_ref, g_ref, bonus_ref = outs
    mu = vec_ref[0:1, :]
    zs = zb + (prev - zb) * mu
    r = zs[:, :W_B]
    k = zs[:, W_B:2 * W_B]
    v = zs[:, 2 * W_B:3 * W_B]
    lora = zs[:, 3 * W_B:3 * W_B + LANES]
    gl = zs[:, 3 * W_B + LANES:]
    w0 = vec_ref[1:2, :W_B]
    a0 = vec_ref[2:3, :W_B]
    kkw = vec_ref[3:4, :W_B]
    kaw = vec_ref[4:5, :W_B]
    rkw = vec_ref[5:6, :W_B]
    w = -_softplus(-(w0 + _dot(jnp.tanh(lora).astype(BF16), wb_ref[...]))) - 0.5
    a = _sigmoid(a0 + _dot(lora.astype(BF16), ab_ref[...]))
    g = _dot(_sigmoid(gl).astype(BF16), gb_ref[...])
    kk = k * kkw
    ss = _dot_split(kk * kk, ones_ref[...])
    kk = kk / jnp.maximum(jnp.sqrt(ss), 1e-12)
    k2 = k * (1.0 + (a - 1.0) * kaw)
    r_ref[...] = r
    lw_ref[...] = -jnp.exp(w)
    k_ref[...] = k2
    v_ref[...] = v
    kk_ref[...] = kk
    beta_ref[...] = kk * a
    g_ref[...] = g
    bonus_ref[...] = _dot_split(r * k2 * rkw, ones_ref[...]) * v


def _rwkv_prep_seq_kernel(zb_ref, pb_ref, vec_ref, wb_ref, ab_ref, gb_ref, ones_ref, *outs,
                          blocks_per_seq):
    i = pl.program_id(0)
    zb = zb_ref[...]
    tm = zb.shape[0]
    first = (i % blocks_per_seq) == 0
    last_prev = jnp.where(first, 0.0, pb_ref[7:8, :])
    prev = jnp.where(_iota((tm, 1), 0) == 0, last_prev, pltpu.roll(zb, shift=1, axis=0))
    _rwkv_prep_math(zb, prev, vec_ref, wb_ref, ab_ref, gb_ref, ones_ref, outs)


def _rwkv_prep_tok_kernel(zb_ref, prev_ref, vec_ref, wb_ref, ab_ref, gb_ref, ones_ref, *outs):
    _rwkv_prep_math(zb_ref[...], prev_ref[...], vec_ref, wb_ref, ab_ref, gb_ref, ones_ref, outs)


def _rwkv_prep(zb, prev, seq_len, wts, tm=512):
    m = zb.shape[0]
    tm = min(tm, m)
    row = lambda i: (i, 0)
    fix = lambda i: (0, 0)
    w_specs = [pl.BlockSpec((8, B_PROJ), fix), pl.BlockSpec((LANES, W_B), fix),
               pl.BlockSpec((LANES, W_B), fix), pl.BlockSpec((R_GATE, W_B), fix),
               pl.BlockSpec((W_B, W_B), fix)]
    w_args = [wts["b_vec"], wts["b_wB"], wts["b_aB"], wts["b_gB"], wts["ones_b"]]
    if prev is None:
        kern = functools.partial(_rwkv_prep_seq_kernel, blocks_per_seq=seq_len // tm)
        sub = tm // 8
        in_specs = [pl.BlockSpec((tm, B_PROJ), row),
                    pl.BlockSpec((8, B_PROJ), lambda i: (jnp.maximum(i * sub - 1, 0), 0))]
        args = [zb, zb]
    else:
        kern = _rwkv_prep_tok_kernel
        in_specs = [pl.BlockSpec((tm, B_PROJ), row), pl.BlockSpec((tm, B_PROJ), row)]
        args = [zb, prev]
    return pl.pallas_call(
        kern,
        grid=(m // tm,),
        in_specs=in_specs + w_specs,
        out_specs=[pl.BlockSpec((tm, W_B), row)] * 8,
        out_shape=[jax.ShapeDtypeStruct((m, W_B), F32)] * 8,
        compiler_params=_params(1),
        name="rwkv_prep",
    )(*args, *w_args)


def _stack(x, low):
    return jnp.concatenate([jnp.where(low, x, 0.0), jnp.where(low, 0.0, x)], axis=0)


CHUNKS_PER_STEP = 4

def _rwkv_chunk_kernel(r_ref, lw_ref, k_ref, v_ref, kk_ref, beta_ref,
                       x1_ref, x2_ref, ub_ref, op_ref, sp_ref, gam_ref):
    c = CHUNK_B
    n2 = 2 * c
    rows = CHUNKS_PER_STEP * c
    ri = _iota((rows, rows), 0)
    ci = _iota((rows, rows), 1)
    tri = jnp.where(ri >= ci, 1.0, 0.0) * jnp.where((ri // c) == (ci // c), 1.0, 0.0)
    lw = lw_ref[...]
    cum = jnp.dot(tri, lw, precision=HIGHEST, preferred_element_type=F32)
    lasts = [cum[(j + 1) * c - 1:(j + 1) * c, :] for j in range(CHUNKS_PER_STEP)]
    for j in range(CHUNKS_PER_STEP):
        gam_ref[j] = jnp.exp(lasts[j])
    cum_last = jnp.concatenate([jnp.broadcast_to(l, (c, W_B)) for l in lasts], axis=0)
    e_pos = jnp.exp(cum)
    e_neg = jnp.exp(-cum)
    e_tail = jnp.exp(cum_last - cum)
    r_t = r_ref[...] * e_pos
    kap_t = kk_ref[...] * jnp.exp(cum - lw)
    beta_h = beta_ref[...] * e_neg
    k_h = k_ref[...] * e_neg
    beta_c = beta_ref[...] * e_tail
    k_c = k_ref[...] * e_tail
    v = v_ref[...]

    low = _iota((c, LANES), 1) < GROUP
    rr = _iota((n2, n2), 0) & (c - 1)
    cc = _iota((n2, n2), 1) & (c - 1)
    strict = rr > cc
    incl = rr >= cc
    streams = [(j, p) for j in range(CHUNKS_PER_STEP) for p in range(N_PAIR)]

    def tile(x, j, p):
        return _stack(x[j * c:(j + 1) * c, p * LANES:(p + 1) * LANES], low)

    kap_s = [tile(kap_t, j, p) for j, p in streams]
    r_s = [tile(r_t, j, p) for j, p in streams]
    v_s = [tile(v, j, p).astype(BF16) for j, p in streams]
    gram = [_dot_nt(jnp.concatenate([kap_s[i], r_s[i]], axis=0).astype(BF16),
                    jnp.concatenate([tile(beta_h, j, p), tile(k_h, j, p)], axis=0).astype(BF16))
            for i, (j, p) in enumerate(streams)]
    n_bf = [jnp.where(strict, g[:n2, :n2], 0.0).astype(BF16) for g in gram]
    av = [_dot(jnp.where(strict, g[:n2, n2:], 0.0).astype(BF16), vs) for g, vs in zip(gram, v_s)]
    for i, (j, p) in enumerate(streams):
        op_ref[j, p] = _dot(jnp.where(incl, gram[i][n2:, n2:], 0.0).astype(BF16), v_s[i])
        sp_ref[j, p] = _dot(tile(k_c, j, p).T.astype(BF16), v_s[i])
        x2_ref[j, p] = jnp.concatenate([jnp.where(incl, gram[i][n2:, :n2], 0.0),
                                        tile(beta_c, j, p).T], axis=0).astype(BF16)
    x = [jnp.concatenate([ks, -a], axis=1) for ks, a in zip(kap_s, av)]
    x = [xi - _dot(nb, xi.astype(BF16)) for xi, nb in zip(x, n_bf)]
    pw = n_bf
    for _ in range(5):
        pw = [_dot(q, q).astype(BF16) for q in pw]
        x = [xi + _dot(q, xi.astype(BF16)) for xi, q in zip(x, pw)]
    for i, (j, p) in enumerate(streams):
        x1_ref[j, p] = jnp.concatenate([x[i][:, :LANES], r_s[i]], axis=0).astype(BF16)
        ub_ref[j, p] = x[i][:, LANES:]


def _rwkv_chunks(r, lw, k2, v, kk, beta):
    m = r.shape[0]
    nc = m // CHUNK_B
    cb = CHUNKS_PER_STEP
    row = lambda i: (i, 0)
    blk = lambda i: (i, 0, 0, 0)
    t = 2 * CHUNK_B
    return pl.pallas_call(
        _rwkv_chunk_kernel,
        grid=(nc // cb,),
        in_specs=[pl.BlockSpec((cb * CHUNK_B, W_B), row)] * 6,
        out_specs=[pl.BlockSpec((cb, N_PAIR, 2 * t, LANES), blk),
                   pl.BlockSpec((cb, N_PAIR, 2 * t, LANES), blk),
                   pl.BlockSpec((cb, N_PAIR, t, LANES), blk),
                   pl.BlockSpec((cb, N_PAIR, t, LANES), blk),
                   pl.BlockSpec((cb, N_PAIR, t, LANES), blk),
                   pl.BlockSpec((cb, 1, W_B), lambda i: (i, 0, 0))],
        out_shape=[jax.ShapeDtypeStruct((nc, N_PAIR, 2 * t, LANES), BF16),
                   jax.ShapeDtypeStruct((nc, N_PAIR, 2 * t, LANES), BF16),
                   jax.ShapeDtypeStruct((nc, N_PAIR, t, LANES), F32),
                   jax.ShapeDtypeStruct((nc, N_PAIR, t, LANES), F32),
                   jax.ShapeDtypeStruct((nc, N_PAIR, t, LANES), F32),
                   jax.ShapeDtypeStruct((nc, 1, W_B), F32)],
        compiler_params=_params(1),
        name="rwkv_chunks",
    )(r, lw, k2, v, kk, beta)


def _rwkv_scan_kernel(x1_ref, x2_ref, ub_ref, op_ref, sp_ref, gam_ref, o_ref, st_ref, st_scr,
                      *, n_seq, n_chunks):
    ci = pl.program_id(0)
    t = 2 * CHUNK_B

    @pl.when(ci == 0)
    def _():
        st_scr[...] = jnp.zeros_like(st_scr)

    eye = _iota((t, t), 0) == _iota((t, t), 1)
    streams = [(b, p) for b in range(n_seq) for p in range(N_PAIR)]
    st = [st_scr[b * N_PAIR + p] for b, p in streams]
    y = [_dot(x1_ref[b, 0, p], s.astype(BF16)) for (b, p), s in zip(streams, st)]
    u = [ub_ref[b, 0, p] - yi[:t] for (b, p), yi in zip(streams, y)]
    z = [_dot(x2_ref[b, 0, p], ui.astype(BF16)) for (b, p), ui in zip(streams, u)]
    for i, (b, p) in enumerate(streams):
        o_s = op_ref[b, 0, p] + y[i][t:] + z[i][:t]
        gam_row = gam_ref[b, 0, :, p * LANES:(p + 1) * LANES]
        gam_col = jnp.sum(jnp.where(eye, gam_row, 0.0), axis=1, keepdims=True)
        st_scr[b * N_PAIR + p] = gam_col * st[i] + z[i][t:] + sp_ref[b, 0, p]
        o_ref[b, :, p * LANES:(p + 1) * LANES] = o_s[:CHUNK_B] + o_s[CHUNK_B:]

    @pl.when(ci == n_chunks - 1)
    def _():
        st_ref[...] = st_scr[...]


def _rwkv_scan(x1, x2, ub, op, sp, gam, n_seq, seq_len):
    nc = seq_len // CHUNK_B
    t = 2 * CHUNK_B
    r5 = lambda a: a.reshape((n_seq, nc) + a.shape[1:])
    blk5 = lambda rows: pl.BlockSpec((n_seq, 1, N_PAIR, rows, LANES), lambda c: (0, c, 0, 0, 0))
    return pl.pallas_call(
        functools.partial(_rwkv_scan_kernel, n_seq=n_seq, n_chunks=nc),
        grid=(nc,),
        in_specs=[blk5(2 * t), blk5(2 * t), blk5(t), blk5(t), blk5(t),
                  pl.BlockSpec((n_seq, 1, 1, W_B), lambda c: (0, c, 0, 0))],
        out_specs=[pl.BlockSpec((n_seq, CHUNK_B, W_B), lambda c: (0, c, 0)),
                   pl.BlockSpec((n_seq * N_PAIR, t, LANES), lambda c: (0, 0, 0))],
        out_shape=[jax.ShapeDtypeStruct((n_seq, seq_len, W_B), F32),
                   jax.ShapeDtypeStruct((n_seq * N_PAIR, t, LANES), F32)],
        scratch_shapes=[pltpu.VMEM((n_seq * N_PAIR, t, LANES), F32)],
        compiler_params=_params(1),
        name="rwkv_scan",
    )(r5(x1), r5(x2), r5(ub), r5(op), r5(sp), r5(gam))


def _rwkv_step_kernel(s_ref, r_ref, lw_ref, k_ref, kk_ref, beta_ref, vcol_ref, o_ref, so_ref):
    for h in range(H_B):
        s = s_ref[0, h]
        sk = jnp.sum(s * kk_ref[0, h], axis=1, keepdims=True)
        s_new = s * jnp.exp(lw_ref[0, h]) - sk * beta_ref[0, h] + vcol_ref[0, h] * k_ref[0, h]
        so_ref[0, h] = s_new
        o_ref[0, h] = jnp.sum(s_new * r_ref[0, h], axis=1, keepdims=True)


def _rwkv_step(state, r, lw, k2, kk, beta, v):
    n = state.shape[0]
    rowv = lambda a: a.reshape(n, H_B, 1, GROUP)
    idx = lambda b: (b, 0, 0, 0)
    row_spec = pl.BlockSpec((1, H_B, 1, GROUP), idx)
    col_spec = pl.BlockSpec((1, H_B, GROUP, 1), idx)
    mat_spec = pl.BlockSpec((1, H_B, GROUP, GROUP), idx)
    o, s_new = pl.pallas_call(
        _rwkv_step_kernel,
        grid=(n,),
        in_specs=[mat_spec] + [row_spec] * 5 + [col_spec],
        out_specs=[col_spec, mat_spec],
        out_shape=[jax.ShapeDtypeStruct((n, H_B, GROUP, 1), F32),
                   jax.ShapeDtypeStruct((n, H_B, GROUP, GROUP), F32)],
        compiler_params=_params(1),
        name="rwkv_step",
    )(state, rowv(r), rowv(lw), rowv(k2), rowv(kk), rowv(beta), v.reshape(n, H_B, GROUP, 1))
    return o.reshape(n, W_B), s_new


BIAS_PIECES = 3


def _cumsum_kernel(lf_ref, place_ref, b_ref, carry):
    @pl.when(pl.program_id(1) == 0)
    def _():
        carry[...] = jnp.zeros_like(carry)

    tb = lf_ref.shape[0]
    tri = (_iota((tb, tb), 0) >= _iota((tb, tb), 1)).astype(F32)
    c = jnp.dot(tri, lf_ref[...], precision=HIGHEST, preferred_element_type=F32) + carry[...]
    carry[...] = c[tb - 1:tb, :]
    rest = -LOG2E * c
    pieces = []
    for _ in range(BIAS_PIECES):
        piece = rest.astype(BF16)
        pieces.append(piece)
        rest = rest - piece.astype(F32)
    b_ref[...] = _dot(jnp.concatenate(pieces, axis=1), place_ref[...]).astype(BF16)


def _bias_placement():
    rows = jnp.arange(BIAS_PIECES * LANES)
    piece, head = rows // LANES, rows % LANES
    col = LANES * (head // 2) + jnp.where(head % 2 == 0, GROUP, 0) + piece
    hit = (col[:, None] == jnp.arange(W_C)[None, :]) & (head < H_C)[:, None]
    return hit.astype(BF16)


def _cumsum(lf, n_seq, seq_len, tb=512):
    nb = seq_len // tb
    return pl.pallas_call(
        _cumsum_kernel,
        grid=(n_seq, nb),
        in_specs=[pl.BlockSpec((tb, LANES), lambda b, j: (b * nb + j, 0)),
                  pl.BlockSpec((BIAS_PIECES * LANES, W_C), lambda b, j: (0, 0))],
        out_specs=pl.BlockSpec((tb, W_C), lambda b, j: (b * nb + j, 0)),
        out_shape=jax.ShapeDtypeStruct((n_seq * seq_len, W_C), BF16),
        scratch_shapes=[pltpu.VMEM((1, LANES), F32)],
        compiler_params=_params(2),
        name="logf_cumsum",
    )(lf, _bias_placement())


def _fox_prompt_kernel(qi_ref, kj_ref, q_ref, k_ref, vt_ref, b_ref, o_ref, qa_scr, m_scr, acc_scr):
    t = pl.program_id(1)
    i = qi_ref[t]
    j = kj_ref[t]
    tq = q_ref.shape[0]
    tk = k_ref.shape[0]
    low_row = _iota((LANES, tq), 0) < GROUP

    @pl.when(j == 0)
    def _():
        lane = _iota((tq, LANES), 1)
        ones_hi = jnp.where(lane < GROUP + BIAS_PIECES, 1.0, 0.0).astype(BF16)
        ones_lo = jnp.where(lane < BIAS_PIECES, 1.0, 0.0).astype(BF16)
        for p in range(N_PAIR):
            q = q_ref[:, p * LANES:(p + 1) * LANES]
            qa_scr[2 * p] = jnp.where(lane < GROUP, q, ones_hi)
            qa_scr[2 * p + 1] = jnp.where(lane < GROUP, ones_lo, q)
        m_scr[...] = jnp.full_like(m_scr, NEG_BIG)
        acc_scr[...] = jnp.zeros_like(acc_scr)

    def step(diagonal):
        low_k = _iota((tk, LANES), 1) < GROUP
        low_v = _iota((LANES, tk), 0) < GROUP
        ka, va = [], []
        for p in range(N_PAIR):
            k = k_ref[:, p * LANES:(p + 1) * LANES]
            bias = b_ref[:, p * LANES:(p + 1) * LANES]
            vt = vt_ref[p * LANES:(p + 1) * LANES, :]
            one = jnp.ones_like(vt)
            ka += [jnp.where(low_k, k, bias), jnp.where(low_k, bias, k)]
            va += [jnp.where(low_v, vt, one), jnp.where(low_v, one, vt)]
        scores = [_dot_nt(ka[h], qa_scr[h]) for h in range(H_C)]
        for h in range(H_C):
            s = scores[h]
            if diagonal:
                s = jnp.where(_iota((tk, tq), 0) <= _iota((tk, tq), 1), s, NEG_BIG)
            m_prev = m_scr[h]
            m_new = jnp.maximum(m_prev, jnp.max(s, axis=0, keepdims=True))
            alpha = jnp.exp2(m_prev - m_new)
            pr = jnp.exp2(s - m_new).astype(BF16)
            acc_scr[h] = alpha * acc_scr[h] + _dot(va[h], pr)
            m_scr[h] = m_new

    @pl.when(j < i)
    def _():
        step(False)

    @pl.when(j == i)
    def _():
        step(True)
        for p in range(N_PAIR):
            a0 = acc_scr[2 * p]
            a1 = acc_scr[2 * p + 1]
            out = jnp.where(low_row, a0 / a0[LANES - 1:LANES, :], a1 / a1[0:1, :])
            o_ref[:, p * LANES:(p + 1) * LANES] = out.T.astype(BF16)


def _fox_prompt(q, k, vt, bias, n_seq, seq_len, tq=512):
    nq = seq_len // tq
    pairs = [(i, j) for i in range(nq) for j in range(i + 1)]
    qi = jnp.asarray([i for i, _ in pairs], jnp.int32)
    kj = jnp.asarray([j for _, j in pairs], jnp.int32)
    q_spec = pl.BlockSpec((tq, W_C), lambda b, t, qi, kj: (b * nq + qi[t], 0))
    k_spec = pl.BlockSpec((tq, W_C), lambda b, t, qi, kj: (b * nq + kj[t], 0))
    vt_spec = pl.BlockSpec((W_C, tq), lambda b, t, qi, kj: (0, b * nq + kj[t]))
    return pl.pallas_call(
        _fox_prompt_kernel,
        grid_spec=pltpu.PrefetchScalarGridSpec(
            num_scalar_prefetch=2,
            grid=(n_seq, len(pairs)),
            in_specs=[q_spec, k_spec, vt_spec, k_spec],
            out_specs=q_spec,
            scratch_shapes=[pltpu.VMEM((H_C, tq, LANES), BF16), pltpu.VMEM((H_C, 1, tq), F32),
                            pltpu.VMEM((H_C, LANES, tq), F32)]),
        out_shape=jax.ShapeDtypeStruct((n_seq * seq_len, W_C), BF16),
        compiler_params=_params(2),
        name="fox_prompt",
    )(qi, kj, q, k, vt, bias)


HEAD_ROWS = 16


def _fox_sample_kernel(pt_ref, q_ref, kn_ref, vn_ref, lfn_ref, *refs, pages, n_steps):
    k_refs = refs[:pages]
    v_refs = refs[pages:2 * pages]
    lf_refs = refs[2 * pages:3 * pages]
    o_ref, qf_scr, qb_scr, m_scr, l_scr, acc_scr, carry = refs[3 * pages:]
    s_id = pl.program_id(1)
    page = k_refs[0].shape[1]
    head_of_lane = _iota((HEAD_ROWS, W_C), 1) // GROUP
    own = head_of_lane == _iota((HEAD_ROWS, W_C), 0)

    @pl.when(s_id == 0)
    def _():
        qrows = jnp.where(own, q_ref[0].astype(F32), 0.0)
        qf_scr[...] = qrows
        qb_scr[...] = qrows.astype(BF16)
        m_scr[...] = jnp.full_like(m_scr, NEG_BIG)
        l_scr[...] = jnp.zeros_like(l_scr)
        acc_scr[...] = jnp.zeros_like(acc_scr)
        carry[...] = jnp.zeros_like(carry)

    upto = (_iota((page, page), 0) <= _iota((page, page), 1)).astype(F32)
    lf_all = jnp.concatenate([lf_refs[u][...] for u in range(pages)], axis=0)
    c_all = jnp.dot(lf_all, upto, precision=HIGHEST, preferred_element_type=F32)
    totals = [c_all[8 * u:8 * (u + 1), page - 1:page] for u in range(pages)]
    run = carry[...]
    cts = []
    for u in range(pages):
        cts.append(c_all[8 * u:8 * (u + 1)] + run)
        run = run + totals[u]
    carry[...] = run
    ct = jnp.concatenate(cts, axis=1)
    ct = jnp.concatenate([ct, jnp.zeros_like(ct)], axis=0)
    kcat = jnp.concatenate([k_refs[u][...].astype(BF16) for u in range(pages)], axis=1)
    vcat = jnp.concatenate([v_refs[u][...].astype(BF16) for u in range(pages)], axis=1)
    s = _dot(qb_scr[...], kcat) - LOG2E * ct
    m_prev = m_scr[...]
    m_new = jnp.maximum(m_prev, jnp.max(s, axis=1, keepdims=True))
    alpha = jnp.exp2(m_prev - m_new)
    pr = jnp.exp2(s - m_new)
    l_scr[...] = alpha * l_scr[...] + jnp.sum(pr, axis=1, keepdims=True)
    acc_scr[...] = alpha * acc_scr[...] + _dot_nt(pr.astype(BF16), vcat)
    m_scr[...] = m_new

    @pl.when(s_id == n_steps - 1)
    def _():
        c_past = jnp.concatenate([carry[...], jnp.zeros_like(carry)], axis=0)
        s_new = (jnp.sum(qf_scr[...] * kn_ref[0], axis=1, keepdims=True)
                 - LOG2E * (c_past + lfn_ref[0]))
        m_prev = m_scr[...]
        m_new = jnp.maximum(m_prev, s_new)
        alpha = jnp.exp2(m_prev - m_new)
        pn = jnp.exp2(s_new - m_new)
        l_fin = alpha * l_scr[...] + pn
        acc = alpha * acc_scr[...] + pn * vn_ref[0]
        o_ref[0] = jnp.sum(jnp.where(own, acc / l_fin, 0.0), axis=0, keepdims=True)


def _fox_sample(layer, q, k_new, v_new, lf_new, cache_k, cache_v, cache_lf, page_table, pages=16):
    n, n_pages = page_table.shape
    page = cache_k.shape[3]
    n_steps = n_pages // pages
    lfn = jnp.pad(lf_new[:, :H_C], ((0, 0), (0, HEAD_ROWS - H_C))).reshape(n, HEAD_ROWS, 1)
    seq3 = lambda w: pl.BlockSpec((1, 1, w), lambda b, s, pt: (b, 0, 0))

    def paged(rows, u):
        return pl.BlockSpec((None, None, rows, page),
                            lambda b, s, pt, u=u: (layer, pt[b, s * pages + u], 0, 0))

    in_specs = ([seq3(W_C), seq3(W_C), seq3(W_C),
                 pl.BlockSpec((1, HEAD_ROWS, 1), lambda b, s, pt: (b, 0, 0))]
                + [paged(W_C, u) for u in range(pages)]
                + [paged(W_C, u) for u in range(pages)]
                + [paged(8, u) for u in range(pages)])
    out = pl.pallas_call(
        functools.partial(_fox_sample_kernel, pages=pages, n_steps=n_steps),
        grid_spec=pltpu.PrefetchScalarGridSpec(
            num_scalar_prefetch=1,
            grid=(n, n_steps),
            in_specs=in_specs,
            out_specs=pl.BlockSpec((1, 1, W_C), lambda b, s, pt: (b, 0, 0)),
            scratch_shapes=[pltpu.VMEM((HEAD_ROWS, W_C), F32), pltpu.VMEM((HEAD_ROWS, W_C), BF16),
                            pltpu.VMEM((HEAD_ROWS, 1), F32), pltpu.VMEM((HEAD_ROWS, 1), F32),
                            pltpu.VMEM((HEAD_ROWS, W_C), F32), pltpu.VMEM((8, 1), F32)]),
        out_shape=jax.ShapeDtypeStruct((n, 1, W_C), F32),
        compiler_params=_params(2),
        name="fox_sample",
    )(page_table, q.reshape(n, 1, W_C), k_new.reshape(n, 1, W_C), v_new.reshape(n, 1, W_C), lfn,
      *([cache_k] * pages), *([cache_v] * pages), *([cache_lf] * pages))
    return out.reshape(n, W_C)


def _outproj_kernel(x_ref, ya_ref, ob_ref, bonus_ref, g_ref, yc_ref, ln_ref, avg_ref, wo_ref, o_ref):
    ob = ob_ref[...]
    mu = _dot_split(ob, avg_ref[...])
    d = ob - mu
    var = _dot((d * d).astype(BF16), avg_ref[...])
    yb = (d * lax.rsqrt(var + GN_EPS) * ln_ref[0:1, :] + ln_ref[1:2, :] + bonus_ref[...]) * g_ref[...]
    acc = _dot(ya_ref[...], wo_ref[0:W_A, :])
    acc += _dot(yb.astype(BF16), wo_ref[W_A:W_A + W_B, :])
    acc += _dot(yc_ref[...], wo_ref[W_A + W_B:, :])
    o_ref[...] = x_ref[...] + acc


def _outproj(x, ya, ob, bonus, g, yc, ln, avg_b, wo, tm=512):
    m = x.shape[0]
    tm = min(tm, m)
    row = lambda i: (i, 0)
    fix = lambda i: (0, 0)
    return pl.pallas_call(
        _outproj_kernel,
        grid=(m // tm,),
        in_specs=[pl.BlockSpec((tm, D_MODEL), row), pl.BlockSpec((tm, W_A), row),
                  pl.BlockSpec((tm, W_B), row), pl.BlockSpec((tm, W_B), row),
                  pl.BlockSpec((tm, W_B), row), pl.BlockSpec((tm, W_C), row),
                  pl.BlockSpec((8, W_B), fix), pl.BlockSpec((W_B, W_B), fix),
                  pl.BlockSpec((D_MODEL, D_MODEL), fix)],
        out_specs=pl.BlockSpec((tm, D_MODEL), row),
        out_shape=jax.ShapeDtypeStruct((m, D_MODEL), F32),
        compiler_params=_params(1),
        name="outproj",
    )(x, ya, ob, bonus, g, yc, ln, avg_b, wo)


def _block_diag_const(width, value):
    idx = jnp.arange(width) // GROUP
    return jnp.where(idx[:, None] == idx[None, :], value, 0.0).astype(BF16)


def _pad_rows(vecs, width):
    rows = [jnp.pad(v, (0, width - v.shape[0])) for v in vecs]
    rows += [jnp.zeros((width,), F32)] * (8 - len(rows))
    return jnp.stack(rows)


def _layer_weights(l, norm_g, w_ffn_in, w_ffn_out, w_in, a_ws, a_bs, a_norm_g, b_mu, b_w0, b_wB,
                   b_a0, b_aB, b_gB, b_kk, b_ka, b_rk, b_ln_g, b_ln_b, c_fb, w_o):
    zeros_lora = jnp.zeros((R_DECAY, W_B), F32)
    return dict(
        norm_g=norm_g[l],
        w_ffn_in=w_ffn_in[l].astype(BF16),
        w_ffn_out=w_ffn_out[l].astype(BF16),
        w_in=jnp.pad(w_in[l], ((0, 0), (0, IN_PROJ_PAD - IN_PROJ))).astype(BF16),
        wkv_t=w_in[l][:, A_PROJ + B_PROJ + W_C:A_PROJ + B_PROJ + 3 * W_C].T.astype(BF16),
        c_fb=jnp.pad(c_fb[l], (0, LANES - H_C)).reshape(1, LANES),
        a_ws=a_ws[l],
        a_bias=jnp.repeat(a_bs[l].T, GROUP, axis=1),
        a_norm_g=a_norm_g[l],
        b_vec=_pad_rows([b_mu[l], b_w0[l], b_a0[l], b_kk[l], b_ka[l], b_rk[l]], B_PROJ),
        b_wB=jnp.concatenate([b_wB[l], zeros_lora], axis=0).astype(BF16),
        b_aB=jnp.concatenate([zeros_lora, b_aB[l]], axis=0).astype(BF16),
        b_gB=b_gB[l].astype(BF16),
        b_ln=_pad_rows([b_ln_g[l], b_ln_b[l]], W_B),
        w_o=w_o[l].astype(BF16),
        ones_b=_block_diag_const(W_B, 1.0),
        avg_b=_block_diag_const(W_B, 1.0 / GROUP),
        avg_a=_block_diag_const(W_A, 1.0 / GROUP),
    )


def _mix_prompt(wts, z, n_seq, seq_len):
    za, zb, q, lf, _, _, kc16, vct16 = z
    ya, _ = _gmlp(za, wts["a_norm_g"], wts["a_ws"], wts["a_bias"], wts["avg_a"])
    r, lw, k2, vb, kk, beta, g, bonus = _rwkv_prep(zb, None, seq_len, wts)
    x1, x2, ub, op, sp, gam = _rwkv_chunks(r, lw, k2, vb, kk, beta)
    ob, st = _rwkv_scan(x1, x2, ub, op, sp, gam, n_seq, seq_len)
    yc = _fox_prompt(q, kc16, vct16, _cumsum(lf, n_seq, seq_len), n_seq, seq_len)
    st = st.reshape(n_seq, N_PAIR, 2, GROUP, 2, GROUP)
    wkv = jnp.stack([st[:, :, 0, :, 0, :], st[:, :, 1, :, 1, :]], axis=2)
    wkv = wkv.reshape(n_seq, H_B, GROUP, GROUP).transpose(0, 1, 3, 2)
    return ya, ob.reshape(n_seq * seq_len, W_B), bonus, g, yc, wkv


def _mix_sample(wts, layer, z, shift0, wkv0, cache_k, cache_v, cache_lf, page_table):
    za, zb, q, lf, k, v = z
    n = za.shape[0]
    za_pad = jnp.pad(za[:, None, :], ((0, 0), (0, CHUNK_A - 1), (0, 0))).reshape(n * CHUNK_A, A_PROJ)
    ya, va = _gmlp(za_pad, wts["a_norm_g"], wts["a_ws"], wts["a_bias"], wts["avg_a"])
    ya = ya.reshape(n, CHUNK_A, W_A)[:, 0]
    va = va.reshape(n, CHUNK_A, W_A)[:, 0]
    r, lw, k2, vb, kk, beta, g, bonus = _rwkv_prep(zb, shift0, 1, wts)
    ob, wkv = _rwkv_step(wkv0, r, lw, k2, kk, beta, vb)
    yc = _fox_sample(layer, q, k, v, lf, cache_k, cache_v, cache_lf, page_table)
    return ya, ob, bonus, g, yc.astype(BF16), wkv, va


def kernel(x_prompt, x_sample, cache_k, cache_v, cache_logf, state_wkv, state_shift, page_table,
           norm_g, w_ffn_in, w_ffn_out, w_in, a_ws, a_bs, a_norm_g, b_mu, b_w0, b_wB, b_a0, b_aB,
           b_gB, b_kk, b_ka, b_rk, b_ln_g, b_ln_b, c_fb, w_o, final_norm):
    n_p, seq_len, _ = x_prompt.shape
    n_s = x_sample.shape[0]
    depth = norm_g.shape[0]
    n_phys, page = cache_k.shape[1], cache_k.shape[2]
    ck = jnp.transpose(cache_k, (0, 1, 3, 4, 2)).reshape(depth, n_phys, W_C, page)
    cv = jnp.transpose(cache_v, (0, 1, 3, 4, 2)).reshape(depth, n_phys, W_C, page)
    clf = jnp.pad(jnp.transpose(cache_logf, (0, 1, 3, 2)), ((0, 0), (0, 0), (0, 8 - H_C), (0, 0)))
    xp = x_prompt.reshape(n_p * seq_len, D_MODEL)
    xs = x_sample.reshape(n_s, D_MODEL)
    outs = {name: [] for name in ("kp", "vp", "lfp", "wkvp", "shp", "ks", "vs", "lfs", "wkvs", "shs", "va")}
    for l in range(depth):
        wts = _layer_weights(l, norm_g, w_ffn_in, w_ffn_out, w_in, a_ws, a_bs, a_norm_g, b_mu, b_w0,
                             b_wB, b_a0, b_aB, b_gB, b_kk, b_ka, b_rk, b_ln_g, b_ln_b, c_fb, w_o)
        last = l == depth - 1
        fin = final_norm if last else None

        xp = _ffn(xp, wts["norm_g"][0], wts["w_ffn_in"][0], wts["w_ffn_out"][0])
        zp = _inproj(xp, wts["norm_g"][1], wts["w_in"], wts["wkv_t"], wts["c_fb"], seq_len=seq_len)
        ya, ob, bonus, g, yc, wkvp = _mix_prompt(wts, zp, n_p, seq_len)
        xp = _outproj(xp, ya, ob, bonus, g, yc, wts["b_ln"], wts["avg_b"], wts["w_o"])
        xp = _ffn(xp, wts["norm_g"][2], wts["w_ffn_in"][1], wts["w_ffn_out"][1], final_g=fin)
        outs["kp"].append(zp[4].reshape(n_p, H_C, GROUP, seq_len).transpose(0, 3, 1, 2))
        outs["vp"].append(zp[5].reshape(n_p, H_C, GROUP, seq_len).transpose(0, 3, 1, 2))
        outs["lfp"].append(zp[3][:, :H_C].reshape(n_p, seq_len, H_C))
        outs["wkvp"].append(wkvp)
        outs["shp"].append(zp[1].reshape(n_p, seq_len, B_PROJ)[:, -1])

        xs = _ffn(xs, wts["norm_g"][0], wts["w_ffn_in"][0], wts["w_ffn_out"][0])
        zs = _inproj(xs, wts["norm_g"][1], wts["w_in"], wts["wkv_t"], wts["c_fb"])
        ya, ob, bonus, g, yc, wkvs, va = _mix_sample(wts, l, zs, state_shift[l], state_wkv[l],
                                                     ck, cv, clf, page_table)
        xs = _outproj(xs, ya, ob, bonus, g, yc, wts["b_ln"], wts["avg_b"], wts["w_o"])
        xs = _ffn(xs, wts["norm_g"][2], wts["w_ffn_in"][1], wts["w_ffn_out"][1], final_g=fin)
        outs["ks"].append(zs[4].reshape(n_s, 1, H_C, GROUP))
        outs["vs"].append(zs[5].reshape(n_s, 1, H_C, GROUP))
        outs["lfs"].append(zs[3][:, :H_C].reshape(n_s, 1, H_C))
        outs["wkvs"].append(wkvs)
        outs["shs"].append(zs[1])
        outs["va"].append(va.reshape(n_s, 1, W_A))
    st = lambda name: jnp.stack(outs[name])
    return (xp.reshape(n_p, seq_len, D_MODEL), xs.reshape(n_s, 1, D_MODEL),
            st("kp"), st("vp"), st("lfp"), st("wkvp"), st("shp"),
            st("ks"), st("vs"), st("lfs"), st("wkvs"), st("shs"), st("va"))
```

```python
import functools

import jax
import jax.numpy as jnp
from jax import lax
from jax.experimental import pallas as pl
from jax.experimental.pallas import tpu as pltpu

F32 = jnp.float32
BF16 = jnp.bfloat16

LANES = 128
D_MODEL = 1024
D_FF = 2816
GROUP = 64
W_A = 256
W_B = 384
W_C = 384
H_B = W_B // GROUP
H_C = W_C // GROUP
N_PAIR = W_B // LANES
R_DECAY = 64
R_AAA = 64
R_GATE = 128
B_PROJ = 3 * W_B + R_DECAY + R_AAA + R_GATE
A_PROJ = 2 * W_A
C_PROJ = 3 * W_C + H_C
IN_PROJ = A_PROJ + B_PROJ + C_PROJ
IN_PROJ_PAD = A_PROJ + B_PROJ + 3 * W_C + LANES
CHUNK_A = 128
CHUNK_B = 64
NORM_EPS = 1e-6
GN_EPS = 64e-5
NEG_BIG = -1e30
LOG2E = 1.4426950408889634
Q_SCALE = LOG2E * GROUP ** -0.5
VMEM_LIMIT = 56 << 20


def _params(n_axes, vmem=VMEM_LIMIT):
    return pltpu.CompilerParams(dimension_semantics=("arbitrary",) * n_axes,
                                vmem_limit_bytes=vmem)


def _sigmoid(x):
    return 1.0 / (1.0 + jnp.exp(-x))


def _softplus(x):
    return jnp.maximum(x, 0.0) + jnp.log(1.0 + jnp.exp(-jnp.abs(x)))


def _gelu_tanh(x):
    return 0.5 * x * (1.0 + jnp.tanh(0.7978845608028654 * (x + 0.044715 * (x * x * x))))


def _rms(x, g):
    return x * lax.rsqrt(jnp.mean(x * x, axis=-1, keepdims=True) + NORM_EPS) * g


def _dot(a, b):
    return jnp.dot(a, b, preferred_element_type=F32)


def _dot_nt(a, b):
    return lax.dot_general(a, b, (((1,), (1,)), ((), ())), preferred_element_type=F32)


def _dot_split(a, b_bf):
    hi = a.astype(BF16)
    lo = (a - hi.astype(F32)).astype(BF16)
    return _dot(hi, b_bf) + _dot(lo, b_bf)


def _bf16_pieces(x):
    pieces = []
    for _ in range(3):
        piece = x.astype(BF16)
        pieces.append(piece)
        x = x - piece.astype(F32)
    return pieces


def _dot_ones_left(ones, x):
    n = x.shape[1]
    y = _dot(ones.astype(BF16), jnp.concatenate(_bf16_pieces(x), axis=1))
    return y[:, :n] + y[:, n:2 * n] + y[:, 2 * n:]


def _dot_ones_right(x, ones):
    m = x.shape[0]
    y = _dot(jnp.concatenate(_bf16_pieces(x), axis=0), ones.astype(BF16))
    return y[:m] + y[m:2 * m] + y[2 * m:]


def _iota(shape, dim):
    return lax.broadcasted_iota(jnp.int32, shape, dim)


def _ffn_kernel(*refs, n_ff, final):
    if final:
        x_ref, g_ref, wg_ref, wu_ref, wo_ref, fg_ref, o_ref, h_scr, acc_scr = refs
    else:
        x_ref, g_ref, wg_ref, wu_ref, wo_ref, o_ref, h_scr, acc_scr = refs
    j = pl.program_id(1)

    @pl.when(j == 0)
    def _():
        h_scr[...] = _rms(x_ref[...], g_ref[...]).astype(BF16)
        acc_scr[...] = jnp.zeros_like(acc_scr)

    h = h_scr[...]
    gate = _dot(h, wg_ref[...])
    up = _dot(h, wu_ref[...])
    act = (gate * _sigmoid(gate) * up).astype(BF16)
    acc_scr[...] += _dot(act, wo_ref[...])

    @pl.when(j == n_ff - 1)
    def _():
        y = x_ref[...] + 0.5 * acc_scr[...]
        if final:
            y = _rms(y, fg_ref[...])
        o_ref[...] = y


def _ffn(x, g, w_in, w_out, final_g=None, tm=512, n_ff=2):
    m = x.shape[0]
    tm = min(tm, m)
    tf = D_FF // n_ff
    final = final_g is not None
    in_specs = [
        pl.BlockSpec((tm, D_MODEL), lambda i, j: (i, 0)),
        pl.BlockSpec((1, D_MODEL), lambda i, j: (0, 0)),
        pl.BlockSpec((D_MODEL, tf), lambda i, j: (0, j)),
        pl.BlockSpec((D_MODEL, tf), lambda i, j: (0, j + n_ff)),
        pl.BlockSpec((tf, D_MODEL), lambda i, j: (j, 0)),
    ]
    args = [x, g.reshape(1, D_MODEL), w_in, w_in, w_out]
    if final:
        in_specs.append(pl.BlockSpec((1, D_MODEL), lambda i, j: (0, 0)))
        args.append(final_g.reshape(1, D_MODEL))
    return pl.pallas_call(
        functools.partial(_ffn_kernel, n_ff=n_ff, final=final),
        grid=(m // tm, n_ff),
        in_specs=in_specs,
        out_specs=pl.BlockSpec((tm, D_MODEL), lambda i, j: (i, 0)),
        out_shape=jax.ShapeDtypeStruct((m, D_MODEL), F32),
        scratch_shapes=[pltpu.VMEM((tm, D_MODEL), BF16), pltpu.VMEM((tm, D_MODEL), F32)],
        compiler_params=_params(2),
        name="ffn",
    )(*args)


def _inproj_kernel(x_ref, g_ref, w_ref, wkvt_ref, fb_ref, za_ref, zb_ref, q_ref, lf_ref, *kv_refs,
                   channel_major):
    h = _rms(x_ref[...], g_ref[...]).astype(BF16)
    z = _dot(h, w_ref[...])
    o = A_PROJ
    za_ref[...] = z[:, :o]
    zb_ref[...] = z[:, o:o + B_PROJ]
    o += B_PROJ
    q_ref[...] = (z[:, o:o + W_C] * Q_SCALE).astype(BF16)
    k = z[:, o + W_C:o + 2 * W_C]
    v = z[:, o + 2 * W_C:o + 3 * W_C]
    lf_ref[...] = -_softplus(-(z[:, o + 3 * W_C:] + fb_ref[...]))
    if channel_major:
        kt_ref, vt_ref, kb_ref, vtb_ref = kv_refs
        kt_ref[...] = _dot_nt(wkvt_ref[0:W_C, :], h)
        vt = _dot_nt(wkvt_ref[W_C:, :], h)
        vt_ref[...] = vt
        vtb_ref[...] = vt.astype(BF16)
        kb_ref[...] = k.astype(BF16)
    else:
        k_ref, v_ref = kv_refs
        k_ref[...] = k
        v_ref[...] = v


def _inproj(x, g, w_pad, wkv_t, fb_pad, seq_len=None, tm=512):
    m = x.shape[0]
    tm = min(tm, m)
    row = lambda i: (i, 0)
    fix = lambda i: (0, 0)
    widths = (A_PROJ, B_PROJ, W_C, LANES)
    dtypes = (F32, F32, BF16, F32)
    out_specs = [pl.BlockSpec((tm, w), row) for w in widths]
    out_shape = [jax.ShapeDtypeStruct((m, w), d) for w, d in zip(widths, dtypes)]
    if seq_len is None:
        out_specs += [pl.BlockSpec((tm, W_C), row)] * 2
        out_shape += [jax.ShapeDtypeStruct((m, W_C), F32)] * 2
    else:
        bps = seq_len // tm
        seq_blk = pl.BlockSpec((None, W_C, tm), lambda i: (i // bps, 0, i % bps))
        out_specs += [seq_blk, seq_blk, pl.BlockSpec((tm, W_C), row), pl.BlockSpec((W_C, tm), lambda i: (0, i))]
        out_shape += [jax.ShapeDtypeStruct((m // seq_len, W_C, seq_len), F32)] * 2
        out_shape += [jax.ShapeDtypeStruct((m, W_C), BF16), jax.ShapeDtypeStruct((W_C, m), BF16)]
    return pl.pallas_call(
        functools.partial(_inproj_kernel, channel_major=seq_len is not None),
        grid=(m // tm,),
        in_specs=[pl.BlockSpec((tm, D_MODEL), row), pl.BlockSpec((1, D_MODEL), fix),
                  pl.BlockSpec((D_MODEL, IN_PROJ_PAD), fix), pl.BlockSpec((2 * W_C, D_MODEL), fix),
                  pl.BlockSpec((1, LANES), fix)],
        out_specs=out_specs,
        out_shape=out_shape,
        compiler_params=_params(1),
        name="inproj",
    )(x, g.reshape(1, D_MODEL), w_pad, wkv_t, fb_pad)


def _gmlp_kernel(za_ref, gain_ref, ws_ref, bias_ref, avg_ref, ya_ref, va_ref, *, n_chunks):
    z = _gelu_tanh(za_ref[...])
    u = z[:, :W_A]
    v = z[:, W_A:]
    ms = _dot((v * v).astype(BF16), avg_ref[...])
    vn = v * lax.rsqrt(ms + NORM_EPS) * gain_ref[...]
    va_ref[...] = vn
    causal = _iota((CHUNK_A, CHUNK_A), 0) >= _iota((CHUNK_A, CHUNK_A), 1)
    lane_group = _iota((CHUNK_A, W_A), 1) // GROUP
    wm = [jnp.where(causal, ws_ref[g], 0.0).astype(BF16) for g in range(W_A // GROUP)]
    for c in range(n_chunks):
        rows = slice(c * CHUNK_A, (c + 1) * CHUNK_A)
        vc = vn[rows]
        s = bias_ref[...]
        for g in range(W_A // GROUP):
            s = s + _dot(wm[g], jnp.where(lane_group == g, vc, 0.0).astype(BF16))
        ya_ref[rows, :] = (u[rows] * s).astype(BF16)


def _gmlp(za, gain, ws, bias_full, avg_a, tm=512):
    m = za.shape[0]
    tm = min(tm, m)
    row = lambda i: (i, 0)
    fix = lambda i: (0, 0)
    return pl.pallas_call(
        functools.partial(_gmlp_kernel, n_chunks=tm // CHUNK_A),
        grid=(m // tm,),
        in_specs=[pl.BlockSpec((tm, A_PROJ), row), pl.BlockSpec((1, W_A), fix),
                  pl.BlockSpec((W_A // GROUP, CHUNK_A, CHUNK_A), lambda i: (0, 0, 0)),
                  pl.BlockSpec((CHUNK_A, W_A), fix), pl.BlockSpec((W_A, W_A), fix)],
        out_specs=[pl.BlockSpec((tm, W_A), row), pl.BlockSpec((tm, W_A), row)],
        out_shape=[jax.ShapeDtypeStruct((m, W_A), BF16), jax.ShapeDtypeStruct((m, W_A), F32)],
        compiler_params=_params(1),
        name="gmlp",
    )(za, gain.reshape(1, W_A), ws, bias_full, avg_a)


def _rwkv_prep_math(zb, prev, vec_ref, wb_ref, ab_ref, gb_ref, ones_ref, outs):
    r_ref, lw_ref, k_ref, v_ref, kk_ref, beta_ref, g_ref, bonus_ref = outs
    mu = vec_ref[0:1, :]
    zs = zb + (prev - zb) * mu
    r = zs[:, :W_B]
    k = zs[:, W_B:2 * W_B]
    v = zs[:, 2 * W_B:3 * W_B]
    lora = zs[:, 3 * W_B:3 * W_B + LANES]
    gl = zs[:, 3 * W_B + LANES:]
    w0 = vec_ref[1:2, :W_B]
    a0 = vec_ref[2:3, :W_B]
    kkw = vec_ref[3:4, :W_B]
    kaw = vec_ref[4:5, :W_B]
    rkw = vec_ref[5:6, :W_B]
    w = -_softplus(-(w0 + _dot(jnp.tanh(lora).astype(BF16), wb_ref[...]))) - 0.5
    a = _sigmoid(a0 + _dot(lora.astype(BF16), ab_ref[...]))
    g = _dot(_sigmoid(gl).astype(BF16), gb_ref[...])
    kk = k * kkw
    ss = _dot_split(kk * kk, ones_ref[...])
    kk = kk / jnp.maximum(jnp.sqrt(ss), 1e-12)
    k2 = k * (1.0 + (a - 1.0) * kaw)
    r_ref[...] = r
    lw_ref[...] = -jnp.exp(w)
    k_ref[...] = k2
    v_ref[...] = v
    kk_ref[...] = kk
    beta_ref[...] = kk * a
    g_ref[...] = g
    bonus_ref[...] = _dot_split(r * k2 * rkw, ones_ref[...]) * v


def _rwkv_prep_seq_kernel(zb_ref, pb_ref, vec_ref, wb_ref, ab_ref, gb_ref, ones_ref, *outs,
                          blocks_per_seq):
    i = pl.program_id(0)
    zb = zb_ref[...]
    tm = zb.shape[0]
    first = (i % blocks_per_seq) == 0
    last_prev = jnp.where(first, 0.0, pb_ref[7:8, :])
    prev = jnp.where(_iota((tm, 1), 0) == 0, last_prev, pltpu.roll(zb, shift=1, axis=0))
    _rwkv_prep_math(zb, prev, vec_ref, wb_ref, ab_ref, gb_ref, ones_ref, outs)


def _rwkv_prep_tok_kernel(zb_ref, prev_ref, vec_ref, wb_ref, ab_ref, gb_ref, ones_ref, *outs):
    _rwkv_prep_math(zb_ref[...], prev_ref[...], vec_ref, wb_ref, ab_ref, gb_ref, ones_ref, outs)


def _rwkv_prep(zb, prev, seq_len, wts, tm=512):
    m = zb.shape[0]
    tm = min(tm, m)
    row = lambda i: (i, 0)
    fix = lambda i: (0, 0)
    w_specs = [pl.BlockSpec((8, B_PROJ), fix), pl.BlockSpec((LANES, W_B), fix),
               pl.BlockSpec((LANES, W_B), fix), pl.BlockSpec((R_GATE, W_B), fix),
               pl.BlockSpec((W_B, W_B), fix)]
    w_args = [wts["b_vec"], wts["b_wB"], wts["b_aB"], wts["b_gB"], wts["ones_b"]]
    if prev is None:
        kern = functools.partial(_rwkv_prep_seq_kernel, blocks_per_seq=seq_len // tm)
        sub = tm // 8
        in_specs = [pl.BlockSpec((tm, B_PROJ), row),
                    pl.BlockSpec((8, B_PROJ), lambda i: (jnp.maximum(i * sub - 1, 0), 0))]
        args = [zb, zb]
    else:
        kern = _rwkv_prep_tok_kernel
        in_specs = [pl.BlockSpec((tm, B_PROJ), row), pl.BlockSpec((tm, B_PROJ), row)]
        args = [zb, prev]
    return pl.pallas_call(
        kern,
        grid=(m // tm,),
        in_specs=in_specs + w_specs,
        out_specs=[pl.BlockSpec((tm, W_B), row)] * 8,
        out_shape=[jax.ShapeDtypeStruct((m, W_B), F32)] * 8,
        compiler_params=_params(1),
        name="rwkv_prep",
    )(*args, *w_args)


def _stack(x, low):
    return jnp.concatenate([jnp.where(low, x, 0.0), jnp.where(low, 0.0, x)], axis=0)


CHUNKS_PER_STEP = 4

def _rwkv_chunk_kernel(r_ref, lw_ref, k_ref, v_ref, kk_ref, beta_ref,
                       x1_ref, x2_ref, ub_ref, op_ref, sp_ref, gam_ref):
    c = CHUNK_B
    n2 = 2 * c
    rows = CHUNKS_PER_STEP * c
    ri = _iota((rows, rows), 0)
    ci = _iota((rows, rows), 1)
    tri = jnp.where(ri >= ci, 1.0, 0.0) * jnp.where((ri // c) == (ci // c), 1.0, 0.0)
    lw = lw_ref[...]
    cum = _dot_ones_left(tri, lw)
    lasts = [cum[(j + 1) * c - 1:(j + 1) * c, :] for j in range(CHUNKS_PER_STEP)]
    for j in range(CHUNKS_PER_STEP):
        gam_ref[j] = jnp.exp(lasts[j])
    cum_last = jnp.concatenate([jnp.broadcast_to(l, (c, W_B)) for l in lasts], axis=0)
    e_pos = jnp.exp(cum)
    e_neg = jnp.exp(-cum)
    e_tail = jnp.exp(cum_last - cum)
    r_t = r_ref[...] * e_pos
    kap_t = kk_ref[...] * jnp.exp(cum - lw)
    beta_h = beta_ref[...] * e_neg
    k_h = k_ref[...] * e_neg
    beta_c = beta_ref[...] * e_tail
    k_c = k_ref[...] * e_tail
    v = v_ref[...]

    low = _iota((c, LANES), 1) < GROUP
    rr = _iota((n2, n2), 0) & (c - 1)
    cc = _iota((n2, n2), 1) & (c - 1)
    strict = rr > cc
    incl = rr >= cc
    streams = [(j, p) for j in range(CHUNKS_PER_STEP) for p in range(N_PAIR)]

    def tile(x, j, p):
        return _stack(x[j * c:(j + 1) * c, p * LANES:(p + 1) * LANES], low)

    kap_s = [tile(kap_t, j, p) for j, p in streams]
    r_s = [tile(r_t, j, p) for j, p in streams]
    v_s = [tile(v, j, p).astype(BF16) for j, p in streams]
    gram = [_dot_nt(jnp.concatenate([kap_s[i], r_s[i]], axis=0).astype(BF16),
                    jnp.concatenate([tile(beta_h, j, p), tile(k_h, j, p)], axis=0).astype(BF16))
            for i, (j, p) in enumerate(streams)]
    n_bf = [jnp.where(strict, g[:n2, :n2], 0.0).astype(BF16) for g in gram]
    av = [_dot(jnp.where(strict, g[:n2, n2:], 0.0).astype(BF16), vs) for g, vs in zip(gram, v_s)]
    for i, (j, p) in enumerate(streams):
        op_ref[j, p] = _dot(jnp.where(incl, gram[i][n2:, n2:], 0.0).astype(BF16), v_s[i])
        sp_ref[j, p] = _dot(tile(k_c, j, p).T.astype(BF16), v_s[i])
        x2_ref[j, p] = jnp.concatenate([jnp.where(incl, gram[i][n2:, :n2], 0.0),
                                        tile(beta_c, j, p).T], axis=0).astype(BF16)
    x = [jnp.concatenate([ks, -a], axis=1) for ks, a in zip(kap_s, av)]
    x = [xi - _dot(nb, xi.astype(BF16)) for xi, nb in zip(x, n_bf)]
    pw = n_bf
    for _ in range(5):
        pw = [_dot(q, q).astype(BF16) for q in pw]
        x = [xi + _dot(q, xi.astype(BF16)) for xi, q in zip(x, pw)]
    for i, (j, p) in enumerate(streams):
        x1_ref[j, p] = jnp.concatenate([x[i][:, :LANES], r_s[i]], axis=0).astype(BF16)
        ub_ref[j, p] = x[i][:, LANES:]


def _rwkv_chunks(r, lw, k2, v, kk, beta):
    m = r.shape[0]
    nc = m // CHUNK_B
    cb = CHUNKS_PER_STEP
    row = lambda i: (i, 0)
    blk = lambda i: (i, 0, 0, 0)
    t = 2 * CHUNK_B
    return pl.pallas_call(
        _rwkv_chunk_kernel,
        grid=(nc // cb,),
        in_specs=[pl.BlockSpec((cb * CHUNK_B, W_B), row)] * 6,
        out_specs=[pl.BlockSpec((cb, N_PAIR, 2 * t, LANES), blk),
                   pl.BlockSpec((cb, N_PAIR, 2 * t, LANES), blk),
                   pl.BlockSpec((cb, N_PAIR, t, LANES), blk),
                   pl.BlockSpec((cb, N_PAIR, t, LANES), blk),
                   pl.BlockSpec((cb, N_PAIR, t, LANES), blk),
                   pl.BlockSpec((cb, 1, W_B), lambda i: (i, 0, 0))],
        out_shape=[jax.ShapeDtypeStruct((nc, N_PAIR, 2 * t, LANES), BF16),
                   jax.ShapeDtypeStruct((nc, N_PAIR, 2 * t, LANES), BF16),
                   jax.ShapeDtypeStruct((nc, N_PAIR, t, LANES), F32),
                   jax.ShapeDtypeStruct((nc, N_PAIR, t, LANES), F32),
                   jax.ShapeDtypeStruct((nc, N_PAIR, t, LANES), F32),
                   jax.ShapeDtypeStruct((nc, 1, W_B), F32)],
        compiler_params=_params(1),
        name="rwkv_chunks",
    )(r, lw, k2, v, kk, beta)


def _rwkv_scan_kernel(x1_ref, x2_ref, ub_ref, op_ref, sp_ref, gam_ref, o_ref, st_ref, st_scr,
                      *, n_seq, n_chunks):
    ci = pl.program_id(0)
    t = 2 * CHUNK_B

    @pl.when(ci == 0)
    def _():
        st_scr[...] = jnp.zeros_like(st_scr)

    eye = _iota((t, t), 0) == _iota((t, t), 1)
    streams = [(b, p) for b in range(n_seq) for p in range(N_PAIR)]
    st = [st_scr[b * N_PAIR + p] for b, p in streams]
    y = [_dot(x1_ref[b, 0, p], s.astype(BF16)) for (b, p), s in zip(streams, st)]
    u = [ub_ref[b, 0, p] - yi[:t] for (b, p), yi in zip(streams, y)]
    z = [_dot(x2_ref[b, 0, p], ui.astype(BF16)) for (b, p), ui in zip(streams, u)]
    for i, (b, p) in enumerate(streams):
        o_s = op_ref[b, 0, p] + y[i][t:] + z[i][:t]
        gam_row = gam_ref[b, 0, :, p * LANES:(p + 1) * LANES]
        gam_col = jnp.sum(jnp.where(eye, gam_row, 0.0), axis=1, keepdims=True)
        st_scr[b * N_PAIR + p] = gam_col * st[i] + z[i][t:] + sp_ref[b, 0, p]
        o_ref[b, :, p * LANES:(p + 1) * LANES] = o_s[:CHUNK_B] + o_s[CHUNK_B:]

    @pl.when(ci == n_chunks - 1)
    def _():
        st_ref[...] = st_scr[...]


def _rwkv_scan(x1, x2, ub, op, sp, gam, n_seq, seq_len):
    nc = seq_len // CHUNK_B
    t = 2 * CHUNK_B
    r5 = lambda a: a.reshape((n_seq, nc) + a.shape[1:])
    blk5 = lambda rows: pl.BlockSpec((n_seq, 1, N_PAIR, rows, LANES), lambda c: (0, c, 0, 0, 0))
    return pl.pallas_call(
        functools.partial(_rwkv_scan_kernel, n_seq=n_seq, n_chunks=nc),
        grid=(nc,),
        in_specs=[blk5(2 * t), blk5(2 * t), blk5(t), blk5(t), blk5(t),
                  pl.BlockSpec((n_seq, 1, 1, W_B), lambda c: (0, c, 0, 0))],
        out_specs=[pl.BlockSpec((n_seq, CHUNK_B, W_B), lambda c: (0, c, 0)),
                   pl.BlockSpec((n_seq * N_PAIR, t, LANES), lambda c: (0, 0, 0))],
        out_shape=[jax.ShapeDtypeStruct((n_seq, seq_len, W_B), F32),
                   jax.ShapeDtypeStruct((n_seq * N_PAIR, t, LANES), F32)],
        scratch_shapes=[pltpu.VMEM((n_seq * N_PAIR, t, LANES), F32)],
        compiler_params=_params(1),
        name="rwkv_scan",
    )(r5(x1), r5(x2), r5(ub), r5(op), r5(sp), r5(gam))


def _rwkv_step_kernel(s_ref, r_ref, lw_ref, k_ref, kk_ref, beta_ref, vcol_ref, o_ref, so_ref):
    for h in range(H_B):
        s = s_ref[0, h]
        sk = jnp.sum(s * kk_ref[0, h], axis=1, keepdims=True)
        s_new = s * jnp.exp(lw_ref[0, h]) - sk * beta_ref[0, h] + vcol_ref[0, h] * k_ref[0, h]
        so_ref[0, h] = s_new
        o_ref[0, h] = jnp.sum(s_new * r_ref[0, h], axis=1, keepdims=True)


def _rwkv_step(state, r, lw, k2, kk, beta, v):
    n = state.shape[0]
    rowv = lambda a: a.reshape(n, H_B, 1, GROUP)
    idx = lambda b: (b, 0, 0, 0)
    row_spec = pl.BlockSpec((1, H_B, 1, GROUP), idx)
    col_spec = pl.BlockSpec((1, H_B, GROUP, 1), idx)
    mat_spec = pl.BlockSpec((1, H_B, GROUP, GROUP), idx)
    o, s_new = pl.pallas_call(
        _rwkv_step_kernel,
        grid=(n,),
        in_specs=[mat_spec] + [row_spec] * 5 + [col_spec],
        out_specs=[col_spec, mat_spec],
        out_shape=[jax.ShapeDtypeStruct((n, H_B, GROUP, 1), F32),
                   jax.ShapeDtypeStruct((n, H_B, GROUP, GROUP), F32)],
        compiler_params=_params(1),
        name="rwkv_step",
    )(state, rowv(r), rowv(lw), rowv(k2), rowv(kk), rowv(beta), v.reshape(n, H_B, GROUP, 1))
    return o.reshape(n, W_B), s_new


BIAS_PIECES = 3


def _cumsum_kernel(lf_ref, place_ref, b_ref, carry):
    @pl.when(pl.program_id(1) == 0)
    def _():
        carry[...] = jnp.zeros_like(carry)

    tb = lf_ref.shape[0]
    tri = (_iota((tb, tb), 0) >= _iota((tb, tb), 1)).astype(F32)
    c = _dot_ones_left(tri, lf_ref[...]) + carry[...]
    carry[...] = c[tb - 1:tb, :]
    pieces = _bf16_pieces(-LOG2E * c)
    b_ref[...] = _dot(jnp.concatenate(pieces, axis=1), place_ref[...]).astype(BF16)


def _bias_placement():
    rows = jnp.arange(BIAS_PIECES * LANES)
    piece, head = rows // LANES, rows % LANES
    col = LANES * (head // 2) + jnp.where(head % 2 == 0, GROUP, 0) + piece
    hit = (col[:, None] == jnp.arange(W_C)[None, :]) & (head < H_C)[:, None]
    return hit.astype(BF16)


def _cumsum(lf, n_seq, seq_len, tb=512):
    nb = seq_len // tb
    return pl.pallas_call(
        _cumsum_kernel,
        grid=(n_seq, nb),
        in_specs=[pl.BlockSpec((tb, LANES), lambda b, j: (b * nb + j, 0)),
                  pl.BlockSpec((BIAS_PIECES * LANES, W_C), lambda b, j: (0, 0))],
        out_specs=pl.BlockSpec((tb, W_C), lambda b, j: (b * nb + j, 0)),
        out_shape=jax.ShapeDtypeStruct((n_seq * seq_len, W_C), BF16),
        scratch_shapes=[pltpu.VMEM((1, LANES), F32)],
        compiler_params=_params(2),
        name="logf_cumsum",
    )(lf, _bias_placement())


HEAD_ROWS = 16
DECODE_PAGES = 16


def _prompt_init(q_ref, qa_scr, m_scr, acc_scr):
    tq = q_ref.shape[0]
    lane = _iota((tq, LANES), 1)
    ones_hi = jnp.where(lane < GROUP + BIAS_PIECES, 1.0, 0.0).astype(BF16)
    ones_lo = jnp.where(lane < BIAS_PIECES, 1.0, 0.0).astype(BF16)
    for p in range(N_PAIR):
        q = q_ref[:, p * LANES:(p + 1) * LANES]
        qa_scr[2 * p] = jnp.where(lane < GROUP, q, ones_hi)
        qa_scr[2 * p + 1] = jnp.where(lane < GROUP, ones_lo, q)
    m_scr[...] = jnp.full_like(m_scr, NEG_BIG)
    acc_scr[...] = jnp.zeros_like(acc_scr)


def _prompt_step(k_ref, vt_ref, b_ref, qa_scr, m_scr, acc_scr, diagonal):
    tk = k_ref.shape[0]
    tq = qa_scr.shape[1]
    low_k = _iota((tk, LANES), 1) < GROUP
    low_v = _iota((LANES, tk), 0) < GROUP
    ka, va = [], []
    for p in range(N_PAIR):
        k = k_ref[:, p * LANES:(p + 1) * LANES]
        bias = b_ref[:, p * LANES:(p + 1) * LANES]
        vt = vt_ref[p * LANES:(p + 1) * LANES, :]
        one = jnp.ones_like(vt)
        ka += [jnp.where(low_k, k, bias), jnp.where(low_k, bias, k)]
        va += [jnp.where(low_v, vt, one), jnp.where(low_v, one, vt)]
    scores = [_dot_nt(ka[h], qa_scr[h]) for h in range(H_C)]
    for h in range(H_C):
        s = scores[h]
        if diagonal:
            s = jnp.where(_iota((tk, tq), 0) <= _iota((tk, tq), 1), s, NEG_BIG)
        m_prev = m_scr[h]
        m_new = jnp.maximum(m_prev, jnp.max(s, axis=0, keepdims=True))
        alpha = jnp.exp2(m_prev - m_new)
        pr = jnp.exp2(s - m_new).astype(BF16)
        acc_scr[h] = alpha * acc_scr[h] + _dot(va[h], pr)
        m_scr[h] = m_new


def _prompt_finish(o_ref, acc_scr):
    tq = o_ref.shape[0]
    low_row = _iota((LANES, tq), 0) < GROUP
    for p in range(N_PAIR):
        a0 = acc_scr[2 * p]
        a1 = acc_scr[2 * p + 1]
        out = jnp.where(low_row, a0 / a0[LANES - 1:LANES, :], a1 / a1[0:1, :])
        o_ref[:, p * LANES:(p + 1) * LANES] = out.T.astype(BF16)


def _own_head_mask():
    return (_iota((HEAD_ROWS, W_C), 1) // GROUP) == _iota((HEAD_ROWS, W_C), 0)


def _decode_init(q_ref, qf_scr, qb_scr, m_scr, l_scr, acc_scr, carry):
    qrows = jnp.where(_own_head_mask(), q_ref[0].astype(F32), 0.0)
    qf_scr[...] = qrows
    qb_scr[...] = qrows.astype(BF16)
    m_scr[...] = jnp.full_like(m_scr, NEG_BIG)
    l_scr[...] = jnp.zeros_like(l_scr)
    acc_scr[...] = jnp.zeros_like(acc_scr)
    carry[...] = jnp.zeros_like(carry)


def _decode_step(k_refs, v_refs, lf_refs, qb_scr, m_scr, l_scr, acc_scr, carry):
    pages = len(k_refs)
    page = k_refs[0].shape[1]
    upto = (_iota((page, page), 0) <= _iota((page, page), 1)).astype(F32)
    lf_all = jnp.concatenate([lf_refs[u][...] for u in range(pages)], axis=0)
    c_all = _dot_ones_right(lf_all, upto)
    totals = [c_all[8 * u:8 * (u + 1), page - 1:page] for u in range(pages)]
    run = carry[...]
    cts = []
    for u in range(pages):
        cts.append(c_all[8 * u:8 * (u + 1)] + run)
        run = run + totals[u]
    carry[...] = run
    ct = jnp.concatenate(cts, axis=1)
    ct = jnp.concatenate([ct, jnp.zeros_like(ct)], axis=0)
    kcat = jnp.concatenate([k_refs[u][...].astype(BF16) for u in range(pages)], axis=1)
    vcat = jnp.concatenate([v_refs[u][...].astype(BF16) for u in range(pages)], axis=1)
    s = _dot(qb_scr[...], kcat) - LOG2E * ct
    m_prev = m_scr[...]
    m_new = jnp.maximum(m_prev, jnp.max(s, axis=1, keepdims=True))
    alpha = jnp.exp2(m_prev - m_new)
    pr = jnp.exp2(s - m_new)
    l_scr[...] = alpha * l_scr[...] + jnp.sum(pr, axis=1, keepdims=True)
    acc_scr[...] = alpha * acc_scr[...] + _dot_nt(pr.astype(BF16), vcat)
    m_scr[...] = m_new


def _decode_finish(o_ref, kn_ref, vn_ref, lfn_ref, qf_scr, m_scr, l_scr, acc_scr, carry):
    c_past = jnp.concatenate([carry[...], jnp.zeros_like(carry)], axis=0)
    s_new = (jnp.sum(qf_scr[...] * kn_ref[0], axis=1, keepdims=True)
             - LOG2E * (c_past + lfn_ref[0]))
    m_prev = m_scr[...]
    m_new = jnp.maximum(m_prev, s_new)
    alpha = jnp.exp2(m_prev - m_new)
    pn = jnp.exp2(s_new - m_new)
    l_fin = alpha * l_scr[...] + pn
    acc = alpha * acc_scr[...] + pn * vn_ref[0]
    o_ref[0] = jnp.sum(jnp.where(_own_head_mask(), acc / l_fin, 0.0), axis=0, keepdims=True)


def _fox_kernel(qrow_ref, krow_ref, dseq_ref, pt_ref, q_ref, k_ref, vt_ref, b_ref, qs_ref, kn_ref, vn_ref,
                lfn_ref, *refs, nq, prompt_steps, decode_steps, steps_per_seq):
    pages = DECODE_PAGES
    k_refs = refs[:pages]
    v_refs = refs[pages:2 * pages]
    lf_refs = refs[2 * pages:3 * pages]
    (o_ref, os_ref, qa_scr, m_scr, acc_scr,
     qf_scr, qb_scr, dm_scr, dl_scr, dacc_scr, carry) = refs[3 * pages:]
    g = pl.program_id(0)
    i = qrow_ref[g] % nq
    j = krow_ref[g] % nq
    in_prompt = g < prompt_steps
    in_decode = g < decode_steps
    dstep = jnp.minimum(g, decode_steps - 1) % steps_per_seq
    dstate = (qb_scr, dm_scr, dl_scr, dacc_scr, carry)

    @pl.when(jnp.logical_and(in_decode, dstep == 0))
    def _():
        _decode_init(qs_ref, qf_scr, *dstate)

    @pl.when(jnp.logical_and(in_prompt, j == 0))
    def _():
        _prompt_init(q_ref, qa_scr, m_scr, acc_scr)

    @pl.when(jnp.logical_and(in_prompt, j < i))
    def _():
        _prompt_step(k_ref, vt_ref, b_ref, qa_scr, m_scr, acc_scr, False)
        _decode_step(k_refs, v_refs, lf_refs, *dstate)

    @pl.when(jnp.logical_and(in_prompt, j == i))
    def _():
        _prompt_step(k_ref, vt_ref, b_ref, qa_scr, m_scr, acc_scr, True)
        _decode_step(k_refs, v_refs, lf_refs, *dstate)
        _prompt_finish(o_ref, acc_scr)

    @pl.when(jnp.logical_not(in_prompt))
    def _():
        _decode_step(k_refs, v_refs, lf_refs, *dstate)

    @pl.when(jnp.logical_and(in_decode, dstep == steps_per_seq - 1))
    def _():
        _decode_finish(os_ref, kn_ref, vn_ref, lfn_ref, qf_scr, dm_scr, dl_scr, dacc_scr, carry)


def _fox(layer, q, k, vt, bias, n_seq, seq_len, q_s, k_new, v_new, lf_new, cache_k, cache_v, cache_lf,
         page_table, tq=512):
    nq = seq_len // tq
    pairs = [(i, j) for i in range(nq) for j in range(i + 1)]
    n_s, n_pages = page_table.shape
    page = cache_k.shape[3]
    pages = DECODE_PAGES
    steps_per_seq = n_pages // pages
    prompt_steps = n_seq * len(pairs)
    decode_steps = n_s * steps_per_seq
    n_steps = max(prompt_steps, decode_steps)
    lfn = jnp.pad(lf_new[:, :H_C], ((0, 0), (0, HEAD_ROWS - H_C))).reshape(n_s, HEAD_ROWS, 1)
    p_of = [min(g, prompt_steps - 1) for g in range(n_steps)]
    d_of = [min(g, decode_steps - 1) for g in range(n_steps)]
    qrow = jnp.asarray([(p // len(pairs)) * nq + pairs[p % len(pairs)][0] for p in p_of], jnp.int32)
    krow = jnp.asarray([(p // len(pairs)) * nq + pairs[p % len(pairs)][1] for p in p_of], jnp.int32)
    dseq = jnp.asarray([d // steps_per_seq for d in d_of], jnp.int32)
    step_pages = page_table.reshape(decode_steps, pages)[jnp.asarray(d_of, jnp.int32)]

    q_spec = pl.BlockSpec((tq, W_C), lambda g, qrow, krow, dseq, pt: (qrow[g], 0))
    k_spec = pl.BlockSpec((tq, W_C), lambda g, qrow, krow, dseq, pt: (krow[g], 0))
    vt_spec = pl.BlockSpec((W_C, tq), lambda g, qrow, krow, dseq, pt: (0, krow[g]))
    seq3 = lambda rows, w: pl.BlockSpec((1, rows, w), lambda g, qrow, krow, dseq, pt: (dseq[g], 0, 0))

    def paged(rows, u):
        return pl.BlockSpec((None, None, rows, page),
                            lambda g, qrow, krow, dseq, pt, u=u: (layer, pt[g, u], 0, 0))

    in_specs = ([q_spec, k_spec, vt_spec, k_spec,
                 seq3(1, W_C), seq3(1, W_C), seq3(1, W_C), seq3(HEAD_ROWS, 1)]
                + [paged(W_C, u) for u in range(pages)]
                + [paged(W_C, u) for u in range(pages)]
                + [paged(8, u) for u in range(pages)])
    yc, yc_s = pl.pallas_call(
        functools.partial(_fox_kernel, nq=nq, prompt_steps=prompt_steps, decode_steps=decode_steps,
                          steps_per_seq=steps_per_seq),
        grid_spec=pltpu.PrefetchScalarGridSpec(
            num_scalar_prefetch=4,
            grid=(n_steps,),
            in_specs=in_specs,
            out_specs=[q_spec, seq3(1, W_C)],
            scratch_shapes=[pltpu.VMEM((H_C, tq, LANES), BF16), pltpu.VMEM((H_C, 1, tq), F32),
                            pltpu.VMEM((H_C, LANES, tq), F32),
                            pltpu.VMEM((HEAD_ROWS, W_C), F32), pltpu.VMEM((HEAD_ROWS, W_C), BF16),
                            pltpu.VMEM((HEAD_ROWS, 1), F32), pltpu.VMEM((HEAD_ROWS, 1), F32),
                            pltpu.VMEM((HEAD_ROWS, W_C), F32), pltpu.VMEM((8, 1), F32)]),
        out_shape=[jax.ShapeDtypeStruct((n_seq * seq_len, W_C), BF16),
                   jax.ShapeDtypeStruct((n_s, 1, W_C), F32)],
        compiler_params=_params(1),
        name="fox",
    )(qrow, krow, dseq, step_pages, q, k, vt, bias,
      q_s.reshape(n_s, 1, W_C), k_new.reshape(n_s, 1, W_C), v_new.reshape(n_s, 1, W_C), lfn,
      *([cache_k] * pages), *([cache_v] * pages), *([cache_lf] * pages))
    return yc, yc_s.reshape(n_s, W_C)


def _outproj_kernel(x_ref, ya_ref, ob_ref, bonus_ref, g_ref, yc_ref, ln_ref, avg_ref, wo_ref, o_ref):
    ob = ob_ref[...]
    mu = _dot_split(ob, avg_ref[...])
    d = ob - mu
    var = _dot((d * d).astype(BF16), avg_ref[...])
    yb = (d * lax.rsqrt(var + GN_EPS) * ln_ref[0:1, :] + ln_ref[1:2, :] + bonus_ref[...]) * g_ref[...]
    acc = _dot(ya_ref[...], wo_ref[0:W_A, :])
    acc += _dot(yb.astype(BF16), wo_ref[W_A:W_A + W_B, :])
    acc += _dot(yc_ref[...], wo_ref[W_A + W_B:, :])
    o_ref[...] = x_ref[...] + acc


def _outproj(x, ya, ob, bonus, g, yc, ln, avg_b, wo, tm=512):
    m = x.shape[0]
    tm = min(tm, m)
    row = lambda i: (i, 0)
    fix = lambda i: (0, 0)
    return pl.pallas_call(
        _outproj_kernel,
        grid=(m // tm,),
        in_specs=[pl.BlockSpec((tm, D_MODEL), row), pl.BlockSpec((tm, W_A), row),
                  pl.BlockSpec((tm, W_B), row), pl.BlockSpec((tm, W_B), row),
                  pl.BlockSpec((tm, W_B), row), pl.BlockSpec((tm, W_C), row),
                  pl.BlockSpec((8, W_B), fix), pl.BlockSpec((W_B, W_B), fix),
                  pl.BlockSpec((D_MODEL, D_MODEL), fix)],
        out_specs=pl.BlockSpec((tm, D_MODEL), row),
        out_shape=jax.ShapeDtypeStruct((m, D_MODEL), F32),
        compiler_params=_params(1),
        name="outproj",
    )(x, ya, ob, bonus, g, yc, ln, avg_b, wo)


def _block_diag_const(width, value):
    idx = jnp.arange(width) // GROUP
    return jnp.where(idx[:, None] == idx[None, :], value, 0.0).astype(BF16)


def _pad_rows(vecs, width):
    rows = [jnp.pad(v, (0, width - v.shape[0])) for v in vecs]
    rows += [jnp.zeros((width,), F32)] * (8 - len(rows))
    return jnp.stack(rows)


def _layer_weights(l, norm_g, w_ffn_in, w_ffn_out, w_in, a_ws, a_bs, a_norm_g, b_mu, b_w0, b_wB,
                   b_a0, b_aB, b_gB, b_kk, b_ka, b_rk, b_ln_g, b_ln_b, c_fb, w_o):
    zeros_lora = jnp.zeros((R_DECAY, W_B), F32)
    return dict(
        norm_g=norm_g[l],
        w_ffn_in=w_ffn_in[l].astype(BF16),
        w_ffn_out=w_ffn_out[l].astype(BF16),
        w_in=jnp.pad(w_in[l], ((0, 0), (0, IN_PROJ_PAD - IN_PROJ))).astype(BF16),
        wkv_t=w_in[l][:, A_PROJ + B_PROJ + W_C:A_PROJ + B_PROJ + 3 * W_C].T.astype(BF16),
        c_fb=jnp.pad(c_fb[l], (0, LANES - H_C)).reshape(1, LANES),
        a_ws=a_ws[l],
        a_bias=jnp.repeat(a_bs[l].T, GROUP, axis=1),
        a_norm_g=a_norm_g[l],
        b_vec=_pad_rows([b_mu[l], b_w0[l], b_a0[l], b_kk[l], b_ka[l], b_rk[l]], B_PROJ),
        b_wB=jnp.concatenate([b_wB[l], zeros_lora], axis=0).astype(BF16),
        b_aB=jnp.concatenate([zeros_lora, b_aB[l]], axis=0).astype(BF16),
        b_gB=b_gB[l].astype(BF16),
        b_ln=_pad_rows([b_ln_g[l], b_ln_b[l]], W_B),
        w_o=w_o[l].astype(BF16),
        ones_b=_block_diag_const(W_B, 1.0),
        avg_b=_block_diag_const(W_B, 1.0 / GROUP),
        avg_a=_block_diag_const(W_A, 1.0 / GROUP),
    )


def _mix_prompt(wts, z, n_seq, seq_len):
    za, zb = z[0], z[1]
    ya, _ = _gmlp(za, wts["a_norm_g"], wts["a_ws"], wts["a_bias"], wts["avg_a"])
    r, lw, k2, vb, kk, beta, g, bonus = _rwkv_prep(zb, None, seq_len, wts)
    x1, x2, ub, op, sp, gam = _rwkv_chunks(r, lw, k2, vb, kk, beta)
    ob, st = _rwkv_scan(x1, x2, ub, op, sp, gam, n_seq, seq_len)
    st = st.reshape(n_seq, N_PAIR, 2, GROUP, 2, GROUP)
    wkv = jnp.stack([st[:, :, 0, :, 0, :], st[:, :, 1, :, 1, :]], axis=2)
    wkv = wkv.reshape(n_seq, H_B, GROUP, GROUP).transpose(0, 1, 3, 2)
    return ya, ob.reshape(n_seq * seq_len, W_B), bonus, g, wkv


def _mix_sample(wts, z, shift0, wkv0):
    za, zb = z[0], z[1]
    n = za.shape[0]
    za_pad = jnp.pad(za[:, None, :], ((0, 0), (0, CHUNK_A - 1), (0, 0))).reshape(n * CHUNK_A, A_PROJ)
    ya, va = _gmlp(za_pad, wts["a_norm_g"], wts["a_ws"], wts["a_bias"], wts["avg_a"])
    ya = ya.reshape(n, CHUNK_A, W_A)[:, 0]
    va = va.reshape(n, CHUNK_A, W_A)[:, 0]
    r, lw, k2, vb, kk, beta, g, bonus = _rwkv_prep(zb, shift0, 1, wts)
    ob, wkv = _rwkv_step(wkv0, r, lw, k2, kk, beta, vb)
    return ya, ob, bonus, g, wkv, va


def kernel(x_prompt, x_sample, cache_k, cache_v, cache_logf, state_wkv, state_shift, page_table,
           norm_g, w_ffn_in, w_ffn_out, w_in, a_ws, a_bs, a_norm_g, b_mu, b_w0, b_wB, b_a0, b_aB,
           b_gB, b_kk, b_ka, b_rk, b_ln_g, b_ln_b, c_fb, w_o, final_norm):
    n_p, seq_len, _ = x_prompt.shape
    n_s = x_sample.shape[0]
    depth = norm_g.shape[0]
    n_phys, page = cache_k.shape[1], cache_k.shape[2]
    ck = jnp.transpose(cache_k, (0, 1, 3, 4, 2)).reshape(depth, n_phys, W_C, page)
    cv = jnp.transpose(cache_v, (0, 1, 3, 4, 2)).reshape(depth, n_phys, W_C, page)
    clf = jnp.pad(jnp.transpose(cache_logf, (0, 1, 3, 2)), ((0, 0), (0, 0), (0, 8 - H_C), (0, 0)))
    xp = x_prompt.reshape(n_p * seq_len, D_MODEL)
    xs = x_sample.reshape(n_s, D_MODEL)
    outs = {name: [] for name in ("kp", "vp", "lfp", "wkvp", "shp", "ks", "vs", "lfs", "wkvs", "shs", "va")}
    for l in range(depth):
        wts = _layer_weights(l, norm_g, w_ffn_in, w_ffn_out, w_in, a_ws, a_bs, a_norm_g, b_mu, b_w0,
                             b_wB, b_a0, b_aB, b_gB, b_kk, b_ka, b_rk, b_ln_g, b_ln_b, c_fb, w_o)
        last = l == depth - 1
        fin = final_norm if last else None

        xp = _ffn(xp, wts["norm_g"][0], wts["w_ffn_in"][0], wts["w_ffn_out"][0])
        xs = _ffn(xs, wts["norm_g"][0], wts["w_ffn_in"][0], wts["w_ffn_out"][0])
        zp = _inproj(xp, wts["norm_g"][1], wts["w_in"], wts["wkv_t"], wts["c_fb"], seq_len=seq_len)
        zs = _inproj(xs, wts["norm_g"][1], wts["w_in"], wts["wkv_t"], wts["c_fb"])
        ya, ob, bonus, g, wkvp = _mix_prompt(wts, zp, n_p, seq_len)
        ya_s, ob_s, bonus_s, g_s, wkvs, va = _mix_sample(wts, zs, state_shift[l], state_wkv[l])
        yc, yc_s = _fox(l, zp[2], zp[6], zp[7], _cumsum(zp[3], n_p, seq_len), n_p, seq_len,
                        zs[2], zs[4], zs[5], zs[3], ck, cv, clf, page_table)
        xp = _outproj(xp, ya, ob, bonus, g, yc, wts["b_ln"], wts["avg_b"], wts["w_o"])
        xs = _outproj(xs, ya_s, ob_s, bonus_s, g_s, yc_s.astype(BF16), wts["b_ln"], wts["avg_b"], wts["w_o"])
        xp = _ffn(xp, wts["norm_g"][2], wts["w_ffn_in"][1], wts["w_ffn_out"][1], final_g=fin)
        xs = _ffn(xs, wts["norm_g"][2], wts["w_ffn_in"][1], wts["w_ffn_out"][1], final_g=fin)
        outs["kp"].append(zp[4].reshape(n_p, H_C, GROUP, seq_len).transpose(0, 3, 1, 2))
        outs["vp"].append(zp[5].reshape(n_p, H_C, GROUP, seq_len).transpose(0, 3, 1, 2))
        outs["lfp"].append(zp[3][:, :H_C].reshape(n_p, seq_len, H_C))
        outs["wkvp"].append(wkvp)
        outs["shp"].append(zp[1].reshape(n_p, seq_len, B_PROJ)[:, -1])
        outs["ks"].append(zs[4].reshape(n_s, 1, H_C, GROUP))
        outs["vs"].append(zs[5].reshape(n_s, 1, H_C, GROUP))
        outs["lfs"].append(zs[3][:, :H_C].reshape(n_s, 1, H_C))
        outs["wkvs"].append(wkvs)
        outs["shs"].append(zs[1])
        outs["va"].append(va.reshape(n_s, 1, W_A))
    st = lambda name: jnp.stack(outs[name])
    return (xp.reshape(n_p, seq_len, D_MODEL), xs.reshape(n_s, 1, D_MODEL),
            st("kp"), st("vp"), st("lfp"), st("wkvp"), st("shp"),
            st("ks"), st("vs"), st("lfs"), st("wkvs"), st("shs"), st("va"))
```

```python
import functools

import jax
import jax.numpy as jnp
from jax import lax
from jax.experimental import pallas as pl
from jax.experimental.pallas import tpu as pltpu

F32 = jnp.float32
BF16 = jnp.bfloat16

LANES = 128
D_MODEL = 1024
D_FF = 2816
GROUP = 64
W_A = 256
W_B = 384
W_C = 384
H_B = W_B // GROUP
H_C = W_C // GROUP
N_PAIR = W_B // LANES
R_DECAY = 64
R_AAA = 64
R_GATE = 128
B_PROJ = 3 * W_B + R_DECAY + R_AAA + R_GATE
A_PROJ = 2 * W_A
C_PROJ = 3 * W_C + H_C
IN_PROJ = A_PROJ + B_PROJ + C_PROJ
IN_PROJ_PAD = A_PROJ + B_PROJ + 3 * W_C + LANES
CHUNK_A = 128
CHUNK_B = 64
NORM_EPS = 1e-6
GN_EPS = 64e-5
NEG_BIG = -1e30
LOG2E = 1.4426950408889634
Q_SCALE = LOG2E * GROUP ** -0.5
VMEM_LIMIT = 56 << 20


def _params(n_axes, vmem=VMEM_LIMIT):
    return pltpu.CompilerParams(dimension_semantics=("arbitrary",) * n_axes,
                                vmem_limit_bytes=vmem)


def _sigmoid(x):
    return 1.0 / (1.0 + jnp.exp(-x))


def _softplus(x):
    return jnp.maximum(x, 0.0) + jnp.log(1.0 + jnp.exp(-jnp.abs(x)))


def _gelu_tanh(x):
    return 0.5 * x * (1.0 + jnp.tanh(0.7978845608028654 * (x + 0.044715 * (x * x * x))))


def _rms(x, g):
    return x * lax.rsqrt(jnp.mean(x * x, axis=-1, keepdims=True) + NORM_EPS) * g


def _dot(a, b):
    return jnp.dot(a, b, preferred_element_type=F32)


def _dot_nt(a, b):
    return lax.dot_general(a, b, (((1,), (1,)), ((), ())), preferred_element_type=F32)


def _dot_split(a, b_bf):
    hi = a.astype(BF16)
    lo = (a - hi.astype(F32)).astype(BF16)
    return _dot(hi, b_bf) + _dot(lo, b_bf)


def _bf16_pieces(x):
    pieces = []
    for _ in range(3):
        piece = x.astype(BF16)
        pieces.append(piece)
        x = x - piece.astype(F32)
    return pieces


def _dot_ones_left(ones, x):
    n = x.shape[1]
    y = _dot(ones.astype(BF16), jnp.concatenate(_bf16_pieces(x), axis=1))
    return y[:, :n] + y[:, n:2 * n] + y[:, 2 * n:]


def _dot_ones_right(x, ones):
    m = x.shape[0]
    y = _dot(jnp.concatenate(_bf16_pieces(x), axis=0), ones.astype(BF16))
    return y[:m] + y[m:2 * m] + y[2 * m:]


def _iota(shape, dim):
    return lax.broadcasted_iota(jnp.int32, shape, dim)


def _ffn_kernel(*refs, n_ff, final):
    if final:
        x_ref, g_ref, wg_ref, wu_ref, wo_ref, fg_ref, o_ref, h_scr, acc_scr = refs
    else:
        x_ref, g_ref, wg_ref, wu_ref, wo_ref, o_ref, h_scr, acc_scr = refs
    j = pl.program_id(1)

    def ff_slice(h):
        gate = _dot(h, wg_ref[...])
        up = _dot(h, wu_ref[...])
        act = (gate * _sigmoid(gate) * up).astype(BF16)
        return _dot(act, wo_ref[...])

    @pl.when(j == 0)
    def _():
        h = _rms(x_ref[...], g_ref[...]).astype(BF16)
        h_scr[...] = h
        acc_scr[...] = ff_slice(h)

    if n_ff > 2:
        @pl.when(jnp.logical_and(j > 0, j < n_ff - 1))
        def _():
            acc_scr[...] += ff_slice(h_scr[...])

    @pl.when(j == n_ff - 1)
    def _():
        y = x_ref[...] + 0.5 * (acc_scr[...] + ff_slice(h_scr[...]))
        if final:
            y = _rms(y, fg_ref[...])
        o_ref[...] = y


def _ffn(x, g, w_in, w_out, final_g=None, tm=512, n_ff=2):
    assert n_ff >= 2
    m = x.shape[0]
    tm = min(tm, m)
    tf = D_FF // n_ff
    final = final_g is not None
    in_specs = [
        pl.BlockSpec((tm, D_MODEL), lambda i, j: (i, 0)),
        pl.BlockSpec((1, D_MODEL), lambda i, j: (0, 0)),
        pl.BlockSpec((D_MODEL, tf), lambda i, j: (0, j)),
        pl.BlockSpec((D_MODEL, tf), lambda i, j: (0, j + n_ff)),
        pl.BlockSpec((tf, D_MODEL), lambda i, j: (j, 0)),
    ]
    args = [x, g.reshape(1, D_MODEL), w_in, w_in, w_out]
    if final:
        in_specs.append(pl.BlockSpec((1, D_MODEL), lambda i, j: (0, 0)))
        args.append(final_g.reshape(1, D_MODEL))
    return pl.pallas_call(
        functools.partial(_ffn_kernel, n_ff=n_ff, final=final),
        grid=(m // tm, n_ff),
        in_specs=in_specs,
        out_specs=pl.BlockSpec((tm, D_MODEL), lambda i, j: (i, 0)),
        out_shape=jax.ShapeDtypeStruct((m, D_MODEL), F32),
        scratch_shapes=[pltpu.VMEM((tm, D_MODEL), BF16), pltpu.VMEM((tm, D_MODEL), F32)],
        compiler_params=_params(2),
        name="ffn",
    )(*args)


def _inproj_kernel(x_ref, g_ref, w_ref, fb_ref, za_ref, zb_ref, q_ref, lf_ref, *kv_refs,
                   channel_major):
    h = _rms(x_ref[...], g_ref[...]).astype(BF16)
    z = _dot(h, w_ref[...])
    o = A_PROJ
    za_ref[...] = z[:, :o]
    zb_ref[...] = z[:, o:o + B_PROJ]
    o += B_PROJ
    q_ref[...] = (z[:, o:o + W_C] * Q_SCALE).astype(BF16)
    k = z[:, o + W_C:o + 2 * W_C]
    v = z[:, o + 2 * W_C:o + 3 * W_C]
    lf_ref[...] = -_softplus(-(z[:, o + 3 * W_C:] + fb_ref[...]))
    if channel_major:
        kt_ref, vt_ref, kb_ref, vtb_ref = kv_refs
        kt_ref[...] = k.T
        vt = v.T
        vt_ref[...] = vt
        vtb_ref[...] = vt.astype(BF16)
        kb_ref[...] = k.astype(BF16)
    else:
        k_ref, v_ref = kv_refs
        k_ref[...] = k
        v_ref[...] = v


def _inproj(x, g, w_pad, fb_pad, seq_len=None, tm=512):
    m = x.shape[0]
    tm = min(tm, m)
    row = lambda i: (i, 0)
    fix = lambda i: (0, 0)
    widths = (A_PROJ, B_PROJ, W_C, LANES)
    dtypes = (F32, F32, BF16, F32)
    out_specs = [pl.BlockSpec((tm, w), row) for w in widths]
    out_shape = [jax.ShapeDtypeStruct((m, w), d) for w, d in zip(widths, dtypes)]
    if seq_len is None:
        out_specs += [pl.BlockSpec((tm, W_C), row)] * 2
        out_shape += [jax.ShapeDtypeStruct((m, W_C), F32)] * 2
    else:
        bps = seq_len // tm
        seq_blk = pl.BlockSpec((None, W_C, tm), lambda i: (i // bps, 0, i % bps))
        out_specs += [seq_blk, seq_blk, pl.BlockSpec((tm, W_C), row), pl.BlockSpec((W_C, tm), lambda i: (0, i))]
        out_shape += [jax.ShapeDtypeStruct((m // seq_len, W_C, seq_len), F32)] * 2
        out_shape += [jax.ShapeDtypeStruct((m, W_C), BF16), jax.ShapeDtypeStruct((W_C, m), BF16)]
    return pl.pallas_call(
        functools.partial(_inproj_kernel, channel_major=seq_len is not None),
        grid=(m // tm,),
        in_specs=[pl.BlockSpec((tm, D_MODEL), row), pl.BlockSpec((1, D_MODEL), fix),
                  pl.BlockSpec((D_MODEL, IN_PROJ_PAD), fix),
                  pl.BlockSpec((1, LANES), fix)],
        out_specs=out_specs,
        out_shape=out_shape,
        compiler_params=_params(1),
        name="inproj",
    )(x, g.reshape(1, D_MODEL), w_pad, fb_pad)


def _gmlp_kernel(za_ref, gain_ref, ws_ref, bias_ref, avg_ref, ya_ref, va_ref, *, n_chunks):
    z = _gelu_tanh(za_ref[...])
    u = z[:, :W_A]
    v = z[:, W_A:]
    ms = _dot((v * v).astype(BF16), avg_ref[...])
    vn = v * lax.rsqrt(ms + NORM_EPS) * gain_ref[...]
    va_ref[...] = vn
    causal = _iota((CHUNK_A, CHUNK_A), 0) >= _iota((CHUNK_A, CHUNK_A), 1)
    lane_group = _iota((CHUNK_A, W_A), 1) // GROUP
    wm = [jnp.where(causal, ws_ref[g], 0.0).astype(BF16) for g in range(W_A // GROUP)]
    for c in range(n_chunks):
        rows = slice(c * CHUNK_A, (c + 1) * CHUNK_A)
        vc = vn[rows]
        s = bias_ref[...]
        for g in range(W_A // GROUP):
            s = s + _dot(wm[g], jnp.where(lane_group == g, vc, 0.0).astype(BF16))
        ya_ref[rows, :] = (u[rows] * s).astype(BF16)


def _gmlp(za, gain, ws, bias_full, avg_a, tm=512):
    m = za.shape[0]
    tm = min(tm, m)
    row = lambda i: (i, 0)
    fix = lambda i: (0, 0)
    return pl.pallas_call(
        functools.partial(_gmlp_kernel, n_chunks=tm // CHUNK_A),
        grid=(m // tm,),
        in_specs=[pl.BlockSpec((tm, A_PROJ), row), pl.BlockSpec((1, W_A), fix),
                  pl.BlockSpec((W_A // GROUP, CHUNK_A, CHUNK_A), lambda i: (0, 0, 0)),
                  pl.BlockSpec((CHUNK_A, W_A), fix), pl.BlockSpec((W_A, W_A), fix)],
        out_specs=[pl.BlockSpec((tm, W_A), row), pl.BlockSpec((tm, W_A), row)],
        out_shape=[jax.ShapeDtypeStruct((m, W_A), BF16), jax.ShapeDtypeStruct((m, W_A), F32)],
        compiler_params=_params(1),
        name="gmlp",
    )(za, gain.reshape(1, W_A), ws, bias_full, avg_a)


def _rwkv_prep_math(zb, prev, vec_ref, wb_ref, ab_ref, gb_ref, ones_ref):
    mu = vec_ref[0:1, :]
    zs = zb + (prev - zb) * mu
    r = zs[:, :W_B]
    k = zs[:, W_B:2 * W_B]
    v = zs[:, 2 * W_B:3 * W_B]
    lora = zs[:, 3 * W_B:3 * W_B + LANES]
    gl = zs[:, 3 * W_B + LANES:]
    w0 = vec_ref[1:2, :W_B]
    a0 = vec_ref[2:3, :W_B]
    kkw = vec_ref[3:4, :W_B]
    kaw = vec_ref[4:5, :W_B]
    rkw = vec_ref[5:6, :W_B]
    w = -_softplus(-(w0 + _dot(jnp.tanh(lora).astype(BF16), wb_ref[...]))) - 0.5
    a = _sigmoid(a0 + _dot(lora.astype(BF16), ab_ref[...]))
    g = _dot(_sigmoid(gl).astype(BF16), gb_ref[...])
    kk = k * kkw
    ss = _dot_split(kk * kk, ones_ref[...])
    kk = kk / jnp.maximum(jnp.sqrt(ss), 1e-12)
    k2 = k * (1.0 + (a - 1.0) * kaw)
    lw = -jnp.exp(w)
    bonus = _dot_split(r * k2 * rkw, ones_ref[...]) * v
    return r, lw, k2, v, kk, kk * a, g, bonus


def _rwkv_prep_specs(wts):
    fix = lambda i: (0, 0)
    specs = [pl.BlockSpec((8, B_PROJ), fix), pl.BlockSpec((LANES, W_B), fix),
             pl.BlockSpec((LANES, W_B), fix), pl.BlockSpec((R_GATE, W_B), fix),
             pl.BlockSpec((W_B, W_B), fix)]
    return specs, [wts["b_vec"], wts["b_wB"], wts["b_aB"], wts["b_gB"], wts["ones_b"]]


def _rwkv_prep_tok_kernel(zb_ref, prev_ref, vec_ref, wb_ref, ab_ref, gb_ref, ones_ref, *outs):
    vals = _rwkv_prep_math(zb_ref[...], prev_ref[...], vec_ref, wb_ref, ab_ref, gb_ref, ones_ref)
    for ref, val in zip(outs, vals):
        ref[...] = val


def _rwkv_prep(zb, prev, wts):
    m = zb.shape[0]
    row = lambda i: (i, 0)
    w_specs, w_args = _rwkv_prep_specs(wts)
    return pl.pallas_call(
        _rwkv_prep_tok_kernel,
        grid=(1,),
        in_specs=[pl.BlockSpec((m, B_PROJ), row), pl.BlockSpec((m, B_PROJ), row)] + w_specs,
        out_specs=[pl.BlockSpec((m, W_B), row)] * 8,
        out_shape=[jax.ShapeDtypeStruct((m, W_B), F32)] * 8,
        compiler_params=_params(1),
        name="rwkv_prep",
    )(zb, prev, *w_args)


def _stack(x, low):
    return jnp.concatenate([jnp.where(low, x, 0.0), jnp.where(low, 0.0, x)], axis=0)


CHUNKS_PER_STEP = 4

def _rwkv_chunk_kernel(zb_ref, pb_ref, vec_ref, wb_ref, ab_ref, gb_ref, ones_ref,
                       x1_ref, x2_ref, ub_ref, op_ref, sp_ref, gam_ref, g_ref, bonus_ref, *, blocks_per_seq):
    c = CHUNK_B
    n2 = 2 * c
    rows = CHUNKS_PER_STEP * c
    zb = zb_ref[...]
    first = (pl.program_id(0) % blocks_per_seq) == 0
    last_prev = jnp.where(first, 0.0, pb_ref[7:8, :])
    prev = jnp.where(_iota((rows, 1), 0) == 0, last_prev, pltpu.roll(zb, shift=1, axis=0))
    r, lw, k2, v, kk, beta, g, bonus = _rwkv_prep_math(zb, prev, vec_ref, wb_ref, ab_ref, gb_ref, ones_ref)
    g_ref[...] = g
    bonus_ref[...] = bonus
    ri = _iota((rows, rows), 0)
    ci = _iota((rows, rows), 1)
    tri = jnp.where(ri >= ci, 1.0, 0.0) * jnp.where((ri // c) == (ci // c), 1.0, 0.0)
    cum = _dot_ones_left(tri, lw)
    lasts = [cum[(j + 1) * c - 1:(j + 1) * c, :] for j in range(CHUNKS_PER_STEP)]
    for j in range(CHUNKS_PER_STEP):
        gam_ref[j] = jnp.exp(lasts[j])
    cum_last = jnp.concatenate([jnp.broadcast_to(l, (c, W_B)) for l in lasts], axis=0)
    e_pos = jnp.exp(cum)
    e_neg = jnp.exp(-cum)
    e_tail = jnp.exp(cum_last - cum)
    r_t = r * e_pos
    kap_t = kk * jnp.exp(cum - lw)
    beta_h = beta * e_neg
    k_h = k2 * e_neg
    beta_c = beta * e_tail
    k_c = k2 * e_tail

    low = _iota((c, LANES), 1) < GROUP
    rr = _iota((n2, n2), 0) & (c - 1)
    cc = _iota((n2, n2), 1) & (c - 1)
    strict = rr > cc
    incl = rr >= cc
    streams = [(j, p) for j in range(CHUNKS_PER_STEP) for p in range(N_PAIR)]

    def tile(x, j, p):
        return _stack(x[j * c:(j + 1) * c, p * LANES:(p + 1) * LANES], low)

    kap_s = [tile(kap_t, j, p) for j, p in streams]
    r_s = [tile(r_t, j, p) for j, p in streams]
    v_s = [tile(v, j, p).astype(BF16) for j, p in streams]
    gram = [_dot_nt(jnp.concatenate([kap_s[i], r_s[i]], axis=0).astype(BF16),
                    jnp.concatenate([tile(beta_h, j, p), tile(k_h, j, p)], axis=0).astype(BF16))
            for i, (j, p) in enumerate(streams)]
    n_bf = [jnp.where(strict, g[:n2, :n2], 0.0).astype(BF16) for g in gram]
    av = [_dot(jnp.where(strict, g[:n2, n2:], 0.0).astype(BF16), vs) for g, vs in zip(gram, v_s)]
    for i, (j, p) in enumerate(streams):
        op_ref[j, p] = _dot(jnp.where(incl, gram[i][n2:, n2:], 0.0).astype(BF16), v_s[i])
        sp_ref[j, p] = _dot(tile(k_c, j, p).T.astype(BF16), v_s[i])
        x2_ref[j, p] = jnp.concatenate([jnp.where(incl, gram[i][n2:, :n2], 0.0),
                                        tile(beta_c, j, p).T], axis=0).astype(BF16)
    x = [jnp.concatenate([ks, -a], axis=1) for ks, a in zip(kap_s, av)]
    x = [xi - _dot(nb, xi.astype(BF16)) for xi, nb in zip(x, n_bf)]
    pw = n_bf
    for _ in range(5):
        pw = [_dot(q, q).astype(BF16) for q in pw]
        x = [xi + _dot(q, xi.astype(BF16)) for xi, q in zip(x, pw)]
    for i, (j, p) in enumerate(streams):
        x1_ref[j, p] = jnp.concatenate([x[i][:, :LANES], r_s[i]], axis=0).astype(BF16)
        ub_ref[j, p] = x[i][:, LANES:]


def _rwkv_chunks(zb, seq_len, wts):
    m = zb.shape[0]
    nc = m // CHUNK_B
    cb = CHUNKS_PER_STEP
    tm = cb * CHUNK_B
    row = lambda i: (i, 0)
    blk = lambda i: (i, 0, 0, 0)
    t = 2 * CHUNK_B
    w_specs, w_args = _rwkv_prep_specs(wts)
    return pl.pallas_call(
        functools.partial(_rwkv_chunk_kernel, blocks_per_seq=seq_len // tm),
        grid=(nc // cb,),
        in_specs=[pl.BlockSpec((tm, B_PROJ), row),
                  pl.BlockSpec((8, B_PROJ), lambda i: (jnp.maximum(i * (tm // 8) - 1, 0), 0))] + w_specs,
        out_specs=[pl.BlockSpec((cb, N_PAIR, 2 * t, LANES), blk),
                   pl.BlockSpec((cb, N_PAIR, 2 * t, LANES), blk),
                   pl.BlockSpec((cb, N_PAIR, t, LANES), blk),
                   pl.BlockSpec((cb, N_PAIR, t, LANES), blk),
                   pl.BlockSpec((cb, N_PAIR, t, LANES), blk),
                   pl.BlockSpec((cb, 1, W_B), lambda i: (i, 0, 0)),
                   pl.BlockSpec((tm, W_B), row), pl.BlockSpec((tm, W_B), row)],
        out_shape=[jax.ShapeDtypeStruct((nc, N_PAIR, 2 * t, LANES), BF16),
                   jax.ShapeDtypeStruct((nc, N_PAIR, 2 * t, LANES), BF16),
                   jax.ShapeDtypeStruct((nc, N_PAIR, t, LANES), F32),
                   jax.ShapeDtypeStruct((nc, N_PAIR, t, LANES), F32),
                   jax.ShapeDtypeStruct((nc, N_PAIR, t, LANES), F32),
                   jax.ShapeDtypeStruct((nc, 1, W_B), F32),
                   jax.ShapeDtypeStruct((m, W_B), F32), jax.ShapeDtypeStruct((m, W_B), F32)],
        compiler_params=_params(1),
        name="rwkv_chunks",
    )(zb, zb, *w_args)


def _rwkv_scan_kernel(x1_ref, x2_ref, ub_ref, op_ref, sp_ref, gam_ref, o_ref, st_ref, st_scr,
                      *, n_seq, n_chunks):
    ci = pl.program_id(0)
    t = 2 * CHUNK_B

    @pl.when(ci == 0)
    def _():
        st_scr[...] = jnp.zeros_like(st_scr)

    eye = _iota((t, t), 0) == _iota((t, t), 1)
    streams = [(b, p) for b in range(n_seq) for p in range(N_PAIR)]
    st = [st_scr[b * N_PAIR + p] for b, p in streams]
    y = [_dot(x1_ref[b, 0, p], s.astype(BF16)) for (b, p), s in zip(streams, st)]
    u = [ub_ref[b, 0, p] - yi[:t] for (b, p), yi in zip(streams, y)]
    z = [_dot(x2_ref[b, 0, p], ui.astype(BF16)) for (b, p), ui in zip(streams, u)]
    for i, (b, p) in enumerate(streams):
        o_s = op_ref[b, 0, p] + y[i][t:] + z[i][:t]
        gam_row = gam_ref[b, 0, :, p * LANES:(p + 1) * LANES]
        gam_col = jnp.sum(jnp.where(eye, gam_row, 0.0), axis=1, keepdims=True)
        st_scr[b * N_PAIR + p] = gam_col * st[i] + z[i][t:] + sp_ref[b, 0, p]
        o_ref[b, :, p * LANES:(p + 1) * LANES] = o_s[:CHUNK_B] + o_s[CHUNK_B:]

    @pl.when(ci == n_chunks - 1)
    def _():
        st_ref[...] = st_scr[...]


def _rwkv_scan(x1, x2, ub, op, sp, gam, n_seq, seq_len):
    nc = seq_len // CHUNK_B
    t = 2 * CHUNK_B
    r5 = lambda a: a.reshape((n_seq, nc) + a.shape[1:])
    blk5 = lambda rows: pl.BlockSpec((n_seq, 1, N_PAIR, rows, LANES), lambda c: (0, c, 0, 0, 0))
    return pl.pallas_call(
        functools.partial(_rwkv_scan_kernel, n_seq=n_seq, n_chunks=nc),
        grid=(nc,),
        in_specs=[blk5(2 * t), blk5(2 * t), blk5(t), blk5(t), blk5(t),
                  pl.BlockSpec((n_seq, 1, 1, W_B), lambda c: (0, c, 0, 0))],
        out_specs=[pl.BlockSpec((n_seq, CHUNK_B, W_B), lambda c: (0, c, 0)),
                   pl.BlockSpec((n_seq * N_PAIR, t, LANES), lambda c: (0, 0, 0))],
        out_shape=[jax.ShapeDtypeStruct((n_seq, seq_len, W_B), F32),
                   jax.ShapeDtypeStruct((n_seq * N_PAIR, t, LANES), F32)],
        scratch_shapes=[pltpu.VMEM((n_seq * N_PAIR, t, LANES), F32)],
        compiler_params=_params(1),
        name="rwkv_scan",
    )(r5(x1), r5(x2), r5(ub), r5(op), r5(sp), r5(gam))


STEP_SEQS = 8


def _rwkv_step_kernel(s_ref, r_ref, lw_ref, k_ref, kk_ref, beta_ref, vcol_ref, o_ref, so_ref):
    for b in range(s_ref.shape[0]):
        for h in range(H_B):
            s = s_ref[b, h]
            sk = jnp.sum(s * kk_ref[b, h], axis=1, keepdims=True)
            s_new = s * jnp.exp(lw_ref[b, h]) - sk * beta_ref[b, h] + vcol_ref[b, h] * k_ref[b, h]
            so_ref[b, h] = s_new
            o_ref[b, h] = jnp.sum(s_new * r_ref[b, h], axis=1, keepdims=True)


def _rwkv_step(state, r, lw, k2, kk, beta, v):
    n = state.shape[0]
    bs = STEP_SEQS if n % STEP_SEQS == 0 else 1
    rowv = lambda a: a.reshape(n, H_B, 1, GROUP)
    idx = lambda b: (b, 0, 0, 0)
    row_spec = pl.BlockSpec((bs, H_B, 1, GROUP), idx)
    col_spec = pl.BlockSpec((bs, H_B, GROUP, 1), idx)
    mat_spec = pl.BlockSpec((bs, H_B, GROUP, GROUP), idx)
    o, s_new = pl.pallas_call(
        _rwkv_step_kernel,
        grid=(n // bs,),
        in_specs=[mat_spec] + [row_spec] * 5 + [col_spec],
        out_specs=[col_spec, mat_spec],
        out_shape=[jax.ShapeDtypeStruct((n, H_B, GROUP, 1), F32),
                   jax.ShapeDtypeStruct((n, H_B, GROUP, GROUP), F32)],
        compiler_params=_params(1),
        name="rwkv_step",
    )(state, rowv(r), rowv(lw), rowv(k2), rowv(kk), rowv(beta), v.reshape(n, H_B, GROUP, 1))
    return o.reshape(n, W_B), s_new


BIAS_PIECES = 3


def _cumsum_kernel(lf_ref, place_ref, b_ref, carry):
    @pl.when(pl.program_id(1) == 0)
    def _():
        carry[...] = jnp.zeros_like(carry)

    tb = lf_ref.shape[0]
    tri = (_iota((tb, tb), 0) >= _iota((tb, tb), 1)).astype(F32)
    c = _dot_ones_left(tri, lf_ref[...]) + carry[...]
    carry[...] = c[tb - 1:tb, :]
    pieces = _bf16_pieces(-LOG2E * c)
    b_ref[...] = _dot(jnp.concatenate(pieces, axis=1), place_ref[...]).astype(BF16)


def _bias_placement():
    rows = jnp.arange(BIAS_PIECES * LANES)
    piece, head = rows // LANES, rows % LANES
    col = LANES * (head // 2) + jnp.where(head % 2 == 0, GROUP, 0) + piece
    hit = (col[:, None] == jnp.arange(W_C)[None, :]) & (head < H_C)[:, None]
    return hit.astype(BF16)


def _cumsum(lf, n_seq, seq_len, tb=512):
    nb = seq_len // tb
    return pl.pallas_call(
        _cumsum_kernel,
        grid=(n_seq, nb),
        in_specs=[pl.BlockSpec((tb, LANES), lambda b, j: (b * nb + j, 0)),
                  pl.BlockSpec((BIAS_PIECES * LANES, W_C), lambda b, j: (0, 0))],
        out_specs=pl.BlockSpec((tb, W_C), lambda b, j: (b * nb + j, 0)),
        out_shape=jax.ShapeDtypeStruct((n_seq * seq_len, W_C), BF16),
        scratch_shapes=[pltpu.VMEM((1, LANES), F32)],
        compiler_params=_params(2),
        name="logf_cumsum",
    )(lf, _bias_placement())


HEAD_ROWS = 16
DECODE_PAGES = 16


def _prompt_init(q_ref, qa_scr, m_scr, acc_scr):
    tq = q_ref.shape[0]
    lane = _iota((tq, LANES), 1)
    ones_hi = jnp.where(lane < GROUP + BIAS_PIECES, 1.0, 0.0).astype(BF16)
    ones_lo = jnp.where(lane < BIAS_PIECES, 1.0, 0.0).astype(BF16)
    for p in range(N_PAIR):
        q = q_ref[:, p * LANES:(p + 1) * LANES]
        qa_scr[2 * p] = jnp.where(lane < GROUP, q, ones_hi)
        qa_scr[2 * p + 1] = jnp.where(lane < GROUP, ones_lo, q)
    m_scr[...] = jnp.full_like(m_scr, NEG_BIG)
    acc_scr[...] = jnp.zeros_like(acc_scr)


def _prompt_step(k_ref, vt_ref, b_ref, qa_scr, m_scr, acc_scr, diagonal):
    tk = k_ref.shape[0]
    tq = qa_scr.shape[1]
    low_k = _iota((tk, LANES), 1) < GROUP
    low_v = _iota((LANES, tk), 0) < GROUP
    ka, va = [], []
    for p in range(N_PAIR):
        k = k_ref[:, p * LANES:(p + 1) * LANES]
        bias = b_ref[:, p * LANES:(p + 1) * LANES]
        vt = vt_ref[p * LANES:(p + 1) * LANES, :]
        one = jnp.ones_like(vt)
        ka += [jnp.where(low_k, k, bias), jnp.where(low_k, bias, k)]
        va += [jnp.where(low_v, vt, one), jnp.where(low_v, one, vt)]
    scores = [_dot_nt(ka[h], qa_scr[h]) for h in range(H_C)]
    for h in range(H_C):
        s = scores[h]
        if diagonal:
            s = jnp.where(_iota((tk, tq), 0) <= _iota((tk, tq), 1), s, NEG_BIG)
        m_prev = m_scr[h]
        m_new = jnp.maximum(m_prev, jnp.max(s, axis=0, keepdims=True))
        alpha = jnp.exp2(m_prev - m_new)
        pr = jnp.exp2(s - m_new).astype(BF16)
        acc_scr[h] = alpha * acc_scr[h] + _dot(va[h], pr)
        m_scr[h] = m_new


def _prompt_finish(o_ref, acc_scr):
    tq = o_ref.shape[0]
    low_row = _iota((LANES, tq), 0) < GROUP
    for p in range(N_PAIR):
        a0 = acc_scr[2 * p]
        a1 = acc_scr[2 * p + 1]
        out = jnp.where(low_row, a0 / a0[LANES - 1:LANES, :], a1 / a1[0:1, :])
        o_ref[:, p * LANES:(p + 1) * LANES] = out.T.astype(BF16)


def _own_head_mask():
    return (_iota((HEAD_ROWS, W_C), 1) // GROUP) == _iota((HEAD_ROWS, W_C), 0)


def _decode_init(q_ref, qf_scr, qb_scr, m_scr, l_scr, acc_scr, carry):
    qrows = jnp.where(_own_head_mask(), q_ref[0].astype(F32), 0.0)
    qf_scr[...] = qrows
    qb_scr[...] = qrows.astype(BF16)
    m_scr[...] = jnp.full_like(m_scr, NEG_BIG)
    l_scr[...] = jnp.zeros_like(l_scr)
    acc_scr[...] = jnp.zeros_like(acc_scr)
    carry[...] = jnp.zeros_like(carry)


def _decode_step(k_refs, v_refs, lf_refs, qb_scr, m_scr, l_scr, acc_scr, carry):
    pages = len(k_refs)
    page = k_refs[0].shape[1]
    upto = (_iota((page, page), 0) <= _iota((page, page), 1)).astype(F32)
    lf_all = jnp.concatenate([lf_refs[u][...] for u in range(pages)], axis=0)
    c_all = _dot_ones_right(lf_all, upto)
    totals = [c_all[8 * u:8 * (u + 1), page - 1:page] for u in range(pages)]
    run = carry[...]
    cts = []
    for u in range(pages):
        cts.append(c_all[8 * u:8 * (u + 1)] + run)
        run = run + totals[u]
    carry[...] = run
    ct = jnp.concatenate(cts, axis=1)
    ct = jnp.concatenate([ct, jnp.zeros_like(ct)], axis=0)
    kcat = jnp.concatenate([k_refs[u][...].astype(BF16) for u in range(pages)], axis=1)
    vcat = jnp.concatenate([v_refs[u][...].astype(BF16) for u in range(pages)], axis=1)
    s = _dot(qb_scr[...], kcat) - LOG2E * ct
    m_prev = m_scr[...]
    m_new = jnp.maximum(m_prev, jnp.max(s, axis=1, keepdims=True))
    alpha = jnp.exp2(m_prev - m_new)
    pr = jnp.exp2(s - m_new)
    l_scr[...] = alpha * l_scr[...] + jnp.sum(pr, axis=1, keepdims=True)
    acc_scr[...] = alpha * acc_scr[...] + _dot_nt(pr.astype(BF16), vcat)
    m_scr[...] = m_new


def _decode_finish(o_ref, kn_ref, vn_ref, lfn_ref, qf_scr, m_scr, l_scr, acc_scr, carry):
    c_past = jnp.concatenate([carry[...], jnp.zeros_like(carry)], axis=0)
    s_new = (jnp.sum(qf_scr[...] * kn_ref[0], axis=1, keepdims=True)
             - LOG2E * (c_past + lfn_ref[0]))
    m_prev = m_scr[...]
    m_new = jnp.maximum(m_prev, s_new)
    alpha = jnp.exp2(m_prev - m_new)
    pn = jnp.exp2(s_new - m_new)
    l_fin = alpha * l_scr[...] + pn
    acc = alpha * acc_scr[...] + pn * vn_ref[0]
    o_ref[0] = jnp.sum(jnp.where(_own_head_mask(), acc / l_fin, 0.0), axis=0, keepdims=True)


def _fox_kernel(qrow_ref, krow_ref, dseq_ref, pt_ref, q_ref, k_ref, vt_ref, b_ref, qs_ref, kn_ref, vn_ref,
                lfn_ref, *refs, nq, prompt_steps, decode_steps, steps_per_seq):
    pages = DECODE_PAGES
    k_refs = refs[:pages]
    v_refs = refs[pages:2 * pages]
    lf_refs = refs[2 * pages:3 * pages]
    (o_ref, os_ref, qa_scr, m_scr, acc_scr,
     qf_scr, qb_scr, dm_scr, dl_scr, dacc_scr, carry) = refs[3 * pages:]
    g = pl.program_id(0)
    i = qrow_ref[g] % nq
    j = krow_ref[g] % nq
    in_prompt = g < prompt_steps
    in_decode = g < decode_steps
    dstep = jnp.minimum(g, decode_steps - 1) % steps_per_seq
    dstate = (qb_scr, dm_scr, dl_scr, dacc_scr, carry)

    @pl.when(jnp.logical_and(in_decode, dstep == 0))
    def _():
        _decode_init(qs_ref, qf_scr, *dstate)

    @pl.when(jnp.logical_and(in_prompt, j == 0))
    def _():
        _prompt_init(q_ref, qa_scr, m_scr, acc_scr)

    @pl.when(jnp.logical_and(in_prompt, j < i))
    def _():
        _prompt_step(k_ref, vt_ref, b_ref, qa_scr, m_scr, acc_scr, False)
        _decode_step(k_refs, v_refs, lf_refs, *dstate)

    @pl.when(jnp.logical_and(in_prompt, j == i))
    def _():
        _prompt_step(k_ref, vt_ref, b_ref, qa_scr, m_scr, acc_scr, True)
        _decode_step(k_refs, v_refs, lf_refs, *dstate)
        _prompt_finish(o_ref, acc_scr)

    @pl.when(jnp.logical_not(in_prompt))
    def _():
        _decode_step(k_refs, v_refs, lf_refs, *dstate)

    @pl.when(jnp.logical_and(in_decode, dstep == steps_per_seq - 1))
    def _():
        _decode_finish(os_ref, kn_ref, vn_ref, lfn_ref, qf_scr, dm_scr, dl_scr, dacc_scr, carry)


def _fox(layer, q, k, vt, bias, n_seq, seq_len, q_s, k_new, v_new, lf_new, cache_k, cache_v, cache_lf,
         page_table, tq=512):
    nq = seq_len // tq
    pairs = [(i, j) for i in range(nq) for j in range(i + 1)]
    n_s, n_pages = page_table.shape
    page = cache_k.shape[3]
    pages = DECODE_PAGES
    steps_per_seq = n_pages // pages
    prompt_steps = n_seq * len(pairs)
    decode_steps = n_s * steps_per_seq
    n_steps = max(prompt_steps, decode_steps)
    lfn = jnp.pad(lf_new[:, :H_C], ((0, 0), (0, HEAD_ROWS - H_C))).reshape(n_s, HEAD_ROWS, 1)
    p_of = [min(g, prompt_steps - 1) for g in range(n_steps)]
    d_of = [min(g, decode_steps - 1) for g in range(n_steps)]
    qrow = jnp.asarray([(p // len(pairs)) * nq + pairs[p % len(pairs)][0] for p in p_of], jnp.int32)
    krow = jnp.asarray([(p // len(pairs)) * nq + pairs[p % len(pairs)][1] for p in p_of], jnp.int32)
    dseq = jnp.asarray([d // steps_per_seq for d in d_of], jnp.int32)
    step_pages = page_table.reshape(decode_steps, pages)[jnp.asarray(d_of, jnp.int32)]

    q_spec = pl.BlockSpec((tq, W_C), lambda g, qrow, krow, dseq, pt: (qrow[g], 0))
    k_spec = pl.BlockSpec((tq, W_C), lambda g, qrow, krow, dseq, pt: (krow[g], 0))
    vt_spec = pl.BlockSpec((W_C, tq), lambda g, qrow, krow, dseq, pt: (0, krow[g]))
    seq3 = lambda rows, w: pl.BlockSpec((1, rows, w), lambda g, qrow, krow, dseq, pt: (dseq[g], 0, 0))

    def paged(rows, u):
        return pl.BlockSpec((None, None, rows, page),
                            lambda g, qrow, krow, dseq, pt, u=u: (layer, pt[g, u], 0, 0))

    in_specs = ([q_spec, k_spec, vt_spec, k_spec,
                 seq3(1, W_C), seq3(1, W_C), seq3(1, W_C), seq3(HEAD_ROWS, 1)]
                + [paged(W_C, u) for u in range(pages)]
                + [paged(W_C, u) for u in range(pages)]
                + [paged(8, u) for u in range(pages)])
    yc, yc_s = pl.pallas_call(
        functools.partial(_fox_kernel, nq=nq, prompt_steps=prompt_steps, decode_steps=decode_steps,
                          steps_per_seq=steps_per_seq),
        grid_spec=pltpu.PrefetchScalarGridSpec(
            num_scalar_prefetch=4,
            grid=(n_steps,),
            in_specs=in_specs,
            out_specs=[q_spec, seq3(1, W_C)],
            scratch_shapes=[pltpu.VMEM((H_C, tq, LANES), BF16), pltpu.VMEM((H_C, 1, tq), F32),
                            pltpu.VMEM((H_C, LANES, tq), F32),
                            pltpu.VMEM((HEAD_ROWS, W_C), F32), pltpu.VMEM((HEAD_ROWS, W_C), BF16),
                            pltpu.VMEM((HEAD_ROWS, 1), F32), pltpu.VMEM((HEAD_ROWS, 1), F32),
                            pltpu.VMEM((HEAD_ROWS, W_C), F32), pltpu.VMEM((8, 1), F32)]),
        out_shape=[jax.ShapeDtypeStruct((n_seq * seq_len, W_C), BF16),
                   jax.ShapeDtypeStruct((n_s, 1, W_C), F32)],
        compiler_params=_params(1),
        name="fox",
    )(qrow, krow, dseq, step_pages, q, k, vt, bias,
      q_s.reshape(n_s, 1, W_C), k_new.reshape(n_s, 1, W_C), v_new.reshape(n_s, 1, W_C), lfn,
      *([cache_k] * pages), *([cache_v] * pages), *([cache_lf] * pages))
    return yc, yc_s.reshape(n_s, W_C)


def _outproj_kernel(x_ref, ya_ref, ob_ref, bonus_ref, g_ref, yc_ref, ln_ref, avg_ref, wo_ref, o_ref):
    ob = ob_ref[...]
    mu = _dot_split(ob, avg_ref[...])
    d = ob - mu
    var = _dot((d * d).astype(BF16), avg_ref[...])
    yb = (d * lax.rsqrt(var + GN_EPS) * ln_ref[0:1, :] + ln_ref[1:2, :] + bonus_ref[...]) * g_ref[...]
    acc = _dot(ya_ref[...], wo_ref[0:W_A, :])
    acc += _dot(yb.astype(BF16), wo_ref[W_A:W_A + W_B, :])
    acc += _dot(yc_ref[...], wo_ref[W_A + W_B:, :])
    o_ref[...] = x_ref[...] + acc


def _outproj(x, ya, ob, bonus, g, yc, ln, avg_b, wo, tm=512):
    m = x.shape[0]
    tm = min(tm, m)
    row = lambda i: (i, 0)
    fix = lambda i: (0, 0)
    return pl.pallas_call(
        _outproj_kernel,
        grid=(m // tm,),
        in_specs=[pl.BlockSpec((tm, D_MODEL), row), pl.BlockSpec((tm, W_A), row),
                  pl.BlockSpec((tm, W_B), row), pl.BlockSpec((tm, W_B), row),
                  pl.BlockSpec((tm, W_B), row), pl.BlockSpec((tm, W_C), row),
                  pl.BlockSpec((8, W_B), fix), pl.BlockSpec((W_B, W_B), fix),
                  pl.BlockSpec((D_MODEL, D_MODEL), fix)],
        out_specs=pl.BlockSpec((tm, D_MODEL), row),
        out_shape=jax.ShapeDtypeStruct((m, D_MODEL), F32),
        compiler_params=_params(1),
        name="outproj",
    )(x, ya, ob, bonus, g, yc, ln, avg_b, wo)


def _block_diag_const(width, value):
    idx = jnp.arange(width) // GROUP
    return jnp.where(idx[:, None] == idx[None, :], value, 0.0).astype(BF16)


def _pad_rows(vecs, width):
    rows = [jnp.pad(v, (0, width - v.shape[0])) for v in vecs]
    rows += [jnp.zeros((width,), F32)] * (8 - len(rows))
    return jnp.stack(rows)


def _layer_weights(l, norm_g, w_ffn_in, w_ffn_out, w_in, a_ws, a_bs, a_norm_g, b_mu, b_w0, b_wB,
                   b_a0, b_aB, b_gB, b_kk, b_ka, b_rk, b_ln_g, b_ln_b, c_fb, w_o):
    zeros_lora = jnp.zeros((R_DECAY, W_B), F32)
    return dict(
        norm_g=norm_g[l],
        w_ffn_in=w_ffn_in[l].astype(BF16),
        w_ffn_out=w_ffn_out[l].astype(BF16),
        w_in=jnp.pad(w_in[l], ((0, 0), (0, IN_PROJ_PAD - IN_PROJ))).astype(BF16),
        c_fb=jnp.pad(c_fb[l], (0, LANES - H_C)).reshape(1, LANES),
        a_ws=a_ws[l],
        a_bias=jnp.repeat(a_bs[l].T, GROUP, axis=1),
        a_norm_g=a_norm_g[l],
        b_vec=_pad_rows([b_mu[l], b_w0[l], b_a0[l], b_kk[l], b_ka[l], b_rk[l]], B_PROJ),
        b_wB=jnp.concatenate([b_wB[l], zeros_lora], axis=0).astype(BF16),
        b_aB=jnp.concatenate([zeros_lora, b_aB[l]], axis=0).astype(BF16),
        b_gB=b_gB[l].astype(BF16),
        b_ln=_pad_rows([b_ln_g[l], b_ln_b[l]], W_B),
        w_o=w_o[l].astype(BF16),
        ones_b=_block_diag_const(W_B, 1.0),
        avg_b=_block_diag_const(W_B, 1.0 / GROUP),
        avg_a=_block_diag_const(W_A, 1.0 / GROUP),
    )


def _mix_prompt(wts, z, n_seq, seq_len):
    za, zb = z[0], z[1]
    ya, _ = _gmlp(za, wts["a_norm_g"], wts["a_ws"], wts["a_bias"], wts["avg_a"])
    x1, x2, ub, op, sp, gam, g, bonus = _rwkv_chunks(zb, seq_len, wts)
    ob, st = _rwkv_scan(x1, x2, ub, op, sp, gam, n_seq, seq_len)
    st = st.reshape(n_seq, N_PAIR, 2, GROUP, 2, GROUP)
    wkv = jnp.stack([st[:, :, 0, :, 0, :], st[:, :, 1, :, 1, :]], axis=2)
    wkv = wkv.reshape(n_seq, H_B, GROUP, GROUP).transpose(0, 1, 3, 2)
    return ya, ob.reshape(n_seq * seq_len, W_B), bonus, g, wkv


def _mix_sample(wts, z, shift0, wkv0):
    za, zb = z[0], z[1]
    n = za.shape[0]
    za_pad = jnp.pad(za[:, None, :], ((0, 0), (0, CHUNK_A - 1), (0, 0))).reshape(n * CHUNK_A, A_PROJ)
    ya, va = _gmlp(za_pad, wts["a_norm_g"], wts["a_ws"], wts["a_bias"], wts["avg_a"])
    ya = ya.reshape(n, CHUNK_A, W_A)[:, 0]
    va = va.reshape(n, CHUNK_A, W_A)[:, 0]
    r, lw, k2, vb, kk, beta, g, bonus = _rwkv_prep(zb, shift0, wts)
    ob, wkv = _rwkv_step(wkv0, r, lw, k2, kk, beta, vb)
    return ya, ob, bonus, g, wkv, va


def kernel(x_prompt, x_sample, cache_k, cache_v, cache_logf, state_wkv, state_shift, page_table,
           norm_g, w_ffn_in, w_ffn_out, w_in, a_ws, a_bs, a_norm_g, b_mu, b_w0, b_wB, b_a0, b_aB,
           b_gB, b_kk, b_ka, b_rk, b_ln_g, b_ln_b, c_fb, w_o, final_norm):
    n_p, seq_len, _ = x_prompt.shape
    n_s = x_sample.shape[0]
    depth = norm_g.shape[0]
    n_phys, page = cache_k.shape[1], cache_k.shape[2]
    ck = jnp.transpose(cache_k, (0, 1, 3, 4, 2)).reshape(depth, n_phys, W_C, page)
    cv = jnp.transpose(cache_v, (0, 1, 3, 4, 2)).reshape(depth, n_phys, W_C, page)
    clf = jnp.pad(jnp.transpose(cache_logf, (0, 1, 3, 2)), ((0, 0), (0, 0), (0, 8 - H_C), (0, 0)))
    xp = x_prompt.reshape(n_p * seq_len, D_MODEL)
    xs = x_sample.reshape(n_s, D_MODEL)
    outs = {name: [] for name in ("kp", "vp", "lfp", "wkvp", "shp", "ks", "vs", "lfs", "wkvs", "shs", "va")}
    for l in range(depth):
        wts = _layer_weights(l, norm_g, w_ffn_in, w_ffn_out, w_in, a_ws, a_bs, a_norm_g, b_mu, b_w0,
                             b_wB, b_a0, b_aB, b_gB, b_kk, b_ka, b_rk, b_ln_g, b_ln_b, c_fb, w_o)
        last = l == depth - 1
        fin = final_norm if last else None

        xp = _ffn(xp, wts["norm_g"][0], wts["w_ffn_in"][0], wts["w_ffn_out"][0])
        xs = _ffn(xs, wts["norm_g"][0], wts["w_ffn_in"][0], wts["w_ffn_out"][0])
        zp = _inproj(xp, wts["norm_g"][1], wts["w_in"], wts["c_fb"], seq_len=seq_len)
        zs = _inproj(xs, wts["norm_g"][1], wts["w_in"], wts["c_fb"])
        ya, ob, bonus, g, wkvp = _mix_prompt(wts, zp, n_p, seq_len)
        ya_s, ob_s, bonus_s, g_s, wkvs, va = _mix_sample(wts, zs, state_shift[l], state_wkv[l])
        yc, yc_s = _fox(l, zp[2], zp[6], zp[7], _cumsum(zp[3], n_p, seq_len), n_p, seq_len,
                        zs[2], zs[4], zs[5], zs[3], ck, cv, clf, page_table)
        xp = _outproj(xp, ya, ob, bonus, g, yc, wts["b_ln"], wts["avg_b"], wts["w_o"])
        xs = _outproj(xs, ya_s, ob_s, bonus_s, g_s, yc_s.astype(BF16), wts["b_ln"], wts["avg_b"], wts["w_o"])
        xp = _ffn(xp, wts["norm_g"][2], wts["w_ffn_in"][1], wts["w_ffn_out"][1], final_g=fin)
        xs = _ffn(xs, wts["norm_g"][2], wts["w_ffn_in"][1], wts["w_ffn_out"][1], final_g=fin)
        outs["kp"].append(zp[4].reshape(n_p, H_C, GROUP, seq_len).transpose(0, 3, 1, 2))
        outs["vp"].append(zp[5].reshape(n_p, H_C, GROUP, seq_len).transpose(0, 3, 1, 2))
        outs["lfp"].append(zp[3][:, :H_C].reshape(n_p, seq_len, H_C))
        outs["wkvp"].append(wkvp)
        outs["shp"].append(zp[1].reshape(n_p, seq_len, B_PROJ)[:, -1])
        outs["ks"].append(zs[4].reshape(n_s, 1, H_C, GROUP))
        outs["vs"].append(zs[5].reshape(n_s, 1, H_C, GROUP))
        outs["lfs"].append(zs[3][:, :H_C].reshape(n_s, 1, H_C))
        outs["wkvs"].append(wkvs)
        outs["shs"].append(zs[1])
        outs["va"].append(va.reshape(n_s, 1, W_A))
    st = lambda name: jnp.stack(outs[name])
    return (xp.reshape(n_p, seq_len, D_MODEL), xs.reshape(n_s, 1, D_MODEL),
            st("kp"), st("vp"), st("lfp"), st("wkvp"), st("shp"),
            st("ks"), st("vs"), st("lfs"), st("wkvs"), st("shs"), st("va"))
```

```python
import functools

import jax
import jax.numpy as jnp
from jax import lax
from jax.experimental import pallas as pl
from jax.experimental.pallas import tpu as pltpu

F32 = jnp.float32
BF16 = jnp.bfloat16

LANES = 128
D_MODEL = 1024
D_FF = 2816
GROUP = 64
W_A = 256
W_B = 384
W_C = 384
H_B = W_B // GROUP
H_C = W_C // GROUP
N_PAIR = W_B // LANES
R_DECAY = 64
R_AAA = 64
R_GATE = 128
B_PROJ = 3 * W_B + R_DECAY + R_AAA + R_GATE
A_PROJ = 2 * W_A
C_PROJ = 3 * W_C + H_C
IN_PROJ = A_PROJ + B_PROJ + C_PROJ
IN_PROJ_PAD = A_PROJ + B_PROJ + 3 * W_C + LANES
CHUNK_A = 128
CHUNK_B = 64
NORM_EPS = 1e-6
GN_EPS = 64e-5
NEG_BIG = -1e30
LOG2E = 1.4426950408889634
Q_SCALE = LOG2E * GROUP ** -0.5
VMEM_LIMIT = 56 << 20


def _params(n_axes, vmem=VMEM_LIMIT):
    return pltpu.CompilerParams(dimension_semantics=("arbitrary",) * n_axes,
                                vmem_limit_bytes=vmem)


def _sigmoid(x):
    return 1.0 / (1.0 + jnp.exp(-x))


def _softplus(x):
    return jnp.maximum(x, 0.0) + jnp.log(1.0 + jnp.exp(-jnp.abs(x)))


def _gelu_tanh(x):
    return 0.5 * x * (1.0 + jnp.tanh(0.7978845608028654 * (x + 0.044715 * (x * x * x))))


def _rms(x, g):
    return x * lax.rsqrt(jnp.mean(x * x, axis=-1, keepdims=True) + NORM_EPS) * g


def _dot(a, b):
    return jnp.dot(a, b, preferred_element_type=F32)


def _dot_nt(a, b):
    return lax.dot_general(a, b, (((1,), (1,)), ((), ())), preferred_element_type=F32)


def _dot_split(a, b_bf):
    hi = a.astype(BF16)
    lo = (a - hi.astype(F32)).astype(BF16)
    return _dot(hi, b_bf) + _dot(lo, b_bf)


def _bf16_pieces(x):
    pieces = []
    for _ in range(3):
        piece = x.astype(BF16)
        pieces.append(piece)
        x = x - piece.astype(F32)
    return pieces


def _dot_ones_left(ones, x):
    n = x.shape[1]
    y = _dot(ones.astype(BF16), jnp.concatenate(_bf16_pieces(x), axis=1))
    return y[:, :n] + y[:, n:2 * n] + y[:, 2 * n:]


def _dot_ones_right(x, ones):
    m = x.shape[0]
    y = _dot(jnp.concatenate(_bf16_pieces(x), axis=0), ones.astype(BF16))
    return y[:m] + y[m:2 * m] + y[2 * m:]


def _iota(shape, dim):
    return lax.broadcasted_iota(jnp.int32, shape, dim)


MXU_TILE = 256
FF_SLICES = (6 * MXU_TILE, 5 * MXU_TILE)
assert sum(FF_SLICES) == D_FF


def _ffn_kernel(*refs, final):
    if final:
        x_ref, g_ref, wi_ref, wo_ref, fg_ref, o_ref = refs
    else:
        x_ref, g_ref, wi_ref, wo_ref, o_ref = refs
    x = x_ref[...]
    h = _rms(x, g_ref[...]).astype(BF16)
    acc = None
    lo = 0
    for width in FF_SLICES:
        gate = _dot(h, wi_ref[:, lo:lo + width])
        up = _dot(h, wi_ref[:, D_FF + lo:D_FF + lo + width])
        act = (gate * _sigmoid(gate) * up).astype(BF16)
        part = _dot(act, wo_ref[lo:lo + width, :])
        acc = part if acc is None else acc + part
        lo += width
    y = x + 0.5 * acc
    if final:
        y = _rms(y, fg_ref[...])
    o_ref[...] = y


def _ffn(x, g, w_in, w_out, final_g=None, tm=512):
    m = x.shape[0]
    tm = min(tm, m)
    final = final_g is not None
    fix = lambda i: (0, 0)
    resident = pl.Buffered(1)
    in_specs = [
        pl.BlockSpec((tm, D_MODEL), lambda i: (i, 0)),
        pl.BlockSpec((1, D_MODEL), fix),
        pl.BlockSpec((D_MODEL, 2 * D_FF), fix, pipeline_mode=resident),
        pl.BlockSpec((D_FF, D_MODEL), fix, pipeline_mode=resident),
    ]
    args = [x, g.reshape(1, D_MODEL), w_in, w_out]
    if final:
        in_specs.append(pl.BlockSpec((1, D_MODEL), fix))
        args.append(final_g.reshape(1, D_MODEL))
    return pl.pallas_call(
        functools.partial(_ffn_kernel, final=final),
        grid=(m // tm,),
        in_specs=in_specs,
        out_specs=pl.BlockSpec((tm, D_MODEL), lambda i: (i, 0)),
        out_shape=jax.ShapeDtypeStruct((m, D_MODEL), F32),
        compiler_params=_params(1),
        name="ffn",
    )(*args)


def _inproj_kernel(x_ref, g_ref, w_ref, fb_ref, za_ref, zb_ref, q_ref, lf_ref, *kv_refs,
                   channel_major):
    h = _rms(x_ref[...], g_ref[...]).astype(BF16)
    z = _dot(h, w_ref[...])
    o = A_PROJ
    za_ref[...] = z[:, :o]
    zb_ref[...] = z[:, o:o + B_PROJ]
    o += B_PROJ
    q_ref[...] = (z[:, o:o + W_C] * Q_SCALE).astype(BF16)
    k = z[:, o + W_C:o + 2 * W_C]
    v = z[:, o + 2 * W_C:o + 3 * W_C]
    lf_ref[...] = -_softplus(-(z[:, o + 3 * W_C:] + fb_ref[...]))
    if channel_major:
        kt_ref, vt_ref, kb_ref, vtb_ref = kv_refs
        kt_ref[...] = k.T
        vt = v.T
        vt_ref[...] = vt
        vtb_ref[...] = vt.astype(BF16)
        kb_ref[...] = k.astype(BF16)
    else:
        k_ref, v_ref = kv_refs
        k_ref[...] = k
        v_ref[...] = v


def _inproj(x, g, w_pad, fb_pad, seq_len=None, tm=512):
    m = x.shape[0]
    tm = min(tm, m)
    row = lambda i: (i, 0)
    fix = lambda i: (0, 0)
    widths = (A_PROJ, B_PROJ, W_C, LANES)
    dtypes = (F32, F32, BF16, F32)
    out_specs = [pl.BlockSpec((tm, w), row) for w in widths]
    out_shape = [jax.ShapeDtypeStruct((m, w), d) for w, d in zip(widths, dtypes)]
    if seq_len is None:
        out_specs += [pl.BlockSpec((tm, W_C), row)] * 2
        out_shape += [jax.ShapeDtypeStruct((m, W_C), F32)] * 2
    else:
        bps = seq_len // tm
        seq_blk = pl.BlockSpec((None, W_C, tm), lambda i: (i // bps, 0, i % bps))
        out_specs += [seq_blk, seq_blk, pl.BlockSpec((tm, W_C), row), pl.BlockSpec((W_C, tm), lambda i: (0, i))]
        out_shape += [jax.ShapeDtypeStruct((m // seq_len, W_C, seq_len), F32)] * 2
        out_shape += [jax.ShapeDtypeStruct((m, W_C), BF16), jax.ShapeDtypeStruct((W_C, m), BF16)]
    return pl.pallas_call(
        functools.partial(_inproj_kernel, channel_major=seq_len is not None),
        grid=(m // tm,),
        in_specs=[pl.BlockSpec((tm, D_MODEL), row), pl.BlockSpec((1, D_MODEL), fix),
                  pl.BlockSpec((D_MODEL, IN_PROJ_PAD), fix),
                  pl.BlockSpec((1, LANES), fix)],
        out_specs=out_specs,
        out_shape=out_shape,
        compiler_params=_params(1),
        name="inproj",
    )(x, g.reshape(1, D_MODEL), w_pad, fb_pad)


def _gmlp_kernel(za_ref, gain_ref, ws_ref, bias_ref, avg_ref, ya_ref, va_ref, *, n_chunks):
    z = _gelu_tanh(za_ref[...])
    u = z[:, :W_A]
    v = z[:, W_A:]
    ms = _dot((v * v).astype(BF16), avg_ref[...])
    vn = v * lax.rsqrt(ms + NORM_EPS) * gain_ref[...]
    va_ref[...] = vn
    causal = _iota((CHUNK_A, CHUNK_A), 0) >= _iota((CHUNK_A, CHUNK_A), 1)
    lane_group = _iota((CHUNK_A, W_A), 1) // GROUP
    wm = [jnp.where(causal, ws_ref[g], 0.0).astype(BF16) for g in range(W_A // GROUP)]
    for c in range(n_chunks):
        rows = slice(c * CHUNK_A, (c + 1) * CHUNK_A)
        vc = vn[rows]
        s = bias_ref[...]
        for g in range(W_A // GROUP):
            s = s + _dot(wm[g], jnp.where(lane_group == g, vc, 0.0).astype(BF16))
        ya_ref[rows, :] = (u[rows] * s).astype(BF16)


def _gmlp(za, gain, ws, bias_full, avg_a, tm=512):
    m = za.shape[0]
    tm = min(tm, m)
    row = lambda i: (i, 0)
    fix = lambda i: (0, 0)
    return pl.pallas_call(
        functools.partial(_gmlp_kernel, n_chunks=tm // CHUNK_A),
        grid=(m // tm,),
        in_specs=[pl.BlockSpec((tm, A_PROJ), row), pl.BlockSpec((1, W_A), fix),
                  pl.BlockSpec((W_A // GROUP, CHUNK_A, CHUNK_A), lambda i: (0, 0, 0)),
                  pl.BlockSpec((CHUNK_A, W_A), fix), pl.BlockSpec((W_A, W_A), fix)],
        out_specs=[pl.BlockSpec((tm, W_A), row), pl.BlockSpec((tm, W_A), row)],
        out_shape=[jax.ShapeDtypeStruct((m, W_A), BF16), jax.ShapeDtypeStruct((m, W_A), F32)],
        compiler_params=_params(1),
        name="gmlp",
    )(za, gain.reshape(1, W_A), ws, bias_full, avg_a)


def _rwkv_prep_math(zb, prev, vec_ref, wb_ref, ab_ref, gb_ref, ones_ref):
    mu = vec_ref[0:1, :]
    zs = zb + (prev - zb) * mu
    r = zs[:, :W_B]
    k = zs[:, W_B:2 * W_B]
    v = zs[:, 2 * W_B:3 * W_B]
    lora = zs[:, 3 * W_B:3 * W_B + LANES]
    gl = zs[:, 3 * W_B + LANES:]
    w0 = vec_ref[1:2, :W_B]
    a0 = vec_ref[2:3, :W_B]
    kkw = vec_ref[3:4, :W_B]
    kaw = vec_ref[4:5, :W_B]
    rkw = vec_ref[5:6, :W_B]
    w = -_softplus(-(w0 + _dot(jnp.tanh(lora).astype(BF16), wb_ref[...]))) - 0.5
    a = _sigmoid(a0 + _dot(lora.astype(BF16), ab_ref[...]))
    g = _dot(_sigmoid(gl).astype(BF16), gb_ref[...])
    kk = k * kkw
    ss = _dot_split(kk * kk, ones_ref[...])
    kk = kk / jnp.maximum(jnp.sqrt(ss), 1e-12)
    k2 = k * (1.0 + (a - 1.0) * kaw)
    lw = -jnp.exp(w)
    bonus = _dot_split(r * k2 * rkw, ones_ref[...]) * v
    return r, lw, k2, v, kk, kk * a, g, bonus


def _rwkv_prep_specs(wts):
    fix = lambda i: (0, 0)
    specs = [pl.BlockSpec((8, B_PROJ), fix), pl.BlockSpec((LANES, W_B), fix),
             pl.BlockSpec((LANES, W_B), fix), pl.BlockSpec((R_GATE, W_B), fix),
             pl.BlockSpec((W_B, W_B), fix)]
    return specs, [wts["b_vec"], wts["b_wB"], wts["b_aB"], wts["b_gB"], wts["ones_b"]]


def _rwkv_prep_tok_kernel(zb_ref, prev_ref, vec_ref, wb_ref, ab_ref, gb_ref, ones_ref, *outs):
    vals = _rwkv_prep_math(zb_ref[...], prev_ref[...], vec_ref, wb_ref, ab_ref, gb_ref, ones_ref)
    for ref, val in zip(outs, vals):
        ref[...] = val


def _rwkv_prep(zb, prev, wts):
    m = zb.shape[0]
    row = lambda i: (i, 0)
    w_specs, w_args = _rwkv_prep_specs(wts)
    return pl.pallas_call(
        _rwkv_prep_tok_kernel,
        grid=(1,),
        in_specs=[pl.BlockSpec((m, B_PROJ), row), pl.BlockSpec((m, B_PROJ), row)] + w_specs,
        out_specs=[pl.BlockSpec((m, W_B), row)] * 8,
        out_shape=[jax.ShapeDtypeStruct((m, W_B), F32)] * 8,
        compiler_params=_params(1),
        name="rwkv_prep",
    )(zb, prev, *w_args)


def _stack(x, low):
    return jnp.concatenate([jnp.where(low, x, 0.0), jnp.where(low, 0.0, x)], axis=0)


CHUNKS_PER_STEP = 4

def _rwkv_chunk_kernel(zb_ref, pb_ref, vec_ref, wb_ref, ab_ref, gb_ref, ones_ref,
                       x1_ref, x2_ref, ub_ref, op_ref, sp_ref, gam_ref, g_ref, bonus_ref, *, blocks_per_seq):
    c = CHUNK_B
    n2 = 2 * c
    rows = CHUNKS_PER_STEP * c
    zb = zb_ref[...]
    first = (pl.program_id(0) % blocks_per_seq) == 0
    last_prev = jnp.where(first, 0.0, pb_ref[7:8, :])
    prev = jnp.where(_iota((rows, 1), 0) == 0, last_prev, pltpu.roll(zb, shift=1, axis=0))
    r, lw, k2, v, kk, beta, g, bonus = _rwkv_prep_math(zb, prev, vec_ref, wb_ref, ab_ref, gb_ref, ones_ref)
    g_ref[...] = g
    bonus_ref[...] = bonus
    ri = _iota((rows, rows), 0)
    ci = _iota((rows, rows), 1)
    tri = jnp.where(ri >= ci, 1.0, 0.0) * jnp.where((ri // c) == (ci // c), 1.0, 0.0)
    cum = _dot_ones_left(tri, lw)
    lasts = [cum[(j + 1) * c - 1:(j + 1) * c, :] for j in range(CHUNKS_PER_STEP)]
    for j in range(CHUNKS_PER_STEP):
        gam_ref[j] = jnp.exp(lasts[j])
    cum_last = jnp.concatenate([jnp.broadcast_to(l, (c, W_B)) for l in lasts], axis=0)
    e_pos = jnp.exp(cum)
    e_neg = jnp.exp(-cum)
    e_tail = jnp.exp(cum_last - cum)
    r_t = r * e_pos
    kap_t = kk * jnp.exp(cum - lw)
    beta_h = beta * e_neg
    k_h = k2 * e_neg
    beta_c = beta * e_tail
    k_c = k2 * e_tail

    low = _iota((c, LANES), 1) < GROUP
    rr = _iota((n2, n2), 0) & (c - 1)
    cc = _iota((n2, n2), 1) & (c - 1)
    strict = rr > cc
    incl = rr >= cc
    streams = [(j, p) for j in range(CHUNKS_PER_STEP) for p in range(N_PAIR)]

    def tile(x, j, p):
        return _stack(x[j * c:(j + 1) * c, p * LANES:(p + 1) * LANES], low)

    kap_s = [tile(kap_t, j, p) for j, p in streams]
    r_s = [tile(r_t, j, p) for j, p in streams]
    v_s = [tile(v, j, p).astype(BF16) for j, p in streams]
    gram = [_dot_nt(jnp.concatenate([kap_s[i], r_s[i]], axis=0).astype(BF16),
                    jnp.concatenate([tile(beta_h, j, p), tile(k_h, j, p)], axis=0).astype(BF16))
            for i, (j, p) in enumerate(streams)]
    n_bf = [jnp.where(strict, g[:n2, :n2], 0.0).astype(BF16) for g in gram]
    av = [_dot(jnp.where(strict, g[:n2, n2:], 0.0).astype(BF16), vs) for g, vs in zip(gram, v_s)]
    for i, (j, p) in enumerate(streams):
        op_ref[j, p] = _dot(jnp.where(incl, gram[i][n2:, n2:], 0.0).astype(BF16), v_s[i])
        sp_ref[j, p] = _dot(tile(k_c, j, p).T.astype(BF16), v_s[i])
        x2_ref[j, p] = jnp.concatenate([jnp.where(incl, gram[i][n2:, :n2], 0.0),
                                        tile(beta_c, j, p).T], axis=0).astype(BF16)
    x = [jnp.concatenate([ks, -a], axis=1) for ks, a in zip(kap_s, av)]
    x = [xi - _dot(nb, xi.astype(BF16)) for xi, nb in zip(x, n_bf)]
    pw = n_bf
    for _ in range(5):
        pw = [_dot(q, q).astype(BF16) for q in pw]
        x = [xi + _dot(q, xi.astype(BF16)) for xi, q in zip(x, pw)]
    for i, (j, p) in enumerate(streams):
        x1_ref[j, p] = jnp.concatenate([x[i][:, :LANES], r_s[i]], axis=0).astype(BF16)
        ub_ref[j, p] = x[i][:, LANES:]


def _rwkv_chunks(zb, seq_len, wts):
    m = zb.shape[0]
    nc = m // CHUNK_B
    cb = CHUNKS_PER_STEP
    tm = cb * CHUNK_B
    row = lambda i: (i, 0)
    blk = lambda i: (i, 0, 0, 0)
    t = 2 * CHUNK_B
    w_specs, w_args = _rwkv_prep_specs(wts)
    return pl.pallas_call(
        functools.partial(_rwkv_chunk_kernel, blocks_per_seq=seq_len // tm),
        grid=(nc // cb,),
        in_specs=[pl.BlockSpec((tm, B_PROJ), row),
                  pl.BlockSpec((8, B_PROJ), lambda i: (jnp.maximum(i * (tm // 8) - 1, 0), 0))] + w_specs,
        out_specs=[pl.BlockSpec((cb, N_PAIR, 2 * t, LANES), blk),
                   pl.BlockSpec((cb, N_PAIR, 2 * t, LANES), blk),
                   pl.BlockSpec((cb, N_PAIR, t, LANES), blk),
                   pl.BlockSpec((cb, N_PAIR, t, LANES), blk),
                   pl.BlockSpec((cb, N_PAIR, t, LANES), blk),
                   pl.BlockSpec((cb, 1, W_B), lambda i: (i, 0, 0)),
                   pl.BlockSpec((tm, W_B), row), pl.BlockSpec((tm, W_B), row)],
        out_shape=[jax.ShapeDtypeStruct((nc, N_PAIR, 2 * t, LANES), BF16),
                   jax.ShapeDtypeStruct((nc, N_PAIR, 2 * t, LANES), BF16),
                   jax.ShapeDtypeStruct((nc, N_PAIR, t, LANES), F32),
                   jax.ShapeDtypeStruct((nc, N_PAIR, t, LANES), F32),
                   jax.ShapeDtypeStruct((nc, N_PAIR, t, LANES), F32),
                   jax.ShapeDtypeStruct((nc, 1, W_B), F32),
                   jax.ShapeDtypeStruct((m, W_B), F32), jax.ShapeDtypeStruct((m, W_B), F32)],
        compiler_params=_params(1),
        name="rwkv_chunks",
    )(zb, zb, *w_args)


def _rwkv_scan_kernel(x1_ref, x2_ref, ub_ref, op_ref, sp_ref, gam_ref, o_ref, st_ref, st_scr,
                      *, n_seq, n_chunks):
    ci = pl.program_id(0)
    t = 2 * CHUNK_B

    @pl.when(ci == 0)
    def _():
        st_scr[...] = jnp.zeros_like(st_scr)

    eye = _iota((t, t), 0) == _iota((t, t), 1)
    streams = [(b, p) for b in range(n_seq) for p in range(N_PAIR)]
    st = [st_scr[b * N_PAIR + p] for b, p in streams]
    y = [_dot(x1_ref[b, 0, p], s.astype(BF16)) for (b, p), s in zip(streams, st)]
    u = [ub_ref[b, 0, p] - yi[:t] for (b, p), yi in zip(streams, y)]
    z = [_dot(x2_ref[b, 0, p], ui.astype(BF16)) for (b, p), ui in zip(streams, u)]
    for i, (b, p) in enumerate(streams):
        o_s = op_ref[b, 0, p] + y[i][t:] + z[i][:t]
        gam_row = gam_ref[b, 0, :, p * LANES:(p + 1) * LANES]
        gam_col = jnp.sum(jnp.where(eye, gam_row, 0.0), axis=1, keepdims=True)
        st_scr[b * N_PAIR + p] = gam_col * st[i] + z[i][t:] + sp_ref[b, 0, p]
        o_ref[b, :, p * LANES:(p + 1) * LANES] = o_s[:CHUNK_B] + o_s[CHUNK_B:]

    @pl.when(ci == n_chunks - 1)
    def _():
        st_ref[...] = st_scr[...]


def _rwkv_scan(x1, x2, ub, op, sp, gam, n_seq, seq_len):
    nc = seq_len // CHUNK_B
    t = 2 * CHUNK_B
    r5 = lambda a: a.reshape((n_seq, nc) + a.shape[1:])
    blk5 = lambda rows: pl.BlockSpec((n_seq, 1, N_PAIR, rows, LANES), lambda c: (0, c, 0, 0, 0))
    return pl.pallas_call(
        functools.partial(_rwkv_scan_kernel, n_seq=n_seq, n_chunks=nc),
        grid=(nc,),
        in_specs=[blk5(2 * t), blk5(2 * t), blk5(t), blk5(t), blk5(t),
                  pl.BlockSpec((n_seq, 1, 1, W_B), lambda c: (0, c, 0, 0))],
        out_specs=[pl.BlockSpec((n_seq, CHUNK_B, W_B), lambda c: (0, c, 0)),
                   pl.BlockSpec((n_seq * N_PAIR, t, LANES), lambda c: (0, 0, 0))],
        out_shape=[jax.ShapeDtypeStruct((n_seq, seq_len, W_B), F32),
                   jax.ShapeDtypeStruct((n_seq * N_PAIR, t, LANES), F32)],
        scratch_shapes=[pltpu.VMEM((n_seq * N_PAIR, t, LANES), F32)],
        compiler_params=_params(1),
        name="rwkv_scan",
    )(r5(x1), r5(x2), r5(ub), r5(op), r5(sp), r5(gam))


STEP_SEQS = 8


def _rwkv_step_kernel(s_ref, r_ref, lw_ref, k_ref, kk_ref, beta_ref, vcol_ref, o_ref, so_ref):
    for b in range(s_ref.shape[0]):
        for h in range(H_B):
            s = s_ref[b, h]
            sk = jnp.sum(s * kk_ref[b, h], axis=1, keepdims=True)
            s_new = s * jnp.exp(lw_ref[b, h]) - sk * beta_ref[b, h] + vcol_ref[b, h] * k_ref[b, h]
            so_ref[b, h] = s_new
            o_ref[b, h] = jnp.sum(s_new * r_ref[b, h], axis=1, keepdims=True)


def _rwkv_step(state, r, lw, k2, kk, beta, v):
    n = state.shape[0]
    bs = STEP_SEQS if n % STEP_SEQS == 0 else 1
    rowv = lambda a: a.reshape(n, H_B, 1, GROUP)
    idx = lambda b: (b, 0, 0, 0)
    row_spec = pl.BlockSpec((bs, H_B, 1, GROUP), idx)
    col_spec = pl.BlockSpec((bs, H_B, GROUP, 1), idx)
    mat_spec = pl.BlockSpec((bs, H_B, GROUP, GROUP), idx)
    o, s_new = pl.pallas_call(
        _rwkv_step_kernel,
        grid=(n // bs,),
        in_specs=[mat_spec] + [row_spec] * 5 + [col_spec],
        out_specs=[col_spec, mat_spec],
        out_shape=[jax.ShapeDtypeStruct((n, H_B, GROUP, 1), F32),
                   jax.ShapeDtypeStruct((n, H_B, GROUP, GROUP), F32)],
        compiler_params=_params(1),
        name="rwkv_step",
    )(state, rowv(r), rowv(lw), rowv(k2), rowv(kk), rowv(beta), v.reshape(n, H_B, GROUP, 1))
    return o.reshape(n, W_B), s_new


BIAS_PIECES = 3


def _cumsum_kernel(lf_ref, place_ref, b_ref, carry):
    @pl.when(pl.program_id(1) == 0)
    def _():
        carry[...] = jnp.zeros_like(carry)

    tb = lf_ref.shape[0]
    tri = (_iota((tb, tb), 0) >= _iota((tb, tb), 1)).astype(F32)
    c = _dot_ones_left(tri, lf_ref[...]) + carry[...]
    carry[...] = c[tb - 1:tb, :]
    pieces = _bf16_pieces(-LOG2E * c)
    b_ref[...] = _dot(jnp.concatenate(pieces, axis=1), place_ref[...]).astype(BF16)


def _bias_placement():
    rows = jnp.arange(BIAS_PIECES * LANES)
    piece, head = rows // LANES, rows % LANES
    col = LANES * (head // 2) + jnp.where(head % 2 == 0, GROUP, 0) + piece
    hit = (col[:, None] == jnp.arange(W_C)[None, :]) & (head < H_C)[:, None]
    return hit.astype(BF16)


def _cumsum(lf, n_seq, seq_len, tb=512):
    nb = seq_len // tb
    return pl.pallas_call(
        _cumsum_kernel,
        grid=(n_seq, nb),
        in_specs=[pl.BlockSpec((tb, LANES), lambda b, j: (b * nb + j, 0)),
                  pl.BlockSpec((BIAS_PIECES * LANES, W_C), lambda b, j: (0, 0))],
        out_specs=pl.BlockSpec((tb, W_C), lambda b, j: (b * nb + j, 0)),
        out_shape=jax.ShapeDtypeStruct((n_seq * seq_len, W_C), BF16),
        scratch_shapes=[pltpu.VMEM((1, LANES), F32)],
        compiler_params=_params(2),
        name="logf_cumsum",
    )(lf, _bias_placement())


HEAD_ROWS = 16
DECODE_PAGES = 16


def _prompt_init(q_ref, qa_scr, m_scr, acc_scr):
    tq = q_ref.shape[0]
    lane = _iota((tq, LANES), 1)
    ones_hi = jnp.where(lane < GROUP + BIAS_PIECES, 1.0, 0.0).astype(BF16)
    ones_lo = jnp.where(lane < BIAS_PIECES, 1.0, 0.0).astype(BF16)
    for p in range(N_PAIR):
        q = q_ref[:, p * LANES:(p + 1) * LANES]
        qa_scr[2 * p] = jnp.where(lane < GROUP, q, ones_hi)
        qa_scr[2 * p + 1] = jnp.where(lane < GROUP, ones_lo, q)
    m_scr[...] = jnp.full_like(m_scr, NEG_BIG)
    acc_scr[...] = jnp.zeros_like(acc_scr)


def _prompt_step(k_ref, vt_ref, b_ref, qa_scr, m_scr, acc_scr, diagonal):
    tk = k_ref.shape[0]
    tq = qa_scr.shape[1]
    low_k = _iota((tk, LANES), 1) < GROUP
    low_v = _iota((LANES, tk), 0) < GROUP
    ka, va = [], []
    for p in range(N_PAIR):
        k = k_ref[:, p * LANES:(p + 1) * LANES]
        bias = b_ref[:, p * LANES:(p + 1) * LANES]
        vt = vt_ref[p * LANES:(p + 1) * LANES, :]
        one = jnp.ones_like(vt)
        ka += [jnp.where(low_k, k, bias), jnp.where(low_k, bias, k)]
        va += [jnp.where(low_v, vt, one), jnp.where(low_v, one, vt)]
    scores = [_dot_nt(ka[h], qa_scr[h]) for h in range(H_C)]
    for h in range(H_C):
        s = scores[h]
        if diagonal:
            s = jnp.where(_iota((tk, tq), 0) <= _iota((tk, tq), 1), s, NEG_BIG)
        m_prev = m_scr[h]
        m_new = jnp.maximum(m_prev, jnp.max(s, axis=0, keepdims=True))
        alpha = jnp.exp2(m_prev - m_new)
        pr = jnp.exp2(s - m_new).astype(BF16)
        acc_scr[h] = alpha * acc_scr[h] + _dot(va[h], pr)
        m_scr[h] = m_new


def _prompt_finish(o_ref, acc_scr):
    tq = o_ref.shape[0]
    low_row = _iota((LANES, tq), 0) < GROUP
    for p in range(N_PAIR):
        a0 = acc_scr[2 * p]
        a1 = acc_scr[2 * p + 1]
        out = jnp.where(low_row, a0 / a0[LANES - 1:LANES, :], a1 / a1[0:1, :])
        o_ref[:, p * LANES:(p + 1) * LANES] = out.T.astype(BF16)


def _own_head_mask():
    return (_iota((HEAD_ROWS, W_C), 1) // GROUP) == _iota((HEAD_ROWS, W_C), 0)


def _decode_init(q_ref, qf_scr, qb_scr, m_scr, l_scr, acc_scr, carry):
    qrows = jnp.where(_own_head_mask(), q_ref[0].astype(F32), 0.0)
    qf_scr[...] = qrows
    qb_scr[...] = qrows.astype(BF16)
    m_scr[...] = jnp.full_like(m_scr, NEG_BIG)
    l_scr[...] = jnp.zeros_like(l_scr)
    acc_scr[...] = jnp.zeros_like(acc_scr)
    carry[...] = jnp.zeros_like(carry)


def _decode_step(k_refs, v_refs, lf_refs, qb_scr, m_scr, l_scr, acc_scr, carry):
    pages = len(k_refs)
    page = k_refs[0].shape[1]
    upto = (_iota((page, page), 0) <= _iota((page, page), 1)).astype(F32)
    lf_all = jnp.concatenate([lf_refs[u][...] for u in range(pages)], axis=0)
    c_all = _dot_ones_right(lf_all, upto)
    totals = [c_all[8 * u:8 * (u + 1), page - 1:page] for u in range(pages)]
    run = carry[...]
    cts = []
    for u in range(pages):
        cts.append(c_all[8 * u:8 * (u + 1)] + run)
        run = run + totals[u]
    carry[...] = run
    ct = jnp.concatenate(cts, axis=1)
    ct = jnp.concatenate([ct, jnp.zeros_like(ct)], axis=0)
    kcat = jnp.concatenate([k_refs[u][...].astype(BF16) for u in range(pages)], axis=1)
    vcat = jnp.concatenate([v_refs[u][...].astype(BF16) for u in range(pages)], axis=1)
    s = _dot(qb_scr[...], kcat) - LOG2E * ct
    m_prev = m_scr[...]
    m_new = jnp.maximum(m_prev, jnp.max(s, axis=1, keepdims=True))
    alpha = jnp.exp2(m_prev - m_new)
    pr = jnp.exp2(s - m_new)
    l_scr[...] = alpha * l_scr[...] + jnp.sum(pr, axis=1, keepdims=True)
    acc_scr[...] = alpha * acc_scr[...] + _dot_nt(pr.astype(BF16), vcat)
    m_scr[...] = m_new


def _decode_finish(o_ref, kn_ref, vn_ref, lfn_ref, qf_scr, m_scr, l_scr, acc_scr, carry):
    c_past = jnp.concatenate([carry[...], jnp.zeros_like(carry)], axis=0)
    s_new = (jnp.sum(qf_scr[...] * kn_ref[0], axis=1, keepdims=True)
             - LOG2E * (c_past + lfn_ref[0]))
    m_prev = m_scr[...]
    m_new = jnp.maximum(m_prev, s_new)
    alpha = jnp.exp2(m_prev - m_new)
    pn = jnp.exp2(s_new - m_new)
    l_fin = alpha * l_scr[...] + pn
    acc = alpha * acc_scr[...] + pn * vn_ref[0]
    o_ref[0] = jnp.sum(jnp.where(_own_head_mask(), acc / l_fin, 0.0), axis=0, keepdims=True)


def _fox_kernel(qrow_ref, krow_ref, dseq_ref, pt_ref, q_ref, k_ref, vt_ref, b_ref, qs_ref, kn_ref, vn_ref,
                lfn_ref, *refs, nq, prompt_steps, decode_steps, steps_per_seq):
    pages = DECODE_PAGES
    k_refs = refs[:pages]
    v_refs = refs[pages:2 * pages]
    lf_refs = refs[2 * pages:3 * pages]
    (o_ref, os_ref, qa_scr, m_scr, acc_scr,
     qf_scr, qb_scr, dm_scr, dl_scr, dacc_scr, carry) = refs[3 * pages:]
    g = pl.program_id(0)
    i = qrow_ref[g] % nq
    j = krow_ref[g] % nq
    in_prompt = g < prompt_steps
    in_decode = g < decode_steps
    dstep = jnp.minimum(g, decode_steps - 1) % steps_per_seq
    dstate = (qb_scr, dm_scr, dl_scr, dacc_scr, carry)

    @pl.when(jnp.logical_and(in_decode, dstep == 0))
    def _():
        _decode_init(qs_ref, qf_scr, *dstate)

    @pl.when(jnp.logical_and(in_prompt, j == 0))
    def _():
        _prompt_init(q_ref, qa_scr, m_scr, acc_scr)

    @pl.when(jnp.logical_and(in_prompt, j < i))
    def _():
        _prompt_step(k_ref, vt_ref, b_ref, qa_scr, m_scr, acc_scr, False)
        _decode_step(k_refs, v_refs, lf_refs, *dstate)

    @pl.when(jnp.logical_and(in_prompt, j == i))
    def _():
        _prompt_step(k_ref, vt_ref, b_ref, qa_scr, m_scr, acc_scr, True)
        _decode_step(k_refs, v_refs, lf_refs, *dstate)
        _prompt_finish(o_ref, acc_scr)

    @pl.when(jnp.logical_not(in_prompt))
    def _():
        _decode_step(k_refs, v_refs, lf_refs, *dstate)

    @pl.when(jnp.logical_and(in_decode, dstep == steps_per_seq - 1))
    def _():
        _decode_finish(os_ref, kn_ref, vn_ref, lfn_ref, qf_scr, dm_scr, dl_scr, dacc_scr, carry)


def _fox(layer, q, k, vt, bias, n_seq, seq_len, q_s, k_new, v_new, lf_new, cache_k, cache_v, cache_lf,
         page_table, tq=512):
    nq = seq_len // tq
    pairs = [(i, j) for i in range(nq) for j in range(i + 1)]
    n_s, n_pages = page_table.shape
    page = cache_k.shape[3]
    pages = DECODE_PAGES
    steps_per_seq = n_pages // pages
    prompt_steps = n_seq * len(pairs)
    decode_steps = n_s * steps_per_seq
    n_steps = max(prompt_steps, decode_steps)
    lfn = jnp.pad(lf_new[:, :H_C], ((0, 0), (0, HEAD_ROWS - H_C))).reshape(n_s, HEAD_ROWS, 1)
    p_of = [min(g, prompt_steps - 1) for g in range(n_steps)]
    d_of = [min(g, decode_steps - 1) for g in range(n_steps)]
    qrow = jnp.asarray([(p // len(pairs)) * nq + pairs[p % len(pairs)][0] for p in p_of], jnp.int32)
    krow = jnp.asarray([(p // len(pairs)) * nq + pairs[p % len(pairs)][1] for p in p_of], jnp.int32)
    dseq = jnp.asarray([d // steps_per_seq for d in d_of], jnp.int32)
    step_pages = page_table.reshape(decode_steps, pages)[jnp.asarray(d_of, jnp.int32)]

    q_spec = pl.BlockSpec((tq, W_C), lambda g, qrow, krow, dseq, pt: (qrow[g], 0))
    k_spec = pl.BlockSpec((tq, W_C), lambda g, qrow, krow, dseq, pt: (krow[g], 0))
    vt_spec = pl.BlockSpec((W_C, tq), lambda g, qrow, krow, dseq, pt: (0, krow[g]))
    seq3 = lambda rows, w: pl.BlockSpec((1, rows, w), lambda g, qrow, krow, dseq, pt: (dseq[g], 0, 0))

    def paged(rows, u):
        return pl.BlockSpec((None, None, rows, page),
                            lambda g, qrow, krow, dseq, pt, u=u: (layer, pt[g, u], 0, 0))

    in_specs = ([q_spec, k_spec, vt_spec, k_spec,
                 seq3(1, W_C), seq3(1, W_C), seq3(1, W_C), seq3(HEAD_ROWS, 1)]
                + [paged(W_C, u) for u in range(pages)]
                + [paged(W_C, u) for u in range(pages)]
                + [paged(8, u) for u in range(pages)])
    yc, yc_s = pl.pallas_call(
        functools.partial(_fox_kernel, nq=nq, prompt_steps=prompt_steps, decode_steps=decode_steps,
                          steps_per_seq=steps_per_seq),
        grid_spec=pltpu.PrefetchScalarGridSpec(
            num_scalar_prefetch=4,
            grid=(n_steps,),
            in_specs=in_specs,
            out_specs=[q_spec, seq3(1, W_C)],
            scratch_shapes=[pltpu.VMEM((H_C, tq, LANES), BF16), pltpu.VMEM((H_C, 1, tq), F32),
                            pltpu.VMEM((H_C, LANES, tq), F32),
                            pltpu.VMEM((HEAD_ROWS, W_C), F32), pltpu.VMEM((HEAD_ROWS, W_C), BF16),
                            pltpu.VMEM((HEAD_ROWS, 1), F32), pltpu.VMEM((HEAD_ROWS, 1), F32),
                            pltpu.VMEM((HEAD_ROWS, W_C), F32), pltpu.VMEM((8, 1), F32)]),
        out_shape=[jax.ShapeDtypeStruct((n_seq * seq_len, W_C), BF16),
                   jax.ShapeDtypeStruct((n_s, 1, W_C), F32)],
        compiler_params=_params(1),
        name="fox",
    )(qrow, krow, dseq, step_pages, q, k, vt, bias,
      q_s.reshape(n_s, 1, W_C), k_new.reshape(n_s, 1, W_C), v_new.reshape(n_s, 1, W_C), lfn,
      *([cache_k] * pages), *([cache_v] * pages), *([cache_lf] * pages))
    return yc, yc_s.reshape(n_s, W_C)


def _outproj_kernel(x_ref, ya_ref, ob_ref, bonus_ref, g_ref, yc_ref, ln_ref, avg_ref, wo_ref, o_ref):
    ob = ob_ref[...]
    mu = _dot_split(ob, avg_ref[...])
    d = ob - mu
    var = _dot((d * d).astype(BF16), avg_ref[...])
    yb = (d * lax.rsqrt(var + GN_EPS) * ln_ref[0:1, :] + ln_ref[1:2, :] + bonus_ref[...]) * g_ref[...]
    acc = _dot(ya_ref[...], wo_ref[0:W_A, :])
    acc += _dot(yb.astype(BF16), wo_ref[W_A:W_A + W_B, :])
    acc += _dot(yc_ref[...], wo_ref[W_A + W_B:, :])
    o_ref[...] = x_ref[...] + acc


def _outproj(x, ya, ob, bonus, g, yc, ln, avg_b, wo, tm=512):
    m = x.shape[0]
    tm = min(tm, m)
    row = lambda i: (i, 0)
    fix = lambda i: (0, 0)
    return pl.pallas_call(
        _outproj_kernel,
        grid=(m // tm,),
        in_specs=[pl.BlockSpec((tm, D_MODEL), row), pl.BlockSpec((tm, W_A), row),
                  pl.BlockSpec((tm, W_B), row), pl.BlockSpec((tm, W_B), row),
                  pl.BlockSpec((tm, W_B), row), pl.BlockSpec((tm, W_C), row),
                  pl.BlockSpec((8, W_B), fix), pl.BlockSpec((W_B, W_B), fix),
                  pl.BlockSpec((D_MODEL, D_MODEL), fix)],
        out_specs=pl.BlockSpec((tm, D_MODEL), row),
        out_shape=jax.ShapeDtypeStruct((m, D_MODEL), F32),
        compiler_params=_params(1),
        name="outproj",
    )(x, ya, ob, bonus, g, yc, ln, avg_b, wo)


def _block_diag_const(width, value):
    idx = jnp.arange(width) // GROUP
    return jnp.where(idx[:, None] == idx[None, :], value, 0.0).astype(BF16)


def _pad_rows(vecs, width):
    rows = [jnp.pad(v, (0, width - v.shape[0])) for v in vecs]
    rows += [jnp.zeros((width,), F32)] * (8 - len(rows))
    return jnp.stack(rows)


def _layer_weights(l, norm_g, w_ffn_in, w_ffn_out, w_in, a_ws, a_bs, a_norm_g, b_mu, b_w0, b_wB,
                   b_a0, b_aB, b_gB, b_kk, b_ka, b_rk, b_ln_g, b_ln_b, c_fb, w_o):
    zeros_lora = jnp.zeros((R_DECAY, W_B), F32)
    return dict(
        norm_g=norm_g[l],
        w_ffn_in=w_ffn_in[l].astype(BF16),
        w_ffn_out=w_ffn_out[l].astype(BF16),
        w_in=jnp.pad(w_in[l], ((0, 0), (0, IN_PROJ_PAD - IN_PROJ))).astype(BF16),
        c_fb=jnp.pad(c_fb[l], (0, LANES - H_C)).reshape(1, LANES),
        a_ws=a_ws[l],
        a_bias=jnp.repeat(a_bs[l].T, GROUP, axis=1),
        a_norm_g=a_norm_g[l],
        b_vec=_pad_rows([b_mu[l], b_w0[l], b_a0[l], b_kk[l], b_ka[l], b_rk[l]], B_PROJ),
        b_wB=jnp.concatenate([b_wB[l], zeros_lora], axis=0).astype(BF16),
        b_aB=jnp.concatenate([zeros_lora, b_aB[l]], axis=0).astype(BF16),
        b_gB=b_gB[l].astype(BF16),
        b_ln=_pad_rows([b_ln_g[l], b_ln_b[l]], W_B),
        w_o=w_o[l].astype(BF16),
        ones_b=_block_diag_const(W_B, 1.0),
        avg_b=_block_diag_const(W_B, 1.0 / GROUP),
        avg_a=_block_diag_const(W_A, 1.0 / GROUP),
    )


def _mix_prompt(wts, z, n_seq, seq_len):
    za, zb = z[0], z[1]
    ya, _ = _gmlp(za, wts["a_norm_g"], wts["a_ws"], wts["a_bias"], wts["avg_a"])
    x1, x2, ub, op, sp, gam, g, bonus = _rwkv_chunks(zb, seq_len, wts)
    ob, st = _rwkv_scan(x1, x2, ub, op, sp, gam, n_seq, seq_len)
    st = st.reshape(n_seq, N_PAIR, 2, GROUP, 2, GROUP)
    wkv = jnp.stack([st[:, :, 0, :, 0, :], st[:, :, 1, :, 1, :]], axis=2)
    wkv = wkv.reshape(n_seq, H_B, GROUP, GROUP).transpose(0, 1, 3, 2)
    return ya, ob.reshape(n_seq * seq_len, W_B), bonus, g, wkv


def _mix_sample(wts, z, shift0, wkv0):
    za, zb = z[0], z[1]
    n = za.shape[0]
    za_pad = jnp.pad(za[:, None, :], ((0, 0), (0, CHUNK_A - 1), (0, 0))).reshape(n * CHUNK_A, A_PROJ)
    ya, va = _gmlp(za_pad, wts["a_norm_g"], wts["a_ws"], wts["a_bias"], wts["avg_a"])
    ya = ya.reshape(n, CHUNK_A, W_A)[:, 0]
    va = va.reshape(n, CHUNK_A, W_A)[:, 0]
    r, lw, k2, vb, kk, beta, g, bonus = _rwkv_prep(zb, shift0, wts)
    ob, wkv = _rwkv_step(wkv0, r, lw, k2, kk, beta, vb)
    return ya, ob, bonus, g, wkv, va


def kernel(x_prompt, x_sample, cache_k, cache_v, cache_logf, state_wkv, state_shift, page_table,
           norm_g, w_ffn_in, w_ffn_out, w_in, a_ws, a_bs, a_norm_g, b_mu, b_w0, b_wB, b_a0, b_aB,
           b_gB, b_kk, b_ka, b_rk, b_ln_g, b_ln_b, c_fb, w_o, final_norm):
    n_p, seq_len, _ = x_prompt.shape
    n_s = x_sample.shape[0]
    depth = norm_g.shape[0]
    n_phys, page = cache_k.shape[1], cache_k.shape[2]
    ck = jnp.transpose(cache_k, (0, 1, 3, 4, 2)).reshape(depth, n_phys, W_C, page)
    cv = jnp.transpose(cache_v, (0, 1, 3, 4, 2)).reshape(depth, n_phys, W_C, page)
    clf = jnp.pad(jnp.transpose(cache_logf, (0, 1, 3, 2)), ((0, 0), (0, 0), (0, 8 - H_C), (0, 0)))
    xp = x_prompt.reshape(n_p * seq_len, D_MODEL)
    xs = x_sample.reshape(n_s, D_MODEL)
    outs = {name: [] for name in ("kp", "vp", "lfp", "wkvp", "shp", "ks", "vs", "lfs", "wkvs", "shs", "va")}
    for l in range(depth):
        wts = _layer_weights(l, norm_g, w_ffn_in, w_ffn_out, w_in, a_ws, a_bs, a_norm_g, b_mu, b_w0,
                             b_wB, b_a0, b_aB, b_gB, b_kk, b_ka, b_rk, b_ln_g, b_ln_b, c_fb, w_o)
        last = l == depth - 1
        fin = final_norm if last else None

        xp = _ffn(xp, wts["norm_g"][0], wts["w_ffn_in"][0], wts["w_ffn_out"][0])
        xs = _ffn(xs, wts["norm_g"][0], wts["w_ffn_in"][0], wts["w_ffn_out"][0])
        zp = _inproj(xp, wts["norm_g"][1], wts["w_in"], wts["c_fb"], seq_len=seq_len)
        zs = _inproj(xs, wts["norm_g"][1], wts["w_in"], wts["c_fb"])
        ya, ob, bonus, g, wkvp = _mix_prompt(wts, zp, n_p, seq_len)
        ya_s, ob_s, bonus_s, g_s, wkvs, va = _mix_sample(wts, zs, state_shift[l], state_wkv[l])
        yc, yc_s = _fox(l, zp[2], zp[6], zp[7], _cumsum(zp[3], n_p, seq_len), n_p, seq_len,
                        zs[2], zs[4], zs[5], zs[3], ck, cv, clf, page_table)
        xp = _outproj(xp, ya, ob, bonus, g, yc, wts["b_ln"], wts["avg_b"], wts["w_o"])
        xs = _outproj(xs, ya_s, ob_s, bonus_s, g_s, yc_s.astype(BF16), wts["b_ln"], wts["avg_b"], wts["w_o"])
        xp = _ffn(xp, wts["norm_g"][2], wts["w_ffn_in"][1], wts["w_ffn_out"][1], final_g=fin)
        xs = _ffn(xs, wts["norm_g"][2], wts["w_ffn_in"][1], wts["w_ffn_out"][1], final_g=fin)
        outs["kp"].append(zp[4].reshape(n_p, H_C, GROUP, seq_len).transpose(0, 3, 1, 2))
        outs["vp"].append(zp[5].reshape(n_p, H_C, GROUP, seq_len).transpose(0, 3, 1, 2))
        outs["lfp"].append(zp[3][:, :H_C].reshape(n_p, seq_len, H_C))
        outs["wkvp"].append(wkvp)
        outs["shp"].append(zp[1].reshape(n_p, seq_len, B_PROJ)[:, -1])
        outs["ks"].append(zs[4].reshape(n_s, 1, H_C, GROUP))
        outs["vs"].append(zs[5].reshape(n_s, 1, H_C, GROUP))
        outs["lfs"].append(zs[3][:, :H_C].reshape(n_s, 1, H_C))
        outs["wkvs"].append(wkvs)
        outs["shs"].append(zs[1])
        outs["va"].append(va.reshape(n_s, 1, W_A))
    st = lambda name: jnp.stack(outs[name])
    return (xp.reshape(n_p, seq_len, D_MODEL), xs.reshape(n_s, 1, D_MODEL),
            st("kp"), st("vp"), st("lfp"), st("wkvp"), st("shp"),
            st("ks"), st("vs"), st("lfs"), st("wkvs"), st("shs"), st("va"))
```

```python
import functools

import jax
import jax.numpy as jnp
from jax import lax
from jax.experimental import pallas as pl
from jax.experimental.pallas import tpu as pltpu

F32 = jnp.float32
BF16 = jnp.bfloat16

LANES = 128
D_MODEL = 1024
D_FF = 2816
GROUP = 64
W_A = 256
W_B = 384
W_C = 384
H_B = W_B // GROUP
H_C = W_C // GROUP
N_PAIR = W_B // LANES
R_DECAY = 64
R_AAA = 64
R_GATE = 128
B_PROJ = 3 * W_B + R_DECAY + R_AAA + R_GATE
A_PROJ = 2 * W_A
C_PROJ = 3 * W_C + H_C
IN_PROJ = A_PROJ + B_PROJ + C_PROJ
IN_PROJ_PAD = A_PROJ + B_PROJ + 3 * W_C + LANES
CHUNK_A = 128
CHUNK_B = 64
NORM_EPS = 1e-6
GN_EPS = 64e-5
NEG_BIG = -1e30
LOG2E = 1.4426950408889634
Q_SCALE = LOG2E * GROUP ** -0.5
VMEM_LIMIT = 56 << 20


def _params(n_axes, vmem=VMEM_LIMIT):
    return pltpu.CompilerParams(dimension_semantics=("arbitrary",) * n_axes,
                                vmem_limit_bytes=vmem)


def _sigmoid(x):
    return 1.0 / (1.0 + jnp.exp(-x))


def _softplus(x):
    return jnp.maximum(x, 0.0) + jnp.log(1.0 + jnp.exp(-jnp.abs(x)))


def _gelu_tanh(x):
    return 0.5 * x * (1.0 + jnp.tanh(0.7978845608028654 * (x + 0.044715 * (x * x * x))))


def _rms(x, g):
    return x * lax.rsqrt(jnp.mean(x * x, axis=-1, keepdims=True) + NORM_EPS) * g


def _dot(a, b):
    return jnp.dot(a, b, preferred_element_type=F32)


def _dot_nt(a, b):
    return lax.dot_general(a, b, (((1,), (1,)), ((), ())), preferred_element_type=F32)


def _dot_split(a, b_bf):
    hi = a.astype(BF16)
    lo = (a - hi.astype(F32)).astype(BF16)
    return _dot(hi, b_bf) + _dot(lo, b_bf)


def _bf16_pieces(x):
    pieces = []
    for _ in range(3):
        piece = x.astype(BF16)
        pieces.append(piece)
        x = x - piece.astype(F32)
    return pieces


def _dot_ones_left(ones, x):
    n = x.shape[1]
    y = _dot(ones.astype(BF16), jnp.concatenate(_bf16_pieces(x), axis=1))
    return y[:, :n] + y[:, n:2 * n] + y[:, 2 * n:]


def _dot_ones_right(x, ones):
    m = x.shape[0]
    y = _dot(jnp.concatenate(_bf16_pieces(x), axis=0), ones.astype(BF16))
    return y[:m] + y[m:2 * m] + y[2 * m:]


def _iota(shape, dim):
    return lax.broadcasted_iota(jnp.int32, shape, dim)


MXU_TILE = 256
FF_SLICES = (6 * MXU_TILE, 5 * MXU_TILE)
assert sum(FF_SLICES) == D_FF


def _ffn_kernel(*refs, final):
    if final:
        x_ref, g_ref, wi_ref, wo_ref, fg_ref, o_ref = refs
    else:
        x_ref, g_ref, wi_ref, wo_ref, o_ref = refs
    x = x_ref[...]
    h = _rms(x, g_ref[...]).astype(BF16)
    acc = None
    lo = 0
    for width in FF_SLICES:
        gate = _dot(h, wi_ref[:, lo:lo + width])
        up = _dot(h, wi_ref[:, D_FF + lo:D_FF + lo + width])
        act = (gate * _sigmoid(gate) * up).astype(BF16)
        part = _dot(act, wo_ref[lo:lo + width, :])
        acc = part if acc is None else acc + part
        lo += width
    y = x + 0.5 * acc
    if final:
        y = _rms(y, fg_ref[...])
    o_ref[...] = y


def _ffn(x, g, w_in, w_out, final_g=None, tm=512):
    m = x.shape[0]
    tm = min(tm, m)
    final = final_g is not None
    fix = lambda i: (0, 0)
    resident = pl.Buffered(1)
    in_specs = [
        pl.BlockSpec((tm, D_MODEL), lambda i: (i, 0)),
        pl.BlockSpec((1, D_MODEL), fix),
        pl.BlockSpec((D_MODEL, 2 * D_FF), fix, pipeline_mode=resident),
        pl.BlockSpec((D_FF, D_MODEL), fix, pipeline_mode=resident),
    ]
    args = [x, g.reshape(1, D_MODEL), w_in, w_out]
    if final:
        in_specs.append(pl.BlockSpec((1, D_MODEL), fix))
        args.append(final_g.reshape(1, D_MODEL))
    return pl.pallas_call(
        functools.partial(_ffn_kernel, final=final),
        grid=(m // tm,),
        in_specs=in_specs,
        out_specs=pl.BlockSpec((tm, D_MODEL), lambda i: (i, 0)),
        out_shape=jax.ShapeDtypeStruct((m, D_MODEL), F32),
        compiler_params=_params(1),
        name="ffn",
    )(*args)


def _inproj_kernel(x_ref, g_ref, w_ref, fb_ref, za_ref, zb_ref, q_ref, lf_ref, *kv_refs,
                   channel_major):
    h = _rms(x_ref[...], g_ref[...]).astype(BF16)
    z = _dot(h, w_ref[...])
    o = A_PROJ
    za_ref[...] = z[:, :o]
    zb_ref[...] = z[:, o:o + B_PROJ]
    o += B_PROJ
    q_ref[...] = (z[:, o:o + W_C] * Q_SCALE).astype(BF16)
    k = z[:, o + W_C:o + 2 * W_C]
    v = z[:, o + 2 * W_C:o + 3 * W_C]
    lf_ref[...] = -_softplus(-(z[:, o + 3 * W_C:] + fb_ref[...]))
    if channel_major:
        kt_ref, vt_ref, kb_ref, vtb_ref = kv_refs
        kt_ref[...] = k.T
        vt = v.T
        vt_ref[...] = vt
        vtb_ref[...] = vt.astype(BF16)
        kb_ref[...] = k.astype(BF16)
    else:
        k_ref, v_ref = kv_refs
        k_ref[...] = k
        v_ref[...] = v


def _inproj(x, g, w_pad, fb_pad, seq_len=None, tm=512):
    m = x.shape[0]
    tm = min(tm, m)
    row = lambda i: (i, 0)
    fix = lambda i: (0, 0)
    widths = (A_PROJ, B_PROJ, W_C, LANES)
    dtypes = (F32, F32, BF16, F32)
    out_specs = [pl.BlockSpec((tm, w), row) for w in widths]
    out_shape = [jax.ShapeDtypeStruct((m, w), d) for w, d in zip(widths, dtypes)]
    if seq_len is None:
        out_specs += [pl.BlockSpec((tm, W_C), row)] * 2
        out_shape += [jax.ShapeDtypeStruct((m, W_C), F32)] * 2
    else:
        bps = seq_len // tm
        seq_blk = pl.BlockSpec((None, W_C, tm), lambda i: (i // bps, 0, i % bps))
        out_specs += [seq_blk, seq_blk, pl.BlockSpec((tm, W_C), row), pl.BlockSpec((W_C, tm), lambda i: (0, i))]
        out_shape += [jax.ShapeDtypeStruct((m // seq_len, W_C, seq_len), F32)] * 2
        out_shape += [jax.ShapeDtypeStruct((m, W_C), BF16), jax.ShapeDtypeStruct((W_C, m), BF16)]
    return pl.pallas_call(
        functools.partial(_inproj_kernel, channel_major=seq_len is not None),
        grid=(m // tm,),
        in_specs=[pl.BlockSpec((tm, D_MODEL), row), pl.BlockSpec((1, D_MODEL), fix),
                  pl.BlockSpec((D_MODEL, IN_PROJ_PAD), fix),
                  pl.BlockSpec((1, LANES), fix)],
        out_specs=out_specs,
        out_shape=out_shape,
        compiler_params=_params(1),
        name="inproj",
    )(x, g.reshape(1, D_MODEL), w_pad, fb_pad)


def _gmlp_kernel(za_ref, gain_ref, ws_ref, bias_ref, avg_ref, ya_ref, va_ref, *, n_chunks):
    z = _gelu_tanh(za_ref[...])
    u = z[:, :W_A]
    v = z[:, W_A:]
    ms = _dot((v * v).astype(BF16), avg_ref[...])
    vn = v * lax.rsqrt(ms + NORM_EPS) * gain_ref[...]
    va_ref[...] = vn
    causal = _iota((CHUNK_A, CHUNK_A), 0) >= _iota((CHUNK_A, CHUNK_A), 1)
    lane_group = _iota((CHUNK_A, W_A), 1) // GROUP
    wm = [jnp.where(causal, ws_ref[g], 0.0).astype(BF16) for g in range(W_A // GROUP)]
    for c in range(n_chunks):
        rows = slice(c * CHUNK_A, (c + 1) * CHUNK_A)
        vc = vn[rows]
        s = bias_ref[...]
        for g in range(W_A // GROUP):
            s = s + _dot(wm[g], jnp.where(lane_group == g, vc, 0.0).astype(BF16))
        ya_ref[rows, :] = (u[rows] * s).astype(BF16)


def _gmlp(za, gain, ws, bias_full, avg_a, tm=512):
    m = za.shape[0]
    tm = min(tm, m)
    row = lambda i: (i, 0)
    fix = lambda i: (0, 0)
    return pl.pallas_call(
        functools.partial(_gmlp_kernel, n_chunks=tm // CHUNK_A),
        grid=(m // tm,),
        in_specs=[pl.BlockSpec((tm, A_PROJ), row), pl.BlockSpec((1, W_A), fix),
                  pl.BlockSpec((W_A // GROUP, CHUNK_A, CHUNK_A), lambda i: (0, 0, 0)),
                  pl.BlockSpec((CHUNK_A, W_A), fix), pl.BlockSpec((W_A, W_A), fix)],
        out_specs=[pl.BlockSpec((tm, W_A), row), pl.BlockSpec((tm, W_A), row)],
        out_shape=[jax.ShapeDtypeStruct((m, W_A), BF16), jax.ShapeDtypeStruct((m, W_A), F32)],
        compiler_params=_params(1),
        name="gmlp",
    )(za, gain.reshape(1, W_A), ws, bias_full, avg_a)


def _rwkv_prep_math(zb, prev, vec_ref, wb_ref, ab_ref, gb_ref, ones_ref):
    mu = vec_ref[0:1, :]
    zs = zb + (prev - zb) * mu
    r = zs[:, :W_B]
    k = zs[:, W_B:2 * W_B]
    v = zs[:, 2 * W_B:3 * W_B]
    lora = zs[:, 3 * W_B:3 * W_B + LANES]
    gl = zs[:, 3 * W_B + LANES:]
    w0 = vec_ref[1:2, :W_B]
    a0 = vec_ref[2:3, :W_B]
    kkw = vec_ref[3:4, :W_B]
    kaw = vec_ref[4:5, :W_B]
    rkw = vec_ref[5:6, :W_B]
    w = -_softplus(-(w0 + _dot(jnp.tanh(lora).astype(BF16), wb_ref[...]))) - 0.5
    a = _sigmoid(a0 + _dot(lora.astype(BF16), ab_ref[...]))
    g = _dot(_sigmoid(gl).astype(BF16), gb_ref[...])
    kk = k * kkw
    ss = _dot_split(kk * kk, ones_ref[...])
    kk = kk / jnp.maximum(jnp.sqrt(ss), 1e-12)
    k2 = k * (1.0 + (a - 1.0) * kaw)
    lw = -jnp.exp(w)
    bonus = _dot_split(r * k2 * rkw, ones_ref[...]) * v
    return r, lw, k2, v, kk, kk * a, g, bonus


def _rwkv_prep_specs(wts):
    fix = lambda i: (0, 0)
    specs = [pl.BlockSpec((8, B_PROJ), fix), pl.BlockSpec((LANES, W_B), fix),
             pl.BlockSpec((LANES, W_B), fix), pl.BlockSpec((R_GATE, W_B), fix),
             pl.BlockSpec((W_B, W_B), fix)]
    return specs, [wts["b_vec"], wts["b_wB"], wts["b_aB"], wts["b_gB"], wts["ones_b"]]


def _rwkv_prep_tok_kernel(zb_ref, prev_ref, vec_ref, wb_ref, ab_ref, gb_ref, ones_ref, *outs):
    vals = _rwkv_prep_math(zb_ref[...], prev_ref[...], vec_ref, wb_ref, ab_ref, gb_ref, ones_ref)
    for ref, val in zip(outs, vals):
        ref[...] = val


def _rwkv_prep(zb, prev, wts):
    m = zb.shape[0]
    row = lambda i: (i, 0)
    w_specs, w_args = _rwkv_prep_specs(wts)
    return pl.pallas_call(
        _rwkv_prep_tok_kernel,
        grid=(1,),
        in_specs=[pl.BlockSpec((m, B_PROJ), row), pl.BlockSpec((m, B_PROJ), row)] + w_specs,
        out_specs=[pl.BlockSpec((m, W_B), row)] * 8,
        out_shape=[jax.ShapeDtypeStruct((m, W_B), F32)] * 8,
        compiler_params=_params(1),
        name="rwkv_prep",
    )(zb, prev, *w_args)


def _stack(x, low):
    return jnp.concatenate([jnp.where(low, x, 0.0), jnp.where(low, 0.0, x)], axis=0)


CHUNKS_PER_STEP = 4

def _rwkv_chunk_kernel(zb_ref, pb_ref, vec_ref, wb_ref, ab_ref, gb_ref, ones_ref,
                       x1_ref, x2_ref, ub_ref, op_ref, sp_ref, gam_ref, g_ref, bonus_ref, *, blocks_per_seq):
    c = CHUNK_B
    n2 = 2 * c
    rows = CHUNKS_PER_STEP * c
    zb = zb_ref[...]
    first = (pl.program_id(0) % blocks_per_seq) == 0
    last_prev = jnp.where(first, 0.0, pb_ref[7:8, :])
    prev = jnp.where(_iota((rows, 1), 0) == 0, last_prev, pltpu.roll(zb, shift=1, axis=0))
    r, lw, k2, v, kk, beta, g, bonus = _rwkv_prep_math(zb, prev, vec_ref, wb_ref, ab_ref, gb_ref, ones_ref)
    g_ref[...] = g
    bonus_ref[...] = bonus
    ri = _iota((rows, rows), 0)
    ci = _iota((rows, rows), 1)
    tri = jnp.where(ri >= ci, 1.0, 0.0) * jnp.where((ri // c) == (ci // c), 1.0, 0.0)
    cum = _dot_ones_left(tri, lw)
    lasts = [cum[(j + 1) * c - 1:(j + 1) * c, :] for j in range(CHUNKS_PER_STEP)]
    for j in range(CHUNKS_PER_STEP):
        gam_ref[j] = jnp.exp(lasts[j])
    cum_last = jnp.concatenate([jnp.broadcast_to(l, (c, W_B)) for l in lasts], axis=0)
    e_pos = jnp.exp(cum)
    e_neg = jnp.exp(-cum)
    e_tail = jnp.exp(cum_last - cum)
    r_t = r * e_pos
    kap_t = kk * jnp.exp(cum - lw)
    beta_h = beta * e_neg
    k_h = k2 * e_neg
    beta_c = beta * e_tail
    k_c = k2 * e_tail

    low = _iota((c, LANES), 1) < GROUP
    rr = _iota((n2, n2), 0) & (c - 1)
    cc = _iota((n2, n2), 1) & (c - 1)
    strict = rr > cc
    incl = rr >= cc
    streams = [(j, p) for j in range(CHUNKS_PER_STEP) for p in range(N_PAIR)]

    def tile(x, j, p):
        return _stack(x[j * c:(j + 1) * c, p * LANES:(p + 1) * LANES], low)

    kap_s = [tile(kap_t, j, p) for j, p in streams]
    r_s = [tile(r_t, j, p) for j, p in streams]
    v_s = [tile(v, j, p).astype(BF16) for j, p in streams]
    gram = [_dot_nt(jnp.concatenate([kap_s[i], r_s[i]], axis=0).astype(BF16),
                    jnp.concatenate([tile(beta_h, j, p), tile(k_h, j, p)], axis=0).astype(BF16))
            for i, (j, p) in enumerate(streams)]
    n_bf = [jnp.where(strict, g[:n2, :n2], 0.0).astype(BF16) for g in gram]
    av = [_dot(jnp.where(strict, g[:n2, n2:], 0.0).astype(BF16), vs) for g, vs in zip(gram, v_s)]
    for i, (j, p) in enumerate(streams):
        op_ref[j, p] = _dot(jnp.where(incl, gram[i][n2:, n2:], 0.0).astype(BF16), v_s[i])
        sp_ref[j, p] = _dot(tile(k_c, j, p).T.astype(BF16), v_s[i])
        x2_ref[j, p] = jnp.concatenate([jnp.where(incl, gram[i][n2:, :n2], 0.0),
                                        tile(beta_c, j, p).T], axis=0).astype(BF16)
    x = [jnp.concatenate([ks, -a], axis=1) for ks, a in zip(kap_s, av)]
    x = [xi - _dot(nb, xi.astype(BF16)) for xi, nb in zip(x, n_bf)]
    pw = n_bf
    for _ in range(5):
        pw = [_dot(q, q).astype(BF16) for q in pw]
        x = [xi + _dot(q, xi.astype(BF16)) for xi, q in zip(x, pw)]
    for i, (j, p) in enumerate(streams):
        x1_ref[j, p] = jnp.concatenate([x[i][:, :LANES], r_s[i]], axis=0).astype(BF16)
        ub_ref[j, p] = x[i][:, LANES:]


def _rwkv_chunks(zb, seq_len, wts):
    m = zb.shape[0]
    nc = m // CHUNK_B
    cb = CHUNKS_PER_STEP
    tm = cb * CHUNK_B
    row = lambda i: (i, 0)
    blk = lambda i: (i, 0, 0, 0)
    t = 2 * CHUNK_B
    w_specs, w_args = _rwkv_prep_specs(wts)
    return pl.pallas_call(
        functools.partial(_rwkv_chunk_kernel, blocks_per_seq=seq_len // tm),
        grid=(nc // cb,),
        in_specs=[pl.BlockSpec((tm, B_PROJ), row),
                  pl.BlockSpec((8, B_PROJ), lambda i: (jnp.maximum(i * (tm // 8) - 1, 0), 0))] + w_specs,
        out_specs=[pl.BlockSpec((cb, N_PAIR, 2 * t, LANES), blk),
                   pl.BlockSpec((cb, N_PAIR, 2 * t, LANES), blk),
                   pl.BlockSpec((cb, N_PAIR, t, LANES), blk),
                   pl.BlockSpec((cb, N_PAIR, t, LANES), blk),
                   pl.BlockSpec((cb, N_PAIR, t, LANES), blk),
                   pl.BlockSpec((cb, 1, W_B), lambda i: (i, 0, 0)),
                   pl.BlockSpec((tm, W_B), row), pl.BlockSpec((tm, W_B), row)],
        out_shape=[jax.ShapeDtypeStruct((nc, N_PAIR, 2 * t, LANES), BF16),
                   jax.ShapeDtypeStruct((nc, N_PAIR, 2 * t, LANES), BF16),
                   jax.ShapeDtypeStruct((nc, N_PAIR, t, LANES), F32),
                   jax.ShapeDtypeStruct((nc, N_PAIR, t, LANES), F32),
                   jax.ShapeDtypeStruct((nc, N_PAIR, t, LANES), F32),
                   jax.ShapeDtypeStruct((nc, 1, W_B), F32),
                   jax.ShapeDtypeStruct((m, W_B), F32), jax.ShapeDtypeStruct((m, W_B), F32)],
        compiler_params=_params(1),
        name="rwkv_chunks",
    )(zb, zb, *w_args)


def _rwkv_scan_kernel(x1_ref, x2_ref, ub_ref, op_ref, sp_ref, gam_ref, o_ref, st_ref, st_scr,
                      *, n_seq, n_chunks):
    ci = pl.program_id(0)
    t = 2 * CHUNK_B

    @pl.when(ci == 0)
    def _():
        st_scr[...] = jnp.zeros_like(st_scr)

    eye = _iota((t, t), 0) == _iota((t, t), 1)
    streams = [(b, p) for b in range(n_seq) for p in range(N_PAIR)]
    st = [st_scr[b * N_PAIR + p] for b, p in streams]
    y = [_dot(x1_ref[b, 0, p], s.astype(BF16)) for (b, p), s in zip(streams, st)]
    u = [ub_ref[b, 0, p] - yi[:t] for (b, p), yi in zip(streams, y)]
    z = [_dot(x2_ref[b, 0, p], ui.astype(BF16)) for (b, p), ui in zip(streams, u)]
    for i, (b, p) in enumerate(streams):
        o_s = op_ref[b, 0, p] + y[i][t:] + z[i][:t]
        gam_row = gam_ref[b, 0, :, p * LANES:(p + 1) * LANES]
        gam_col = jnp.sum(jnp.where(eye, gam_row, 0.0), axis=1, keepdims=True)
        st_scr[b * N_PAIR + p] = gam_col * st[i] + z[i][t:] + sp_ref[b, 0, p]
        o_ref[b, :, p * LANES:(p + 1) * LANES] = o_s[:CHUNK_B] + o_s[CHUNK_B:]

    @pl.when(ci == n_chunks - 1)
    def _():
        st_ref[...] = st_scr[...]


def _rwkv_scan(x1, x2, ub, op, sp, gam, n_seq, seq_len):
    nc = seq_len // CHUNK_B
    t = 2 * CHUNK_B
    r5 = lambda a: a.reshape((n_seq, nc) + a.shape[1:])
    blk5 = lambda rows: pl.BlockSpec((n_seq, 1, N_PAIR, rows, LANES), lambda c: (0, c, 0, 0, 0))
    return pl.pallas_call(
        functools.partial(_rwkv_scan_kernel, n_seq=n_seq, n_chunks=nc),
        grid=(nc,),
        in_specs=[blk5(2 * t), blk5(2 * t), blk5(t), blk5(t), blk5(t),
                  pl.BlockSpec((n_seq, 1, 1, W_B), lambda c: (0, c, 0, 0))],
        out_specs=[pl.BlockSpec((n_seq, CHUNK_B, W_B), lambda c: (0, c, 0)),
                   pl.BlockSpec((n_seq * N_PAIR, t, LANES), lambda c: (0, 0, 0))],
        out_shape=[jax.ShapeDtypeStruct((n_seq, seq_len, W_B), F32),
                   jax.ShapeDtypeStruct((n_seq * N_PAIR, t, LANES), F32)],
        scratch_shapes=[pltpu.VMEM((n_seq * N_PAIR, t, LANES), F32)],
        compiler_params=_params(1),
        name="rwkv_scan",
    )(r5(x1), r5(x2), r5(ub), r5(op), r5(sp), r5(gam))


STEP_SEQS = 8


def _rwkv_step_kernel(s_ref, r_ref, lw_ref, k_ref, kk_ref, beta_ref, vcol_ref, o_ref, so_ref):
    for b in range(s_ref.shape[0]):
        for h in range(H_B):
            s = s_ref[b, h]
            sk = jnp.sum(s * kk_ref[b, h], axis=1, keepdims=True)
            s_new = s * jnp.exp(lw_ref[b, h]) - sk * beta_ref[b, h] + vcol_ref[b, h] * k_ref[b, h]
            so_ref[b, h] = s_new
            o_ref[b, h] = jnp.sum(s_new * r_ref[b, h], axis=1, keepdims=True)


def _rwkv_step(state, r, lw, k2, kk, beta, v):
    n = state.shape[0]
    bs = STEP_SEQS if n % STEP_SEQS == 0 else 1
    rowv = lambda a: a.reshape(n, H_B, 1, GROUP)
    idx = lambda b: (b, 0, 0, 0)
    row_spec = pl.BlockSpec((bs, H_B, 1, GROUP), idx)
    col_spec = pl.BlockSpec((bs, H_B, GROUP, 1), idx)
    mat_spec = pl.BlockSpec((bs, H_B, GROUP, GROUP), idx)
    o, s_new = pl.pallas_call(
        _rwkv_step_kernel,
        grid=(n // bs,),
        in_specs=[mat_spec] + [row_spec] * 5 + [col_spec],
        out_specs=[col_spec, mat_spec],
        out_shape=[jax.ShapeDtypeStruct((n, H_B, GROUP, 1), F32),
                   jax.ShapeDtypeStruct((n, H_B, GROUP, GROUP), F32)],
        compiler_params=_params(1),
        name="rwkv_step",
    )(state, rowv(r), rowv(lw), rowv(k2), rowv(kk), rowv(beta), v.reshape(n, H_B, GROUP, 1))
    return o.reshape(n, W_B), s_new


BIAS_PIECES = 3


def _cumsum_kernel(lf_ref, place_ref, b_ref, carry):
    @pl.when(pl.program_id(1) == 0)
    def _():
        carry[...] = jnp.zeros_like(carry)

    tb = lf_ref.shape[0]
    tri = (_iota((tb, tb), 0) >= _iota((tb, tb), 1)).astype(F32)
    c = _dot_ones_left(tri, lf_ref[...]) + carry[...]
    carry[...] = c[tb - 1:tb, :]
    pieces = _bf16_pieces(-LOG2E * c)
    b_ref[...] = _dot(jnp.concatenate(pieces, axis=1), place_ref[...]).astype(BF16)


def _bias_placement():
    rows = jnp.arange(BIAS_PIECES * LANES)
    piece, head = rows // LANES, rows % LANES
    col = LANES * (head // 2) + jnp.where(head % 2 == 0, GROUP, 0) + piece
    hit = (col[:, None] == jnp.arange(W_C)[None, :]) & (head < H_C)[:, None]
    return hit.astype(BF16)


def _cumsum(lf, n_seq, seq_len, tb=512):
    nb = seq_len // tb
    return pl.pallas_call(
        _cumsum_kernel,
        grid=(n_seq, nb),
        in_specs=[pl.BlockSpec((tb, LANES), lambda b, j: (b * nb + j, 0)),
                  pl.BlockSpec((BIAS_PIECES * LANES, W_C), lambda b, j: (0, 0))],
        out_specs=pl.BlockSpec((tb, W_C), lambda b, j: (b * nb + j, 0)),
        out_shape=jax.ShapeDtypeStruct((n_seq * seq_len, W_C), BF16),
        scratch_shapes=[pltpu.VMEM((1, LANES), F32)],
        compiler_params=_params(2),
        name="logf_cumsum",
    )(lf, _bias_placement())


HEAD_ROWS = 16
DECODE_PAGES = 16


def _prompt_init(q_ref, qa_scr, m_scr, acc_scr):
    tq = q_ref.shape[0]
    lane = _iota((tq, LANES), 1)
    ones_hi = jnp.where(lane < GROUP + BIAS_PIECES, 1.0, 0.0).astype(BF16)
    ones_lo = jnp.where(lane < BIAS_PIECES, 1.0, 0.0).astype(BF16)
    for p in range(N_PAIR):
        q = q_ref[:, p * LANES:(p + 1) * LANES]
        qa_scr[2 * p] = jnp.where(lane < GROUP, q, ones_hi)
        qa_scr[2 * p + 1] = jnp.where(lane < GROUP, ones_lo, q)
    m_scr[...] = jnp.full_like(m_scr, NEG_BIG)
    acc_scr[...] = jnp.zeros_like(acc_scr)


def _prompt_step(k_ref, vt_ref, b_ref, qa_scr, m_scr, acc_scr, diagonal):
    tk = k_ref.shape[0]
    tq = qa_scr.shape[1]
    low_k = _iota((tk, LANES), 1) < GROUP
    low_v = _iota((LANES, tk), 0) < GROUP
    ka, va = [], []
    for p in range(N_PAIR):
        k = k_ref[:, p * LANES:(p + 1) * LANES]
        bias = b_ref[:, p * LANES:(p + 1) * LANES]
        vt = vt_ref[p * LANES:(p + 1) * LANES, :]
        one = jnp.ones_like(vt)
        ka += [jnp.where(low_k, k, bias), jnp.where(low_k, bias, k)]
        va += [jnp.where(low_v, vt, one), jnp.where(low_v, one, vt)]
    scores = [_dot_nt(ka[h], qa_scr[h]) for h in range(H_C)]
    for h in range(H_C):
        s = scores[h]
        if diagonal:
            s = jnp.where(_iota((tk, tq), 0) <= _iota((tk, tq), 1), s, NEG_BIG)
        m_prev = m_scr[h]
        m_new = jnp.maximum(m_prev, jnp.max(s, axis=0, keepdims=True))
        alpha = jnp.exp2(m_prev - m_new)
        pr = jnp.exp2(s - m_new).astype(BF16)
        acc_scr[h] = alpha * acc_scr[h] + _dot(va[h], pr)
        m_scr[h] = m_new


def _prompt_finish(o_ref, acc_scr):
    tq = o_ref.shape[0]
    low_row = _iota((LANES, tq), 0) < GROUP
    for p in range(N_PAIR):
        a0 = acc_scr[2 * p]
        a1 = acc_scr[2 * p + 1]
        out = jnp.where(low_row, a0 / a0[LANES - 1:LANES, :], a1 / a1[0:1, :])
        o_ref[:, p * LANES:(p + 1) * LANES] = out.T.astype(BF16)


def _own_head_mask():
    return (_iota((HEAD_ROWS, W_C), 1) // GROUP) == _iota((HEAD_ROWS, W_C), 0)


def _decode_init(q_ref, qf_scr, qb_scr, m_scr, l_scr, acc_scr, carry):
    qrows = jnp.where(_own_head_mask(), q_ref[0].astype(F32), 0.0)
    qf_scr[...] = qrows
    qb_scr[...] = qrows.astype(BF16)
    m_scr[...] = jnp.full_like(m_scr, NEG_BIG)
    l_scr[...] = jnp.zeros_like(l_scr)
    acc_scr[...] = jnp.zeros_like(acc_scr)
    carry[...] = jnp.zeros_like(carry)


def _decode_step(k_refs, v_refs, lf_refs, qb_scr, m_scr, l_scr, acc_scr, carry):
    pages = len(k_refs)
    page = k_refs[0].shape[1]
    upto = (_iota((page, page), 0) <= _iota((page, page), 1)).astype(F32)
    lf_all = jnp.concatenate([lf_refs[u][...] for u in range(pages)], axis=0)
    c_all = _dot_ones_right(lf_all, upto)
    totals = [c_all[8 * u:8 * (u + 1), page - 1:page] for u in range(pages)]
    run = carry[...]
    cts = []
    for u in range(pages):
        cts.append(c_all[8 * u:8 * (u + 1)] + run)
        run = run + totals[u]
    carry[...] = run
    ct = jnp.concatenate(cts, axis=1)
    ct = jnp.concatenate([ct, jnp.zeros_like(ct)], axis=0)
    kcat = jnp.concatenate([k_refs[u][...].astype(BF16) for u in range(pages)], axis=1)
    vcat = jnp.concatenate([v_refs[u][...].astype(BF16) for u in range(pages)], axis=1)
    s = _dot(qb_scr[...], kcat) - LOG2E * ct
    m_prev = m_scr[...]
    m_new = jnp.maximum(m_prev, jnp.max(s, axis=1, keepdims=True))
    alpha = jnp.exp2(m_prev - m_new)
    pr = jnp.exp2(s - m_new)
    l_scr[...] = alpha * l_scr[...] + jnp.sum(pr, axis=1, keepdims=True)
    acc_scr[...] = alpha * acc_scr[...] + _dot_nt(pr.astype(BF16), vcat)
    m_scr[...] = m_new


def _decode_finish(o_ref, kn_ref, vn_ref, lfn_ref, qf_scr, m_scr, l_scr, acc_scr, carry):
    c_past = jnp.concatenate([carry[...], jnp.zeros_like(carry)], axis=0)
    s_new = (jnp.sum(qf_scr[...] * kn_ref[0], axis=1, keepdims=True)
             - LOG2E * (c_past + lfn_ref[0]))
    m_prev = m_scr[...]
    m_new = jnp.maximum(m_prev, s_new)
    alpha = jnp.exp2(m_prev - m_new)
    pn = jnp.exp2(s_new - m_new)
    l_fin = alpha * l_scr[...] + pn
    acc = alpha * acc_scr[...] + pn * vn_ref[0]
    o_ref[0] = jnp.sum(jnp.where(_own_head_mask(), acc / l_fin, 0.0), axis=0, keepdims=True)


def _fox_kernel(qrow_ref, krow_ref, dseq_ref, pt_ref, q_ref, k_ref, vt_ref, b_ref, qs_ref, kn_ref, vn_ref,
                lfn_ref, ck_ref, cv_ref, clf_ref, o_ref, os_ref, qa_scr, m_scr, acc_scr,
                qf_scr, qb_scr, dm_scr, dl_scr, dacc_scr, carry, kbuf, vbuf, lfbuf, sems,
                *, layer, nq, prompt_steps, decode_steps, steps_per_seq):
    pages = DECODE_PAGES
    g = pl.program_id(0)

    def page_copies(step, slot):
        copies = []
        for u in range(pages):
            pg = pt_ref[step, u]
            copies += [pltpu.make_async_copy(ck_ref.at[layer, pg], kbuf.at[slot, u], sems.at[slot, 0]),
                       pltpu.make_async_copy(cv_ref.at[layer, pg], vbuf.at[slot, u], sems.at[slot, 1]),
                       pltpu.make_async_copy(clf_ref.at[layer, pg], lfbuf.at[slot, u], sems.at[slot, 2])]
        return copies

    @pl.when(g == 0)
    def _():
        for c in page_copies(0, 0):
            c.start()

    @pl.when(g + 1 < decode_steps)
    def _():
        for c in page_copies(g + 1, (g + 1) % 2):
            c.start()

    @pl.when(g < decode_steps)
    def _():
        for c in page_copies(g, g % 2):
            c.wait()

    slot = jnp.minimum(g, decode_steps - 1) % 2
    k_refs = [kbuf.at[slot, u] for u in range(pages)]
    v_refs = [vbuf.at[slot, u] for u in range(pages)]
    lf_refs = [lfbuf.at[slot, u] for u in range(pages)]
    i = qrow_ref[g] % nq
    j = krow_ref[g] % nq
    in_prompt = g < prompt_steps
    in_decode = g < decode_steps
    dstep = jnp.minimum(g, decode_steps - 1) % steps_per_seq
    dstate = (qb_scr, dm_scr, dl_scr, dacc_scr, carry)

    @pl.when(jnp.logical_and(in_decode, dstep == 0))
    def _():
        _decode_init(qs_ref, qf_scr, *dstate)

    @pl.when(jnp.logical_and(in_prompt, j == 0))
    def _():
        _prompt_init(q_ref, qa_scr, m_scr, acc_scr)

    @pl.when(jnp.logical_and(in_prompt, j < i))
    def _():
        _prompt_step(k_ref, vt_ref, b_ref, qa_scr, m_scr, acc_scr, False)
        _decode_step(k_refs, v_refs, lf_refs, *dstate)

    @pl.when(jnp.logical_and(in_prompt, j == i))
    def _():
        _prompt_step(k_ref, vt_ref, b_ref, qa_scr, m_scr, acc_scr, True)
        _decode_step(k_refs, v_refs, lf_refs, *dstate)
        _prompt_finish(o_ref, acc_scr)

    @pl.when(jnp.logical_not(in_prompt))
    def _():
        _decode_step(k_refs, v_refs, lf_refs, *dstate)

    @pl.when(jnp.logical_and(in_decode, dstep == steps_per_seq - 1))
    def _():
        _decode_finish(os_ref, kn_ref, vn_ref, lfn_ref, qf_scr, dm_scr, dl_scr, dacc_scr, carry)


def _fox(layer, q, k, vt, bias, n_seq, seq_len, q_s, k_new, v_new, lf_new, cache_k, cache_v, cache_lf,
         page_table, tq=512):
    nq = seq_len // tq
    pairs = [(i, j) for i in range(nq) for j in range(i + 1)]
    n_s, n_pages = page_table.shape
    page = cache_k.shape[3]
    pages = DECODE_PAGES
    steps_per_seq = n_pages // pages
    prompt_steps = n_seq * len(pairs)
    decode_steps = n_s * steps_per_seq
    n_steps = max(prompt_steps, decode_steps)
    lfn = jnp.pad(lf_new[:, :H_C], ((0, 0), (0, HEAD_ROWS - H_C))).reshape(n_s, HEAD_ROWS, 1)
    p_of = [min(g, prompt_steps - 1) for g in range(n_steps)]
    d_of = [min(g, decode_steps - 1) for g in range(n_steps)]
    qrow = jnp.asarray([(p // len(pairs)) * nq + pairs[p % len(pairs)][0] for p in p_of], jnp.int32)
    krow = jnp.asarray([(p // len(pairs)) * nq + pairs[p % len(pairs)][1] for p in p_of], jnp.int32)
    dseq = jnp.asarray([d // steps_per_seq for d in d_of], jnp.int32)
    step_pages = page_table.reshape(decode_steps, pages)[jnp.asarray(d_of, jnp.int32)]

    q_spec = pl.BlockSpec((tq, W_C), lambda g, qrow, krow, dseq, pt: (qrow[g], 0))
    k_spec = pl.BlockSpec((tq, W_C), lambda g, qrow, krow, dseq, pt: (krow[g], 0))
    vt_spec = pl.BlockSpec((W_C, tq), lambda g, qrow, krow, dseq, pt: (0, krow[g]))
    seq3 = lambda rows, w: pl.BlockSpec((1, rows, w), lambda g, qrow, krow, dseq, pt: (dseq[g], 0, 0))

    in_hbm = pl.BlockSpec(memory_space=pl.ANY)
    in_specs = [q_spec, k_spec, vt_spec, k_spec,
                seq3(1, W_C), seq3(1, W_C), seq3(1, W_C), seq3(HEAD_ROWS, 1), in_hbm, in_hbm, in_hbm]
    yc, yc_s = pl.pallas_call(
        functools.partial(_fox_kernel, layer=layer, nq=nq, prompt_steps=prompt_steps,
                          decode_steps=decode_steps, steps_per_seq=steps_per_seq),
        grid_spec=pltpu.PrefetchScalarGridSpec(
            num_scalar_prefetch=4,
            grid=(n_steps,),
            in_specs=in_specs,
            out_specs=[q_spec, seq3(1, W_C)],
            scratch_shapes=[pltpu.VMEM((H_C, tq, LANES), BF16), pltpu.VMEM((H_C, 1, tq), F32),
                            pltpu.VMEM((H_C, LANES, tq), F32),
                            pltpu.VMEM((HEAD_ROWS, W_C), F32), pltpu.VMEM((HEAD_ROWS, W_C), BF16),
                            pltpu.VMEM((HEAD_ROWS, 1), F32), pltpu.VMEM((HEAD_ROWS, 1), F32),
                            pltpu.VMEM((HEAD_ROWS, W_C), F32), pltpu.VMEM((8, 1), F32),
                            pltpu.VMEM((2, pages, W_C, page), F32), pltpu.VMEM((2, pages, W_C, page), F32),
                            pltpu.VMEM((2, pages, 8, page), F32), pltpu.SemaphoreType.DMA((2, 3))]),
        out_shape=[jax.ShapeDtypeStruct((n_seq * seq_len, W_C), BF16),
                   jax.ShapeDtypeStruct((n_s, 1, W_C), F32)],
        compiler_params=_params(1),
        name="fox",
    )(qrow, krow, dseq, step_pages, q, k, vt, bias,
      q_s.reshape(n_s, 1, W_C), k_new.reshape(n_s, 1, W_C), v_new.reshape(n_s, 1, W_C), lfn,
      cache_k, cache_v, cache_lf)
    return yc, yc_s.reshape(n_s, W_C)


def _outproj_kernel(x_ref, ya_ref, ob_ref, bonus_ref, g_ref, yc_ref, ln_ref, avg_ref, wo_ref, o_ref):
    ob = ob_ref[...]
    mu = _dot_split(ob, avg_ref[...])
    d = ob - mu
    var = _dot((d * d).astype(BF16), avg_ref[...])
    yb = (d * lax.rsqrt(var + GN_EPS) * ln_ref[0:1, :] + ln_ref[1:2, :] + bonus_ref[...]) * g_ref[...]
    y = jnp.concatenate([ya_ref[...], yb.astype(BF16), yc_ref[...]], axis=1)
    o_ref[...] = x_ref[...] + _dot(y, wo_ref[...])


def _outproj(x, ya, ob, bonus, g, yc, ln, avg_b, wo, tm=512):
    m = x.shape[0]
    tm = min(tm, m)
    row = lambda i: (i, 0)
    fix = lambda i: (0, 0)
    return pl.pallas_call(
        _outproj_kernel,
        grid=(m // tm,),
        in_specs=[pl.BlockSpec((tm, D_MODEL), row), pl.BlockSpec((tm, W_A), row),
                  pl.BlockSpec((tm, W_B), row), pl.BlockSpec((tm, W_B), row),
                  pl.BlockSpec((tm, W_B), row), pl.BlockSpec((tm, W_C), row),
                  pl.BlockSpec((8, W_B), fix), pl.BlockSpec((W_B, W_B), fix),
                  pl.BlockSpec((D_MODEL, D_MODEL), fix)],
        out_specs=pl.BlockSpec((tm, D_MODEL), row),
        out_shape=jax.ShapeDtypeStruct((m, D_MODEL), F32),
        compiler_params=_params(1),
        name="outproj",
    )(x, ya, ob, bonus, g, yc, ln, avg_b, wo)


def _block_diag_const(width, value):
    idx = jnp.arange(width) // GROUP
    return jnp.where(idx[:, None] == idx[None, :], value, 0.0).astype(BF16)


def _pad_rows(vecs, width):
    rows = [jnp.pad(v, (0, width - v.shape[0])) for v in vecs]
    rows += [jnp.zeros((width,), F32)] * (8 - len(rows))
    return jnp.stack(rows)


def _layer_weights(l, norm_g, w_ffn_in, w_ffn_out, w_in, a_ws, a_bs, a_norm_g, b_mu, b_w0, b_wB,
                   b_a0, b_aB, b_gB, b_kk, b_ka, b_rk, b_ln_g, b_ln_b, c_fb, w_o):
    zeros_lora = jnp.zeros((R_DECAY, W_B), F32)
    return dict(
        norm_g=norm_g[l],
        w_ffn_in=w_ffn_in[l].astype(BF16),
        w_ffn_out=w_ffn_out[l].astype(BF16),
        w_in=jnp.pad(w_in[l], ((0, 0), (0, IN_PROJ_PAD - IN_PROJ))).astype(BF16),
        c_fb=jnp.pad(c_fb[l], (0, LANES - H_C)).reshape(1, LANES),
        a_ws=a_ws[l],
        a_bias=jnp.repeat(a_bs[l].T, GROUP, axis=1),
        a_norm_g=a_norm_g[l],
        b_vec=_pad_rows([b_mu[l], b_w0[l], b_a0[l], b_kk[l], b_ka[l], b_rk[l]], B_PROJ),
        b_wB=jnp.concatenate([b_wB[l], zeros_lora], axis=0).astype(BF16),
        b_aB=jnp.concatenate([zeros_lora, b_aB[l]], axis=0).astype(BF16),
        b_gB=b_gB[l].astype(BF16),
        b_ln=_pad_rows([b_ln_g[l], b_ln_b[l]], W_B),
        w_o=w_o[l].astype(BF16),
        ones_b=_block_diag_const(W_B, 1.0),
        avg_b=_block_diag_const(W_B, 1.0 / GROUP),
        avg_a=_block_diag_const(W_A, 1.0 / GROUP),
    )


def _mix_prompt(wts, z, n_seq, seq_len):
    za, zb = z[0], z[1]
    ya, _ = _gmlp(za, wts["a_norm_g"], wts["a_ws"], wts["a_bias"], wts["avg_a"])
    x1, x2, ub, op, sp, gam, g, bonus = _rwkv_chunks(zb, seq_len, wts)
    ob, st = _rwkv_scan(x1, x2, ub, op, sp, gam, n_seq, seq_len)
    st = st.reshape(n_seq, N_PAIR, 2, GROUP, 2, GROUP)
    wkv = jnp.stack([st[:, :, 0, :, 0, :], st[:, :, 1, :, 1, :]], axis=2)
    wkv = wkv.reshape(n_seq, H_B, GROUP, GROUP).transpose(0, 1, 3, 2)
    return ya, ob.reshape(n_seq * seq_len, W_B), bonus, g, wkv


def _mix_sample(wts, z, shift0, wkv0):
    za, zb = z[0], z[1]
    n = za.shape[0]
    za_pad = jnp.pad(za[:, None, :], ((0, 0), (0, CHUNK_A - 1), (0, 0))).reshape(n * CHUNK_A, A_PROJ)
    ya, va = _gmlp(za_pad, wts["a_norm_g"], wts["a_ws"], wts["a_bias"], wts["avg_a"])
    ya = ya.reshape(n, CHUNK_A, W_A)[:, 0]
    va = va.reshape(n, CHUNK_A, W_A)[:, 0]
    r, lw, k2, vb, kk, beta, g, bonus = _rwkv_prep(zb, shift0, wts)
    ob, wkv = _rwkv_step(wkv0, r, lw, k2, kk, beta, vb)
    return ya, ob, bonus, g, wkv, va


def kernel(x_prompt, x_sample, cache_k, cache_v, cache_logf, state_wkv, state_shift, page_table,
           norm_g, w_ffn_in, w_ffn_out, w_in, a_ws, a_bs, a_norm_g, b_mu, b_w0, b_wB, b_a0, b_aB,
           b_gB, b_kk, b_ka, b_rk, b_ln_g, b_ln_b, c_fb, w_o, final_norm):
    n_p, seq_len, _ = x_prompt.shape
    n_s = x_sample.shape[0]
    depth = norm_g.shape[0]
    n_phys, page = cache_k.shape[1], cache_k.shape[2]
    ck = jnp.transpose(cache_k, (0, 1, 3, 4, 2)).reshape(depth, n_phys, W_C, page)
    cv = jnp.transpose(cache_v, (0, 1, 3, 4, 2)).reshape(depth, n_phys, W_C, page)
    clf = jnp.pad(jnp.transpose(cache_logf, (0, 1, 3, 2)), ((0, 0), (0, 0), (0, 8 - H_C), (0, 0)))
    xp = x_prompt.reshape(n_p * seq_len, D_MODEL)
    xs = x_sample.reshape(n_s, D_MODEL)
    outs = {name: [] for name in ("kp", "vp", "lfp", "wkvp", "shp", "ks", "vs", "lfs", "wkvs", "shs", "va")}
    for l in range(depth):
        wts = _layer_weights(l, norm_g, w_ffn_in, w_ffn_out, w_in, a_ws, a_bs, a_norm_g, b_mu, b_w0,
                             b_wB, b_a0, b_aB, b_gB, b_kk, b_ka, b_rk, b_ln_g, b_ln_b, c_fb, w_o)
        last = l == depth - 1
        fin = final_norm if last else None

        xp = _ffn(xp, wts["norm_g"][0], wts["w_ffn_in"][0], wts["w_ffn_out"][0])
        xs = _ffn(xs, wts["norm_g"][0], wts["w_ffn_in"][0], wts["w_ffn_out"][0])
        zp = _inproj(xp, wts["norm_g"][1], wts["w_in"], wts["c_fb"], seq_len=seq_len)
        zs = _inproj(xs, wts["norm_g"][1], wts["w_in"], wts["c_fb"])
        ya, ob, bonus, g, wkvp = _mix_prompt(wts, zp, n_p, seq_len)
        ya_s, ob_s, bonus_s, g_s, wkvs, va = _mix_sample(wts, zs, state_shift[l], state_wkv[l])
        yc, yc_s = _fox(l, zp[2], zp[6], zp[7], _cumsum(zp[3], n_p, seq_len), n_p, seq_len,
                        zs[2], zs[4], zs[5], zs[3], ck, cv, clf, page_table)
        xp = _outproj(xp, ya, ob, bonus, g, yc, wts["b_ln"], wts["avg_b"], wts["w_o"])
        xs = _outproj(xs, ya_s, ob_s, bonus_s, g_s, yc_s.astype(BF16), wts["b_ln"], wts["avg_b"], wts["w_o"])
        xp = _ffn(xp, wts["norm_g"][2], wts["w_ffn_in"][1], wts["w_ffn_out"][1], final_g=fin)
        xs = _ffn(xs, wts["norm_g"][2], wts["w_ffn_in"][1], wts["w_ffn_out"][1], final_g=fin)
        outs["kp"].append(zp[4].reshape(n_p, H_C, GROUP, seq_len).transpose(0, 3, 1, 2))
        outs["vp"].append(zp[5].reshape(n_p, H_C, GROUP, seq_len).transpose(0, 3, 1, 2))
        outs["lfp"].append(zp[3][:, :H_C].reshape(n_p, seq_len, H_C))
        outs["wkvp"].append(wkvp)
        outs["shp"].append(zp[1].reshape(n_p, seq_len, B_PROJ)[:, -1])
        outs["ks"].append(zs[4].reshape(n_s, 1, H_C, GROUP))
        outs["vs"].append(zs[5].reshape(n_s, 1, H_C, GROUP))
        outs["lfs"].append(zs[3][:, :H_C].reshape(n_s, 1, H_C))
        outs["wkvs"].append(wkvs)
        outs["shs"].append(zs[1])
        outs["va"].append(va.reshape(n_s, 1, W_A))
    st = lambda name: jnp.stack(outs[name])
    return (xp.reshape(n_p, seq_len, D_MODEL), xs.reshape(n_s, 1, D_MODEL),
            st("kp"), st("vp"), st("lfp"), st("wkvp"), st("shp"),
            st("ks"), st("vs"), st("lfs"), st("wkvs"), st("shs"), st("va"))
```

```python
import functools

import jax
import jax.numpy as jnp
from jax import lax
from jax.experimental import pallas as pl
from jax.experimental.pallas import tpu as pltpu

F32 = jnp.float32
BF16 = jnp.bfloat16

LANES = 128
D_MODEL = 1024
D_FF = 2816
GROUP = 64
W_A = 256
W_B = 384
W_C = 384
H_B = W_B // GROUP
H_C = W_C // GROUP
N_PAIR = W_B // LANES
R_DECAY = 64
R_AAA = 64
R_GATE = 128
B_PROJ = 3 * W_B + R_DECAY + R_AAA + R_GATE
A_PROJ = 2 * W_A
C_PROJ = 3 * W_C + H_C
IN_PROJ = A_PROJ + B_PROJ + C_PROJ
IN_PROJ_PAD = A_PROJ + B_PROJ + 3 * W_C + LANES
CHUNK_A = 128
CHUNK_B = 64
NORM_EPS = 1e-6
GN_EPS = 64e-5
NEG_BIG = -1e30
LOG2E = 1.4426950408889634
Q_SCALE = LOG2E * GROUP ** -0.5
VMEM_LIMIT = 56 << 20


def _params(n_axes, vmem=VMEM_LIMIT):
    return pltpu.CompilerParams(dimension_semantics=("arbitrary",) * n_axes,
                                vmem_limit_bytes=vmem)


def _sigmoid(x):
    return 1.0 / (1.0 + jnp.exp(-x))


def _softplus(x):
    return jnp.maximum(x, 0.0) + jnp.log(1.0 + jnp.exp(-jnp.abs(x)))


def _gelu_tanh(x):
    return 0.5 * x * (1.0 + jnp.tanh(0.7978845608028654 * (x + 0.044715 * (x * x * x))))


def _rms(x, g):
    return x * lax.rsqrt(jnp.mean(x * x, axis=-1, keepdims=True) + NORM_EPS) * g


def _dot(a, b):
    return jnp.dot(a, b, preferred_element_type=F32)


def _dot_nt(a, b):
    return lax.dot_general(a, b, (((1,), (1,)), ((), ())), preferred_element_type=F32)


def _dot_split(a, b_bf):
    hi = a.astype(BF16)
    lo = (a - hi.astype(F32)).astype(BF16)
    return _dot(hi, b_bf) + _dot(lo, b_bf)


def _bf16_pieces(x):
    pieces = []
    for _ in range(3):
        piece = x.astype(BF16)
        pieces.append(piece)
        x = x - piece.astype(F32)
    return pieces


def _dot_ones_left(ones, x):
    n = x.shape[1]
    y = _dot(ones.astype(BF16), jnp.concatenate(_bf16_pieces(x), axis=1))
    return y[:, :n] + y[:, n:2 * n] + y[:, 2 * n:]


def _dot_ones_right(x, ones):
    m = x.shape[0]
    y = _dot(jnp.concatenate(_bf16_pieces(x), axis=0), ones.astype(BF16))
    return y[:m] + y[m:2 * m] + y[2 * m:]


def _iota(shape, dim):
    return lax.broadcasted_iota(jnp.int32, shape, dim)


MXU_TILE = 256
FF_SLICES = (6 * MXU_TILE, 5 * MXU_TILE)
assert sum(FF_SLICES) == D_FF


def _ffn_kernel(*refs, final):
    if final:
        x_ref, g_ref, wi_ref, wo_ref, fg_ref, o_ref = refs
    else:
        x_ref, g_ref, wi_ref, wo_ref, o_ref = refs
    x = x_ref[...]
    h = _rms(x, g_ref[...]).astype(BF16)
    acc = None
    lo = 0
    for width in FF_SLICES:
        gate = _dot(h, wi_ref[:, lo:lo + width])
        up = _dot(h, wi_ref[:, D_FF + lo:D_FF + lo + width])
        act = (gate * _sigmoid(gate) * up).astype(BF16)
        part = _dot(act, wo_ref[lo:lo + width, :])
        acc = part if acc is None else acc + part
        lo += width
    y = x + 0.5 * acc
    if final:
        y = _rms(y, fg_ref[...])
    o_ref[...] = y


def _ffn(x, g, w_in, w_out, which, final_g=None, tm=512):
    m = x.shape[0]
    tm = min(tm, m)
    final = final_g is not None
    fix = lambda i: (0, 0)
    pick = lambda i: tuple(which) + (0, 0)
    resident = pl.Buffered(1)
    in_specs = [
        pl.BlockSpec((tm, D_MODEL), lambda i: (i, 0)),
        pl.BlockSpec((1, D_MODEL), fix),
        pl.BlockSpec((None, None, D_MODEL, 2 * D_FF), pick, pipeline_mode=resident),
        pl.BlockSpec((None, None, D_FF, D_MODEL), pick, pipeline_mode=resident),
    ]
    args = [x, g.reshape(1, D_MODEL), w_in, w_out]
    if final:
        in_specs.append(pl.BlockSpec((1, D_MODEL), fix))
        args.append(final_g.reshape(1, D_MODEL))
    return pl.pallas_call(
        functools.partial(_ffn_kernel, final=final),
        grid=(m // tm,),
        in_specs=in_specs,
        out_specs=pl.BlockSpec((tm, D_MODEL), lambda i: (i, 0)),
        out_shape=jax.ShapeDtypeStruct((m, D_MODEL), F32),
        compiler_params=_params(1),
        name="ffn",
    )(*args)


def _inproj_kernel(x_ref, g_ref, w_ref, fb_ref, za_ref, zb_ref, q_ref, lf_ref, *kv_refs,
                   channel_major):
    h = _rms(x_ref[...], g_ref[...]).astype(BF16)
    z = _dot(h, w_ref[...])
    o = A_PROJ
    za_ref[...] = z[:, :o]
    zb_ref[...] = z[:, o:o + B_PROJ]
    o += B_PROJ
    q_ref[...] = (z[:, o:o + W_C] * Q_SCALE).astype(BF16)
    k = z[:, o + W_C:o + 2 * W_C]
    v = z[:, o + 2 * W_C:o + 3 * W_C]
    lf_ref[...] = -_softplus(-(z[:, o + 3 * W_C:] + fb_ref[...]))
    if channel_major:
        kt_ref, vt_ref, kb_ref, vtb_ref = kv_refs
        kt_ref[...] = k.T
        vt = v.T
        vt_ref[...] = vt
        vtb_ref[...] = vt.astype(BF16)
        kb_ref[...] = k.astype(BF16)
    else:
        k_ref, v_ref = kv_refs
        k_ref[...] = k
        v_ref[...] = v


def _inproj(x, g, w_pad, fb_pad, seq_len=None, tm=512):
    m = x.shape[0]
    tm = min(tm, m)
    row = lambda i: (i, 0)
    fix = lambda i: (0, 0)
    widths = (A_PROJ, B_PROJ, W_C, LANES)
    dtypes = (F32, F32, BF16, F32)
    out_specs = [pl.BlockSpec((tm, w), row) for w in widths]
    out_shape = [jax.ShapeDtypeStruct((m, w), d) for w, d in zip(widths, dtypes)]
    if seq_len is None:
        out_specs += [pl.BlockSpec((tm, W_C), row)] * 2
        out_shape += [jax.ShapeDtypeStruct((m, W_C), F32)] * 2
    else:
        bps = seq_len // tm
        seq_blk = pl.BlockSpec((None, W_C, tm), lambda i: (i // bps, 0, i % bps))
        out_specs += [seq_blk, seq_blk, pl.BlockSpec((tm, W_C), row), pl.BlockSpec((W_C, tm), lambda i: (0, i))]
        out_shape += [jax.ShapeDtypeStruct((m // seq_len, W_C, seq_len), F32)] * 2
        out_shape += [jax.ShapeDtypeStruct((m, W_C), BF16), jax.ShapeDtypeStruct((W_C, m), BF16)]
    return pl.pallas_call(
        functools.partial(_inproj_kernel, channel_major=seq_len is not None),
        grid=(m // tm,),
        in_specs=[pl.BlockSpec((tm, D_MODEL), row), pl.BlockSpec((1, D_MODEL), fix),
                  pl.BlockSpec((D_MODEL, IN_PROJ_PAD), fix),
                  pl.BlockSpec((1, LANES), fix)],
        out_specs=out_specs,
        out_shape=out_shape,
        compiler_params=_params(1),
        name="inproj",
    )(x, g.reshape(1, D_MODEL), w_pad, fb_pad)


def _gmlp_kernel(za_ref, gain_ref, ws_ref, bias_ref, avg_ref, ya_ref, va_ref, *, n_chunks):
    z = _gelu_tanh(za_ref[...])
    u = z[:, :W_A]
    v = z[:, W_A:]
    ms = _dot((v * v).astype(BF16), avg_ref[...])
    vn = v * lax.rsqrt(ms + NORM_EPS) * gain_ref[...]
    va_ref[...] = vn
    causal = _iota((CHUNK_A, CHUNK_A), 0) >= _iota((CHUNK_A, CHUNK_A), 1)
    lane_group = _iota((CHUNK_A, W_A), 1) // GROUP
    wm = [jnp.where(causal, ws_ref[g], 0.0).astype(BF16) for g in range(W_A // GROUP)]
    for c in range(n_chunks):
        rows = slice(c * CHUNK_A, (c + 1) * CHUNK_A)
        vc = vn[rows]
        s = bias_ref[...]
        for g in range(W_A // GROUP):
            s = s + _dot(wm[g], jnp.where(lane_group == g, vc, 0.0).astype(BF16))
        ya_ref[rows, :] = (u[rows] * s).astype(BF16)


def _gmlp(za, gain, ws, bias_full, avg_a, tm=512):
    m = za.shape[0]
    tm = min(tm, m)
    row = lambda i: (i, 0)
    fix = lambda i: (0, 0)
    return pl.pallas_call(
        functools.partial(_gmlp_kernel, n_chunks=tm // CHUNK_A),
        grid=(m // tm,),
        in_specs=[pl.BlockSpec((tm, A_PROJ), row), pl.BlockSpec((1, W_A), fix),
                  pl.BlockSpec((W_A // GROUP, CHUNK_A, CHUNK_A), lambda i: (0, 0, 0)),
                  pl.BlockSpec((CHUNK_A, W_A), fix), pl.BlockSpec((W_A, W_A), fix)],
        out_specs=[pl.BlockSpec((tm, W_A), row), pl.BlockSpec((tm, W_A), row)],
        out_shape=[jax.ShapeDtypeStruct((m, W_A), BF16), jax.ShapeDtypeStruct((m, W_A), F32)],
        compiler_params=_params(1),
        name="gmlp",
    )(za, gain.reshape(1, W_A), ws, bias_full, avg_a)


def _rwkv_prep_math(zb, prev, vec_ref, wb_ref, ab_ref, gb_ref, ones_ref):
    mu = vec_ref[0:1, :]
    zs = zb + (prev - zb) * mu
    r = zs[:, :W_B]
    k = zs[:, W_B:2 * W_B]
    v = zs[:, 2 * W_B:3 * W_B]
    lora = zs[:, 3 * W_B:3 * W_B + LANES]
    gl = zs[:, 3 * W_B + LANES:]
    w0 = vec_ref[1:2, :W_B]
    a0 = vec_ref[2:3, :W_B]
    kkw = vec_ref[3:4, :W_B]
    kaw = vec_ref[4:5, :W_B]
    rkw = vec_ref[5:6, :W_B]
    w = -_softplus(-(w0 + _dot(jnp.tanh(lora).astype(BF16), wb_ref[...]))) - 0.5
    a = _sigmoid(a0 + _dot(lora.astype(BF16), ab_ref[...]))
    g = _dot(_sigmoid(gl).astype(BF16), gb_ref[...])
    kk = k * kkw
    ss = _dot_split(kk * kk, ones_ref[...])
    kk = kk / jnp.maximum(jnp.sqrt(ss), 1e-12)
    k2 = k * (1.0 + (a - 1.0) * kaw)
    lw = -jnp.exp(w)
    bonus = _dot_split(r * k2 * rkw, ones_ref[...]) * v
    return r, lw, k2, v, kk, kk * a, g, bonus


def _rwkv_prep_specs(wts):
    fix = lambda i: (0, 0)
    specs = [pl.BlockSpec((8, B_PROJ), fix), pl.BlockSpec((LANES, W_B), fix),
             pl.BlockSpec((LANES, W_B), fix), pl.BlockSpec((R_GATE, W_B), fix),
             pl.BlockSpec((W_B, W_B), fix)]
    return specs, [wts["b_vec"], wts["b_wB"], wts["b_aB"], wts["b_gB"], wts["ones_b"]]


def _rwkv_prep_tok_kernel(zb_ref, prev_ref, vec_ref, wb_ref, ab_ref, gb_ref, ones_ref, *outs):
    vals = _rwkv_prep_math(zb_ref[...], prev_ref[...], vec_ref, wb_ref, ab_ref, gb_ref, ones_ref)
    for ref, val in zip(outs, vals):
        ref[...] = val


def _rwkv_prep(zb, prev, wts):
    m = zb.shape[0]
    row = lambda i: (i, 0)
    w_specs, w_args = _rwkv_prep_specs(wts)
    return pl.pallas_call(
        _rwkv_prep_tok_kernel,
        grid=(1,),
        in_specs=[pl.BlockSpec((m, B_PROJ), row), pl.BlockSpec((m, B_PROJ), row)] + w_specs,
        out_specs=[pl.BlockSpec((m, W_B), row)] * 8,
        out_shape=[jax.ShapeDtypeStruct((m, W_B), F32)] * 8,
        compiler_params=_params(1),
        name="rwkv_prep",
    )(zb, prev, *w_args)


def _stack(x, low):
    return jnp.concatenate([jnp.where(low, x, 0.0), jnp.where(low, 0.0, x)], axis=0)


CHUNKS_PER_STEP = 4

def _rwkv_chunk_kernel(zb_ref, pb_ref, vec_ref, wb_ref, ab_ref, gb_ref, ones_ref,
                       x1_ref, x2_ref, ub_ref, op_ref, sp_ref, gam_ref, g_ref, bonus_ref, *, blocks_per_seq):
    c = CHUNK_B
    n2 = 2 * c
    rows = CHUNKS_PER_STEP * c
    zb = zb_ref[...]
    first = (pl.program_id(0) % blocks_per_seq) == 0
    last_prev = jnp.where(first, 0.0, pb_ref[7:8, :])
    prev = jnp.where(_iota((rows, 1), 0) == 0, last_prev, pltpu.roll(zb, shift=1, axis=0))
    r, lw, k2, v, kk, beta, g, bonus = _rwkv_prep_math(zb, prev, vec_ref, wb_ref, ab_ref, gb_ref, ones_ref)
    g_ref[...] = g
    bonus_ref[...] = bonus
    ri = _iota((rows, rows), 0)
    ci = _iota((rows, rows), 1)
    tri = jnp.where(ri >= ci, 1.0, 0.0) * jnp.where((ri // c) == (ci // c), 1.0, 0.0)
    cum = _dot_ones_left(tri, lw)
    lasts = [cum[(j + 1) * c - 1:(j + 1) * c, :] for j in range(CHUNKS_PER_STEP)]
    for j in range(CHUNKS_PER_STEP):
        gam_ref[j] = jnp.exp(lasts[j])
    cum_last = jnp.concatenate([jnp.broadcast_to(l, (c, W_B)) for l in lasts], axis=0)
    e_pos = jnp.exp(cum)
    e_neg = jnp.exp(-cum)
    e_tail = jnp.exp(cum_last - cum)
    r_t = r * e_pos
    kap_t = kk * jnp.exp(cum - lw)
    beta_h = beta * e_neg
    k_h = k2 * e_neg
    beta_c = beta * e_tail
    k_c = k2 * e_tail

    low = _iota((c, LANES), 1) < GROUP
    rr = _iota((n2, n2), 0) & (c - 1)
    cc = _iota((n2, n2), 1) & (c - 1)
    strict = rr > cc
    incl = rr >= cc
    streams = [(j, p) for j in range(CHUNKS_PER_STEP) for p in range(N_PAIR)]

    def tile(x, j, p):
        return _stack(x[j * c:(j + 1) * c, p * LANES:(p + 1) * LANES], low)

    kap_s = [tile(kap_t, j, p) for j, p in streams]
    r_s = [tile(r_t, j, p) for j, p in streams]
    v_s = [tile(v, j, p).astype(BF16) for j, p in streams]
    gram = [_dot_nt(jnp.concatenate([kap_s[i], r_s[i]], axis=0).astype(BF16),
                    jnp.concatenate([tile(beta_h, j, p), tile(k_h, j, p)], axis=0).astype(BF16))
            for i, (j, p) in enumerate(streams)]
    n_bf = [jnp.where(strict, g[:n2, :n2], 0.0).astype(BF16) for g in gram]
    av = [_dot(jnp.where(strict, g[:n2, n2:], 0.0).astype(BF16), vs) for g, vs in zip(gram, v_s)]
    for i, (j, p) in enumerate(streams):
        op_ref[j, p] = _dot(jnp.where(incl, gram[i][n2:, n2:], 0.0).astype(BF16), v_s[i]).astype(BF16)
        sp_ref[j, p] = _dot(tile(k_c, j, p).T.astype(BF16), v_s[i]).astype(BF16)
        x2_ref[j, p] = jnp.concatenate([jnp.where(incl, gram[i][n2:, :n2], 0.0),
                                        tile(beta_c, j, p).T], axis=0).astype(BF16)
    x = [jnp.concatenate([ks, -a], axis=1) for ks, a in zip(kap_s, av)]
    x = [xi - _dot(nb, xi.astype(BF16)) for xi, nb in zip(x, n_bf)]
    pw = n_bf
    for _ in range(5):
        pw = [_dot(q, q).astype(BF16) for q in pw]
        x = [xi + _dot(q, xi.astype(BF16)) for xi, q in zip(x, pw)]
    for i, (j, p) in enumerate(streams):
        x1_ref[j, p] = jnp.concatenate([x[i][:, :LANES], r_s[i]], axis=0).astype(BF16)
        ub_ref[j, p] = x[i][:, LANES:].astype(BF16)


def _rwkv_chunks(zb, seq_len, wts):
    m = zb.shape[0]
    nc = m // CHUNK_B
    cb = CHUNKS_PER_STEP
    tm = cb * CHUNK_B
    row = lambda i: (i, 0)
    blk = lambda i: (i, 0, 0, 0)
    t = 2 * CHUNK_B
    w_specs, w_args = _rwkv_prep_specs(wts)
    return pl.pallas_call(
        functools.partial(_rwkv_chunk_kernel, blocks_per_seq=seq_len // tm),
        grid=(nc // cb,),
        in_specs=[pl.BlockSpec((tm, B_PROJ), row),
                  pl.BlockSpec((8, B_PROJ), lambda i: (jnp.maximum(i * (tm // 8) - 1, 0), 0))] + w_specs,
        out_specs=[pl.BlockSpec((cb, N_PAIR, 2 * t, LANES), blk),
                   pl.BlockSpec((cb, N_PAIR, 2 * t, LANES), blk),
                   pl.BlockSpec((cb, N_PAIR, t, LANES), blk),
                   pl.BlockSpec((cb, N_PAIR, t, LANES), blk),
                   pl.BlockSpec((cb, N_PAIR, t, LANES), blk),
                   pl.BlockSpec((cb, 1, W_B), lambda i: (i, 0, 0)),
                   pl.BlockSpec((tm, W_B), row), pl.BlockSpec((tm, W_B), row)],
        out_shape=[jax.ShapeDtypeStruct((nc, N_PAIR, 2 * t, LANES), BF16),
                   jax.ShapeDtypeStruct((nc, N_PAIR, 2 * t, LANES), BF16),
                   jax.ShapeDtypeStruct((nc, N_PAIR, t, LANES), BF16),
                   jax.ShapeDtypeStruct((nc, N_PAIR, t, LANES), BF16),
                   jax.ShapeDtypeStruct((nc, N_PAIR, t, LANES), BF16),
                   jax.ShapeDtypeStruct((nc, 1, W_B), F32),
                   jax.ShapeDtypeStruct((m, W_B), F32), jax.ShapeDtypeStruct((m, W_B), F32)],
        compiler_params=_params(1),
        name="rwkv_chunks",
    )(zb, zb, *w_args)


def _rwkv_scan_kernel(x1_ref, x2_ref, ub_ref, op_ref, sp_ref, gam_ref, o_ref, st_ref, st_scr,
                      *, n_seq, n_chunks):
    ci = pl.program_id(0)
    t = 2 * CHUNK_B

    @pl.when(ci == 0)
    def _():
        st_scr[...] = jnp.zeros_like(st_scr)

    eye = _iota((t, t), 0) == _iota((t, t), 1)
    streams = [(b, p) for b in range(n_seq) for p in range(N_PAIR)]
    st = [st_scr[b * N_PAIR + p] for b, p in streams]
    y = [_dot(x1_ref[b, 0, p], s.astype(BF16)) for (b, p), s in zip(streams, st)]
    u = [ub_ref[b, 0, p] - yi[:t] for (b, p), yi in zip(streams, y)]
    z = [_dot(x2_ref[b, 0, p], ui.astype(BF16)) for (b, p), ui in zip(streams, u)]
    for i, (b, p) in enumerate(streams):
        o_s = op_ref[b, 0, p] + y[i][t:] + z[i][:t]
        gam_row = gam_ref[b, 0, :, p * LANES:(p + 1) * LANES]
        gam_col = jnp.sum(jnp.where(eye, gam_row, 0.0), axis=1, keepdims=True)
        st_scr[b * N_PAIR + p] = gam_col * st[i] + z[i][t:] + sp_ref[b, 0, p]
        o_ref[b, :, p * LANES:(p + 1) * LANES] = o_s[:CHUNK_B] + o_s[CHUNK_B:]

    @pl.when(ci == n_chunks - 1)
    def _():
        st_ref[...] = st_scr[...]


def _rwkv_scan(x1, x2, ub, op, sp, gam, n_seq, seq_len):
    nc = seq_len // CHUNK_B
    t = 2 * CHUNK_B
    r5 = lambda a: a.reshape((n_seq, nc) + a.shape[1:])
    blk5 = lambda rows: pl.BlockSpec((n_seq, 1, N_PAIR, rows, LANES), lambda c: (0, c, 0, 0, 0))
    return pl.pallas_call(
        functools.partial(_rwkv_scan_kernel, n_seq=n_seq, n_chunks=nc),
        grid=(nc,),
        in_specs=[blk5(2 * t), blk5(2 * t), blk5(t), blk5(t), blk5(t),
                  pl.BlockSpec((n_seq, 1, 1, W_B), lambda c: (0, c, 0, 0))],
        out_specs=[pl.BlockSpec((n_seq, CHUNK_B, W_B), lambda c: (0, c, 0)),
                   pl.BlockSpec((n_seq * N_PAIR, t, LANES), lambda c: (0, 0, 0))],
        out_shape=[jax.ShapeDtypeStruct((n_seq, seq_len, W_B), F32),
                   jax.ShapeDtypeStruct((n_seq * N_PAIR, t, LANES), F32)],
        scratch_shapes=[pltpu.VMEM((n_seq * N_PAIR, t, LANES), F32)],
        compiler_params=_params(1),
        name="rwkv_scan",
    )(r5(x1), r5(x2), r5(ub), r5(op), r5(sp), r5(gam))


STEP_SEQS = 8


def _rwkv_step_kernel(s_ref, r_ref, lw_ref, k_ref, kk_ref, beta_ref, vcol_ref, o_ref, so_ref):
    for b in range(s_ref.shape[0]):
        for h in range(H_B):
            s = s_ref[b, h]
            sk = jnp.sum(s * kk_ref[b, h], axis=1, keepdims=True)
            s_new = s * jnp.exp(lw_ref[b, h]) - sk * beta_ref[b, h] + vcol_ref[b, h] * k_ref[b, h]
            so_ref[b, h] = s_new
            o_ref[b, h] = jnp.sum(s_new * r_ref[b, h], axis=1, keepdims=True)


def _rwkv_step(state, r, lw, k2, kk, beta, v):
    n = state.shape[0]
    bs = STEP_SEQS if n % STEP_SEQS == 0 else 1
    rowv = lambda a: a.reshape(n, H_B, 1, GROUP)
    idx = lambda b: (b, 0, 0, 0)
    row_spec = pl.BlockSpec((bs, H_B, 1, GROUP), idx)
    col_spec = pl.BlockSpec((bs, H_B, GROUP, 1), idx)
    mat_spec = pl.BlockSpec((bs, H_B, GROUP, GROUP), idx)
    o, s_new = pl.pallas_call(
        _rwkv_step_kernel,
        grid=(n // bs,),
        in_specs=[mat_spec] + [row_spec] * 5 + [col_spec],
        out_specs=[col_spec, mat_spec],
        out_shape=[jax.ShapeDtypeStruct((n, H_B, GROUP, 1), F32),
                   jax.ShapeDtypeStruct((n, H_B, GROUP, GROUP), F32)],
        compiler_params=_params(1),
        name="rwkv_step",
    )(state, rowv(r), rowv(lw), rowv(k2), rowv(kk), rowv(beta), v.reshape(n, H_B, GROUP, 1))
    return o.reshape(n, W_B), s_new


BIAS_PIECES = 3


def _cumsum_kernel(lf_ref, place_ref, b_ref, carry):
    @pl.when(pl.program_id(1) == 0)
    def _():
        carry[...] = jnp.zeros_like(carry)

    tb = lf_ref.shape[0]
    tri = (_iota((tb, tb), 0) >= _iota((tb, tb), 1)).astype(F32)
    c = _dot_ones_left(tri, lf_ref[...]) + carry[...]
    carry[...] = c[tb - 1:tb, :]
    pieces = _bf16_pieces(-LOG2E * c)
    b_ref[...] = _dot(jnp.concatenate(pieces, axis=1), place_ref[...]).astype(BF16)


def _bias_placement():
    rows = jnp.arange(BIAS_PIECES * LANES)
    piece, head = rows // LANES, rows % LANES
    col = LANES * (head // 2) + jnp.where(head % 2 == 0, GROUP, 0) + piece
    hit = (col[:, None] == jnp.arange(W_C)[None, :]) & (head < H_C)[:, None]
    return hit.astype(BF16)


def _cumsum(lf, n_seq, seq_len, tb=512):
    nb = seq_len // tb
    return pl.pallas_call(
        _cumsum_kernel,
        grid=(n_seq, nb),
        in_specs=[pl.BlockSpec((tb, LANES), lambda b, j: (b * nb + j, 0)),
                  pl.BlockSpec((BIAS_PIECES * LANES, W_C), lambda b, j: (0, 0))],
        out_specs=pl.BlockSpec((tb, W_C), lambda b, j: (b * nb + j, 0)),
        out_shape=jax.ShapeDtypeStruct((n_seq * seq_len, W_C), BF16),
        scratch_shapes=[pltpu.VMEM((1, LANES), F32)],
        compiler_params=_params(2),
        name="logf_cumsum",
    )(lf, _bias_placement())


HEAD_ROWS = 16
DECODE_PAGES = 16


def _prompt_init(q_ref, qa_scr, m_scr, acc_scr):
    tq = q_ref.shape[0]
    lane = _iota((tq, LANES), 1)
    ones_hi = jnp.where(lane < GROUP + BIAS_PIECES, 1.0, 0.0).astype(BF16)
    ones_lo = jnp.where(lane < BIAS_PIECES, 1.0, 0.0).astype(BF16)
    for p in range(N_PAIR):
        q = q_ref[:, p * LANES:(p + 1) * LANES]
        qa_scr[2 * p] = jnp.where(lane < GROUP, q, ones_hi)
        qa_scr[2 * p + 1] = jnp.where(lane < GROUP, ones_lo, q)
    m_scr[...] = jnp.full_like(m_scr, NEG_BIG)
    acc_scr[...] = jnp.zeros_like(acc_scr)


def _prompt_step(k_ref, vt_ref, b_ref, qa_scr, m_scr, acc_scr, diagonal):
    tk = k_ref.shape[0]
    tq = qa_scr.shape[1]
    low_k = _iota((tk, LANES), 1) < GROUP
    low_v = _iota((LANES, tk), 0) < GROUP
    ka, va = [], []
    for p in range(N_PAIR):
        k = k_ref[:, p * LANES:(p + 1) * LANES]
        bias = b_ref[:, p * LANES:(p + 1) * LANES]
        vt = vt_ref[p * LANES:(p + 1) * LANES, :]
        one = jnp.ones_like(vt)
        ka += [jnp.where(low_k, k, bias), jnp.where(low_k, bias, k)]
        va += [jnp.where(low_v, vt, one), jnp.where(low_v, one, vt)]
    scores = [_dot_nt(ka[h], qa_scr[h]) for h in range(H_C)]
    for h in range(H_C):
        s = scores[h]
        if diagonal:
            s = jnp.where(_iota((tk, tq), 0) <= _iota((tk, tq), 1), s, NEG_BIG)
        m_prev = m_scr[h]
        m_new = jnp.maximum(m_prev, jnp.max(s, axis=0, keepdims=True))
        alpha = jnp.exp2(m_prev - m_new)
        pr = jnp.exp2(s - m_new).astype(BF16)
        acc_scr[h] = alpha * acc_scr[h] + _dot(va[h], pr)
        m_scr[h] = m_new


def _prompt_finish(o_ref, acc_scr):
    tq = o_ref.shape[0]
    low_row = _iota((LANES, tq), 0) < GROUP
    for p in range(N_PAIR):
        a0 = acc_scr[2 * p]
        a1 = acc_scr[2 * p + 1]
        out = jnp.where(low_row, a0 / a0[LANES - 1:LANES, :], a1 / a1[0:1, :])
        o_ref[:, p * LANES:(p + 1) * LANES] = out.T.astype(BF16)


def _own_head_mask():
    return (_iota((HEAD_ROWS, W_C), 1) // GROUP) == _iota((HEAD_ROWS, W_C), 0)


def _decode_init(q_ref, qf_scr, qb_scr, m_scr, l_scr, acc_scr, carry):
    qrows = jnp.where(_own_head_mask(), q_ref[0].astype(F32), 0.0)
    qf_scr[...] = qrows
    qb_scr[...] = qrows.astype(BF16)
    m_scr[...] = jnp.full_like(m_scr, NEG_BIG)
    l_scr[...] = jnp.zeros_like(l_scr)
    acc_scr[...] = jnp.zeros_like(acc_scr)
    carry[...] = jnp.zeros_like(carry)


def _decode_step(k_refs, v_refs, lf_refs, qb_scr, m_scr, l_scr, acc_scr, carry):
    pages = len(k_refs)
    page = k_refs[0].shape[1]
    upto = (_iota((page, page), 0) <= _iota((page, page), 1)).astype(F32)
    lf_all = jnp.concatenate([lf_refs[u][...] for u in range(pages)], axis=0)
    c_all = _dot_ones_right(lf_all, upto)
    totals = [c_all[8 * u:8 * (u + 1), page - 1:page] for u in range(pages)]
    run = carry[...]
    cts = []
    for u in range(pages):
        cts.append(c_all[8 * u:8 * (u + 1)] + run)
        run = run + totals[u]
    carry[...] = run
    ct = jnp.concatenate(cts, axis=1)
    ct = jnp.concatenate([ct, jnp.zeros_like(ct)], axis=0)
    kcat = jnp.concatenate([k_refs[u][...].astype(BF16) for u in range(pages)], axis=1)
    vcat = jnp.concatenate([v_refs[u][...].astype(BF16) for u in range(pages)], axis=1)
    s = _dot(qb_scr[...], kcat) - LOG2E * ct
    m_prev = m_scr[...]
    m_new = jnp.maximum(m_prev, jnp.max(s, axis=1, keepdims=True))
    alpha = jnp.exp2(m_prev - m_new)
    pr = jnp.exp2(s - m_new)
    l_scr[...] = alpha * l_scr[...] + jnp.sum(pr, axis=1, keepdims=True)
    acc_scr[...] = alpha * acc_scr[...] + _dot_nt(pr.astype(BF16), vcat)
    m_scr[...] = m_new


def _decode_finish(o_ref, kn_ref, vn_ref, lfn_ref, qf_scr, m_scr, l_scr, acc_scr, carry):
    c_past = jnp.concatenate([carry[...], jnp.zeros_like(carry)], axis=0)
    s_new = (jnp.sum(qf_scr[...] * kn_ref[0], axis=1, keepdims=True)
             - LOG2E * (c_past + lfn_ref[0]))
    m_prev = m_scr[...]
    m_new = jnp.maximum(m_prev, s_new)
    alpha = jnp.exp2(m_prev - m_new)
    pn = jnp.exp2(s_new - m_new)
    l_fin = alpha * l_scr[...] + pn
    acc = alpha * acc_scr[...] + pn * vn_ref[0]
    o_ref[0] = jnp.sum(jnp.where(_own_head_mask(), acc / l_fin, 0.0), axis=0, keepdims=True)


def _fox_kernel(qrow_ref, krow_ref, dseq_ref, pt_ref, q_ref, k_ref, vt_ref, b_ref, qs_ref, kn_ref, vn_ref,
                lfn_ref, ck_ref, cv_ref, clf_ref, o_ref, os_ref, qa_scr, m_scr, acc_scr,
                qf_scr, qb_scr, dm_scr, dl_scr, dacc_scr, carry, kbuf, vbuf, lfbuf, sems,
                *, layer, nq, prompt_steps, decode_steps, steps_per_seq):
    pages = DECODE_PAGES
    g = pl.program_id(0)

    def page_copies(step, slot):
        copies = []
        for u in range(pages):
            pg = pt_ref[step, u]
            copies += [pltpu.make_async_copy(ck_ref.at[layer, pg], kbuf.at[slot, u], sems.at[slot, 0]),
                       pltpu.make_async_copy(cv_ref.at[layer, pg], vbuf.at[slot, u], sems.at[slot, 1]),
                       pltpu.make_async_copy(clf_ref.at[layer, pg], lfbuf.at[slot, u], sems.at[slot, 2])]
        return copies

    @pl.when(g == 0)
    def _():
        for c in page_copies(0, 0):
            c.start()

    @pl.when(g + 1 < decode_steps)
    def _():
        for c in page_copies(g + 1, (g + 1) % 2):
            c.start()

    @pl.when(g < decode_steps)
    def _():
        for c in page_copies(g, g % 2):
            c.wait()

    slot = jnp.minimum(g, decode_steps - 1) % 2
    k_refs = [kbuf.at[slot, u] for u in range(pages)]
    v_refs = [vbuf.at[slot, u] for u in range(pages)]
    lf_refs = [lfbuf.at[slot, u] for u in range(pages)]
    i = qrow_ref[g] % nq
    j = krow_ref[g] % nq
    in_prompt = g < prompt_steps
    in_decode = g < decode_steps
    dstep = jnp.minimum(g, decode_steps - 1) % steps_per_seq
    dstate = (qb_scr, dm_scr, dl_scr, dacc_scr, carry)

    @pl.when(jnp.logical_and(in_decode, dstep == 0))
    def _():
        _decode_init(qs_ref, qf_scr, *dstate)

    @pl.when(jnp.logical_and(in_prompt, j == 0))
    def _():
        _prompt_init(q_ref, qa_scr, m_scr, acc_scr)

    @pl.when(jnp.logical_and(in_prompt, j < i))
    def _():
        _prompt_step(k_ref, vt_ref, b_ref, qa_scr, m_scr, acc_scr, False)
        _decode_step(k_refs, v_refs, lf_refs, *dstate)

    @pl.when(jnp.logical_and(in_prompt, j == i))
    def _():
        _prompt_step(k_ref, vt_ref, b_ref, qa_scr, m_scr, acc_scr, True)
        _decode_step(k_refs, v_refs, lf_refs, *dstate)
        _prompt_finish(o_ref, acc_scr)

    @pl.when(jnp.logical_not(in_prompt))
    def _():
        _decode_step(k_refs, v_refs, lf_refs, *dstate)

    @pl.when(jnp.logical_and(in_decode, dstep == steps_per_seq - 1))
    def _():
        _decode_finish(os_ref, kn_ref, vn_ref, lfn_ref, qf_scr, dm_scr, dl_scr, dacc_scr, carry)


def _fox(layer, q, k, vt, bias, n_seq, seq_len, q_s, k_new, v_new, lf_new, cache_k, cache_v, cache_lf,
         page_table, tq=512):
    nq = seq_len // tq
    pairs = [(i, j) for i in range(nq) for j in range(i + 1)]
    n_s, n_pages = page_table.shape
    page = cache_k.shape[3]
    pages = DECODE_PAGES
    steps_per_seq = n_pages // pages
    prompt_steps = n_seq * len(pairs)
    decode_steps = n_s * steps_per_seq
    n_steps = max(prompt_steps, decode_steps)
    lfn = jnp.pad(lf_new[:, :H_C], ((0, 0), (0, HEAD_ROWS - H_C))).reshape(n_s, HEAD_ROWS, 1)
    p_of = [min(g, prompt_steps - 1) for g in range(n_steps)]
    d_of = [min(g, decode_steps - 1) for g in range(n_steps)]
    qrow = jnp.asarray([(p // len(pairs)) * nq + pairs[p % len(pairs)][0] for p in p_of], jnp.int32)
    krow = jnp.asarray([(p // len(pairs)) * nq + pairs[p % len(pairs)][1] for p in p_of], jnp.int32)
    dseq = jnp.asarray([d // steps_per_seq for d in d_of], jnp.int32)
    step_pages = page_table.reshape(decode_steps, pages)[jnp.asarray(d_of, jnp.int32)]

    q_spec = pl.BlockSpec((tq, W_C), lambda g, qrow, krow, dseq, pt: (qrow[g], 0))
    k_spec = pl.BlockSpec((tq, W_C), lambda g, qrow, krow, dseq, pt: (krow[g], 0))
    vt_spec = pl.BlockSpec((W_C, tq), lambda g, qrow, krow, dseq, pt: (0, krow[g]))
    seq3 = lambda rows, w: pl.BlockSpec((1, rows, w), lambda g, qrow, krow, dseq, pt: (dseq[g], 0, 0))

    in_hbm = pl.BlockSpec(memory_space=pl.ANY)
    in_specs = [q_spec, k_spec, vt_spec, k_spec,
                seq3(1, W_C), seq3(1, W_C), seq3(1, W_C), seq3(HEAD_ROWS, 1), in_hbm, in_hbm, in_hbm]
    yc, yc_s = pl.pallas_call(
        functools.partial(_fox_kernel, layer=layer, nq=nq, prompt_steps=prompt_steps,
                          decode_steps=decode_steps, steps_per_seq=steps_per_seq),
        grid_spec=pltpu.PrefetchScalarGridSpec(
            num_scalar_prefetch=4,
            grid=(n_steps,),
            in_specs=in_specs,
            out_specs=[q_spec, seq3(1, W_C)],
            scratch_shapes=[pltpu.VMEM((H_C, tq, LANES), BF16), pltpu.VMEM((H_C, 1, tq), F32),
                            pltpu.VMEM((H_C, LANES, tq), F32),
                            pltpu.VMEM((HEAD_ROWS, W_C), F32), pltpu.VMEM((HEAD_ROWS, W_C), BF16),
                            pltpu.VMEM((HEAD_ROWS, 1), F32), pltpu.VMEM((HEAD_ROWS, 1), F32),
                            pltpu.VMEM((HEAD_ROWS, W_C), F32), pltpu.VMEM((8, 1), F32),
                            pltpu.VMEM((2, pages, W_C, page), F32), pltpu.VMEM((2, pages, W_C, page), F32),
                            pltpu.VMEM((2, pages, 8, page), F32), pltpu.SemaphoreType.DMA((2, 3))]),
        out_shape=[jax.ShapeDtypeStruct((n_seq * seq_len, W_C), BF16),
                   jax.ShapeDtypeStruct((n_s, 1, W_C), F32)],
        compiler_params=_params(1),
        name="fox",
    )(qrow, krow, dseq, step_pages, q, k, vt, bias,
      q_s.reshape(n_s, 1, W_C), k_new.reshape(n_s, 1, W_C), v_new.reshape(n_s, 1, W_C), lfn,
      cache_k, cache_v, cache_lf)
    return yc, yc_s.reshape(n_s, W_C)


def _outproj_kernel(x_ref, ya_ref, ob_ref, bonus_ref, g_ref, yc_ref, ln_ref, avg_ref, wo_ref, o_ref):
    ob = ob_ref[...]
    mu = _dot_split(ob, avg_ref[...])
    d = ob - mu
    var = _dot((d * d).astype(BF16), avg_ref[...])
    yb = (d * lax.rsqrt(var + GN_EPS) * ln_ref[0:1, :] + ln_ref[1:2, :] + bonus_ref[...]) * g_ref[...]
    y = jnp.concatenate([ya_ref[...], yb.astype(BF16), yc_ref[...]], axis=1)
    o_ref[...] = x_ref[...] + _dot(y, wo_ref[...])


def _outproj(x, ya, ob, bonus, g, yc, ln, avg_b, wo, tm=512):
    m = x.shape[0]
    tm = min(tm, m)
    row = lambda i: (i, 0)
    fix = lambda i: (0, 0)
    return pl.pallas_call(
        _outproj_kernel,
        grid=(m // tm,),
        in_specs=[pl.BlockSpec((tm, D_MODEL), row), pl.BlockSpec((tm, W_A), row),
                  pl.BlockSpec((tm, W_B), row), pl.BlockSpec((tm, W_B), row),
                  pl.BlockSpec((tm, W_B), row), pl.BlockSpec((tm, W_C), row),
                  pl.BlockSpec((8, W_B), fix), pl.BlockSpec((W_B, W_B), fix),
                  pl.BlockSpec((D_MODEL, D_MODEL), fix)],
        out_specs=pl.BlockSpec((tm, D_MODEL), row),
        out_shape=jax.ShapeDtypeStruct((m, D_MODEL), F32),
        compiler_params=_params(1),
        name="outproj",
    )(x, ya, ob, bonus, g, yc, ln, avg_b, wo)


def _block_diag_const(width, value):
    idx = jnp.arange(width) // GROUP
    return jnp.where(idx[:, None] == idx[None, :], value, 0.0).astype(BF16)


def _pad_rows(vecs, width):
    rows = [jnp.pad(v, (0, width - v.shape[0])) for v in vecs]
    rows += [jnp.zeros((width,), F32)] * (8 - len(rows))
    return jnp.stack(rows)


def _layer_weights(l, norm_g, w_ffn_in, w_ffn_out, w_in, a_ws, a_bs, a_norm_g, b_mu, b_w0, b_wB,
                   b_a0, b_aB, b_gB, b_kk, b_ka, b_rk, b_ln_g, b_ln_b, c_fb, w_o):
    zeros_lora = jnp.zeros((R_DECAY, W_B), F32)
    return dict(
        norm_g=norm_g[l],
        w_ffn_in=w_ffn_in.astype(BF16),
        w_ffn_out=w_ffn_out.astype(BF16),
        w_in=jnp.pad(w_in[l], ((0, 0), (0, IN_PROJ_PAD - IN_PROJ))).astype(BF16),
        c_fb=jnp.pad(c_fb[l], (0, LANES - H_C)).reshape(1, LANES),
        a_ws=a_ws[l],
        a_bias=jnp.repeat(a_bs[l].T, GROUP, axis=1),
        a_norm_g=a_norm_g[l],
        b_vec=_pad_rows([b_mu[l], b_w0[l], b_a0[l], b_kk[l], b_ka[l], b_rk[l]], B_PROJ),
        b_wB=jnp.concatenate([b_wB[l], zeros_lora], axis=0).astype(BF16),
        b_aB=jnp.concatenate([zeros_lora, b_aB[l]], axis=0).astype(BF16),
        b_gB=b_gB[l].astype(BF16),
        b_ln=_pad_rows([b_ln_g[l], b_ln_b[l]], W_B),
        w_o=w_o[l].astype(BF16),
        ones_b=_block_diag_const(W_B, 1.0),
        avg_b=_block_diag_const(W_B, 1.0 / GROUP),
        avg_a=_block_diag_const(W_A, 1.0 / GROUP),
    )


def _mix_prompt(wts, z, n_seq, seq_len):
    za, zb = z[0], z[1]
    ya, _ = _gmlp(za, wts["a_norm_g"], wts["a_ws"], wts["a_bias"], wts["avg_a"])
    x1, x2, ub, op, sp, gam, g, bonus = _rwkv_chunks(zb, seq_len, wts)
    ob, st = _rwkv_scan(x1, x2, ub, op, sp, gam, n_seq, seq_len)
    st = st.reshape(n_seq, N_PAIR, 2, GROUP, 2, GROUP)
    wkv = jnp.stack([st[:, :, 0, :, 0, :], st[:, :, 1, :, 1, :]], axis=2)
    wkv = wkv.reshape(n_seq, H_B, GROUP, GROUP).transpose(0, 1, 3, 2)
    return ya, ob.reshape(n_seq * seq_len, W_B), bonus, g, wkv


def _mix_sample(wts, z, shift0, wkv0):
    za, zb = z[0], z[1]
    n = za.shape[0]
    za_pad = jnp.pad(za[:, None, :], ((0, 0), (0, CHUNK_A - 1), (0, 0))).reshape(n * CHUNK_A, A_PROJ)
    ya, va = _gmlp(za_pad, wts["a_norm_g"], wts["a_ws"], wts["a_bias"], wts["avg_a"])
    ya = ya.reshape(n, CHUNK_A, W_A)[:, 0]
    va = va.reshape(n, CHUNK_A, W_A)[:, 0]
    r, lw, k2, vb, kk, beta, g, bonus = _rwkv_prep(zb, shift0, wts)
    ob, wkv = _rwkv_step(wkv0, r, lw, k2, kk, beta, vb)
    return ya, ob, bonus, g, wkv, va


def kernel(x_prompt, x_sample, cache_k, cache_v, cache_logf, state_wkv, state_shift, page_table,
           norm_g, w_ffn_in, w_ffn_out, w_in, a_ws, a_bs, a_norm_g, b_mu, b_w0, b_wB, b_a0, b_aB,
           b_gB, b_kk, b_ka, b_rk, b_ln_g, b_ln_b, c_fb, w_o, final_norm):
    n_p, seq_len, _ = x_prompt.shape
    n_s = x_sample.shape[0]
    depth = norm_g.shape[0]
    n_phys, page = cache_k.shape[1], cache_k.shape[2]
    ck = jnp.transpose(cache_k, (0, 1, 3, 4, 2)).reshape(depth, n_phys, W_C, page)
    cv = jnp.transpose(cache_v, (0, 1, 3, 4, 2)).reshape(depth, n_phys, W_C, page)
    clf = jnp.pad(jnp.transpose(cache_logf, (0, 1, 3, 2)), ((0, 0), (0, 0), (0, 8 - H_C), (0, 0)))
    xp = x_prompt.reshape(n_p * seq_len, D_MODEL)
    xs = x_sample.reshape(n_s, D_MODEL)
    outs = {name: [] for name in ("kp", "vp", "lfp", "wkvp", "shp", "ks", "vs", "lfs", "wkvs", "shs", "va")}
    for l in range(depth):
        wts = _layer_weights(l, norm_g, w_ffn_in, w_ffn_out, w_in, a_ws, a_bs, a_norm_g, b_mu, b_w0,
                             b_wB, b_a0, b_aB, b_gB, b_kk, b_ka, b_rk, b_ln_g, b_ln_b, c_fb, w_o)
        last = l == depth - 1
        fin = final_norm if last else None

        xp = _ffn(xp, wts["norm_g"][0], wts["w_ffn_in"], wts["w_ffn_out"], (l, 0))
        xs = _ffn(xs, wts["norm_g"][0], wts["w_ffn_in"], wts["w_ffn_out"], (l, 0))
        zp = _inproj(xp, wts["norm_g"][1], wts["w_in"], wts["c_fb"], seq_len=seq_len)
        zs = _inproj(xs, wts["norm_g"][1], wts["w_in"], wts["c_fb"])
        ya, ob, bonus, g, wkvp = _mix_prompt(wts, zp, n_p, seq_len)
        ya_s, ob_s, bonus_s, g_s, wkvs, va = _mix_sample(wts, zs, state_shift[l], state_wkv[l])
        yc, yc_s = _fox(l, zp[2], zp[6], zp[7], _cumsum(zp[3], n_p, seq_len), n_p, seq_len,
                        zs[2], zs[4], zs[5], zs[3], ck, cv, clf, page_table)
        xp = _outproj(xp, ya, ob, bonus, g, yc, wts["b_ln"], wts["avg_b"], wts["w_o"])
        xs = _outproj(xs, ya_s, ob_s, bonus_s, g_s, yc_s.astype(BF16), wts["b_ln"], wts["avg_b"], wts["w_o"])
        xp = _ffn(xp, wts["norm_g"][2], wts["w_ffn_in"], wts["w_ffn_out"], (l, 1), final_g=fin)
        xs = _ffn(xs, wts["norm_g"][2], wts["w_ffn_in"], wts["w_ffn_out"], (l, 1), final_g=fin)
        outs["kp"].append(zp[4].reshape(n_p, H_C, GROUP, seq_len).transpose(0, 3, 1, 2))
        outs["vp"].append(zp[5].reshape(n_p, H_C, GROUP, seq_len).transpose(0, 3, 1, 2))
        outs["lfp"].append(zp[3][:, :H_C].reshape(n_p, seq_len, H_C))
        outs["wkvp"].append(wkvp)
        outs["shp"].append(zp[1].reshape(n_p, seq_len, B_PROJ)[:, -1])
        outs["ks"].append(zs[4].reshape(n_s, 1, H_C, GROUP))
        outs["vs"].append(zs[5].reshape(n_s, 1, H_C, GROUP))
        outs["lfs"].append(zs[3][:, :H_C].reshape(n_s, 1, H_C))
        outs["wkvs"].append(wkvs)
        outs["shs"].append(zs[1])
        outs["va"].append(va.reshape(n_s, 1, W_A))
    st = lambda name: jnp.stack(outs[name])
    return (xp.reshape(n_p, seq_len, D_MODEL), xs.reshape(n_s, 1, D_MODEL),
            st("kp"), st("vp"), st("lfp"), st("wkvp"), st("shp"),
            st("ks"), st("vs"), st("lfs"), st("wkvs"), st("shs"), st("va"))
```
